```python
import math
import jax, jax.numpy as jnp
from jax import lax
import numpy as np

D_MODEL = 2048
BATCH = 4
SEQ = 4096
DEPTH = 1

HEAD_DIM = 128
A_Q_HEADS = 8
A_KV_HEADS = 2
A_GROUP = A_Q_HEADS // A_KV_HEADS
WINDOW = 128
A_BLOCK = 128
B_HEADS = 8
GRID_W = 64
WIN_H = 8
WIN_W = 16
Q_COL_BLOCK = 16
K_COL_SPAN = Q_COL_BLOCK + WIN_W
D_FF = 5632
CONV_W = 3
ROPE_THETA = 10000.0
EPS = 1e-6
NEG = -1e30

A_Q_DIM = A_Q_HEADS * HEAD_DIM
A_KV_DIM = A_KV_HEADS * HEAD_DIM
B_DIM = B_HEADS * HEAD_DIM
IN_SPLITS = [A_Q_DIM, A_KV_DIM, A_KV_DIM, B_DIM, B_DIM, B_DIM, D_MODEL, D_MODEL]
IN_COLS = sum(IN_SPLITS)

kernel_name = "hybrid_window_gqa_natten_convglu"


def rmsnorm(x, g):
    xf = x.astype(jnp.float32)
    y = xf * lax.rsqrt(jnp.mean(xf * xf, axis=-1, keepdims=True) + EPS)
    return (y * g.astype(jnp.float32)).astype(x.dtype)


def rope(x, pos):
    half = x.shape[-1] // 2
    inv_freq = ROPE_THETA ** (-jnp.arange(half, dtype=jnp.float32) * (2.0 / x.shape[-1]))
    ang = pos[:, None] * inv_freq[None, :]
    cos = jnp.cos(ang)[None, :, None, :]
    sin = jnp.sin(ang)[None, :, None, :]
    xf = x.astype(jnp.float32)
    x1, x2 = xf[..., :half], xf[..., half:]
    return jnp.concatenate([x1 * cos - x2 * sin, x2 * cos + x1 * sin], axis=-1).astype(x.dtype)


def window_attention(q, k, v, sink):
    b, s_len, _, d = q.shape
    nb = s_len // A_BLOCK
    qb = q.reshape(b, nb, A_BLOCK, A_KV_HEADS, A_GROUP, d)
    pad = ((0, 0), (A_BLOCK, A_BLOCK), (0, 0), (0, 0))
    kp = jnp.pad(k, pad).reshape(b, nb + 2, A_BLOCK, A_KV_HEADS, d)
    vp = jnp.pad(v, pad).reshape(b, nb + 2, A_BLOCK, A_KV_HEADS, d)
    kb = jnp.concatenate([kp[:, 0:nb], kp[:, 1:nb + 1], kp[:, 2:nb + 2]], axis=2)
    vb = jnp.concatenate([vp[:, 0:nb], vp[:, 1:nb + 1], vp[:, 2:nb + 2]], axis=2)
    s = jnp.einsum('bnqhgd,bnkhd->bnhgqk', qb, kb,
                   preferred_element_type=jnp.float32) * (1.0 / math.sqrt(d))
    qpos = np.arange(nb)[:, None] * A_BLOCK + np.arange(A_BLOCK)[None, :]
    kpos = (np.arange(nb)[:, None] - 1) * A_BLOCK + np.arange(3 * A_BLOCK)[None, :]
    valid = ((np.abs(qpos[:, :, None] - kpos[:, None, :]) <= WINDOW)
             & (kpos >= 0)[:, None, :] & (kpos < s_len)[:, None, :])
    s = jnp.where(jnp.asarray(valid)[None, :, None, None], s, NEG)
    sink_b = sink.astype(jnp.float32).reshape(A_KV_HEADS, A_GROUP)[None, None, :, :, None, None]
    m = jnp.maximum(jnp.max(s, axis=-1, keepdims=True), sink_b)
    p = jnp.exp(s - m)
    p = p / (jnp.sum(p, axis=-1, keepdims=True) + jnp.exp(sink_b - m))
    o = jnp.einsum('bnhgqk,bnkhd->bnqhgd', p.astype(v.dtype), vb)
    return o.reshape(b, s_len, A_Q_HEADS * d)


def neighbourhood_attention(q, k, v, rpb):
    b, s_len, h, d = q.shape
    rows = s_len // GRID_W
    kh = min(WIN_H, rows)
    rb = kh
    nrb = -(-rows // rb)
    rows_p = nrb * rb
    kspan = min(rb + kh, rows)
    ncb = GRID_W // Q_COL_BLOCK
    nq = rb * Q_COL_BLOCK
    nk = kspan * K_COL_SPAN

    qg = jnp.pad(q.reshape(b, rows, GRID_W, h, d), ((0, 0), (0, rows_p - rows), (0, 0), (0, 0), (0, 0)))
    qb = qg.reshape(b, nrb, rb, ncb, Q_COL_BLOCK, h, d).transpose(0, 1, 3, 2, 4, 5, 6)
    qb = qb.reshape(b, nrb, ncb, nq, h, d)

    r0 = np.arange(nrb) * rb
    krows = np.clip(r0 - kh // 2, 0, rows - kspan)[:, None] + np.arange(kspan)[None, :]
    c0 = np.arange(ncb) * Q_COL_BLOCK
    kcols = np.clip(c0 - WIN_W // 2, 0, GRID_W - K_COL_SPAN)[:, None] + np.arange(K_COL_SPAN)[None, :]

    kg = k.reshape(b, rows, GRID_W, h, d)
    vg = v.reshape(b, rows, GRID_W, h, d)
    ridx = krows[:, None, :, None]
    cidx = kcols[None, :, None, :]
    kb = kg[:, ridx, cidx].reshape(b, nrb, ncb, nk, h, d)
    vb = vg[:, ridx, cidx].reshape(b, nrb, ncb, nk, h, d)

    shp_q = (nrb, ncb, rb, Q_COL_BLOCK)
    shp_k = (nrb, ncb, kspan, K_COL_SPAN)
    qr = np.broadcast_to((r0[:, None] + np.arange(rb)[None, :])[:, None, :, None], shp_q).reshape(nrb, ncb, nq)
    qc = np.broadcast_to((c0[:, None] + np.arange(Q_COL_BLOCK)[None, :])[None, :, None, :], shp_q).reshape(nrb, ncb, nq)
    kr = np.broadcast_to(krows[:, None, :, None], shp_k).reshape(nrb, ncb, nk)
    kc = np.broadcast_to(kcols[None, :, None, :], shp_k).reshape(nrb, ncb, nk)
    qr, qc = qr[..., :, None], qc[..., :, None]
    kr, kc = kr[..., None, :], kc[..., None, :]
    rs = np.clip(qr - kh // 2, 0, rows - kh)
    cs = np.clip(qc - WIN_W // 2, 0, GRID_W - WIN_W)
    valid = (kr >= rs) & (kr < rs + kh) & (kc >= cs) & (kc < cs + WIN_W)
    idx_r = np.clip(kr - qr + WIN_H - 1, 0, 2 * WIN_H - 2).astype(np.int32)
    idx_c = np.clip(kc - qc + WIN_W - 1, 0, 2 * WIN_W - 2).astype(np.int32)
    bias = rpb.astype(jnp.float32)[:, idx_r, idx_c].transpose(1, 2, 0, 3, 4)

    s = jnp.einsum('bncqhd,bnckhd->bnchqk', qb, kb,
                   preferred_element_type=jnp.float32) * (1.0 / math.sqrt(d))
    s = jnp.where(jnp.asarray(valid)[None, :, :, None], s + bias[None], NEG)
    p = jax.nn.softmax(s, axis=-1)
    o = jnp.einsum('bnchqk,bnckhd->bncqhd', p.astype(v.dtype), vb)
    o = o.reshape(b, nrb, ncb, rb, Q_COL_BLOCK, h, d).transpose(0, 1, 3, 2, 4, 5, 6)
    o = o.reshape(b, rows_p, GRID_W, h * d)[:, :rows]
    return o.reshape(b, s_len, h * d)


def setup_inputs(seed: int = 0) -> dict:
    key = jax.random.key(seed)
    ks = jax.random.split(key, 20)
    f32 = jnp.float32

    def nrm(k, shape, scale):
        return jax.random.normal(k, shape, f32) * scale

    L = DEPTH
    return {
        "x": jax.random.normal(ks[0], (BATCH, SEQ, D_MODEL), f32),
        "norm_mix": 1.0 + nrm(ks[1], (L, D_MODEL), 0.02),
        "w_in": nrm(ks[2], (L, D_MODEL, IN_COLS), D_MODEL ** -0.5),
        "a_q_norm": 1.0 + nrm(ks[3], (L, HEAD_DIM), 0.02),
        "a_k_norm": 1.0 + nrm(ks[4], (L, HEAD_DIM), 0.02),
        "a_sink": nrm(ks[5], (L, A_Q_HEADS), 0.5),
        "b_q_norm": 1.0 + nrm(ks[6], (L, HEAD_DIM), 0.02),
        "b_k_norm": 1.0 + nrm(ks[7], (L, HEAD_DIM), 0.02),
        "b_rpb": nrm(ks[8], (L, B_HEADS, 2 * WIN_H - 1, 2 * WIN_W - 1), 0.1),
        "w_branch_a": nrm(ks[9], (L, A_Q_DIM, D_MODEL), A_Q_DIM ** -0.5),
        "w_branch_b": nrm(ks[10], (L, B_DIM, D_MODEL), B_DIM ** -0.5),
        "w_out": nrm(ks[11], (L, D_MODEL, D_MODEL), D_MODEL ** -0.5),
        "norm_ffn": 1.0 + nrm(ks[12], (L, D_MODEL), 0.02),
        "w_up": nrm(ks[13], (L, D_MODEL, 2 * D_FF), D_MODEL ** -0.5),
        "conv_w": nrm(ks[14], (L, CONV_W, 2 * D_FF), CONV_W ** -0.5),
        "conv_b": nrm(ks[15], (L, 2 * D_FF), 0.01),
        "w_down": nrm(ks[16], (L, D_FF, D_MODEL), D_FF ** -0.5),
    }


def reference(x, norm_mix, w_in, a_q_norm, a_k_norm, a_sink, b_q_norm, b_k_norm, b_rpb,
              w_branch_a, w_branch_b, w_out, norm_ffn, w_up, conv_w, conv_b, w_down):
    b, s_len, _ = x.shape
    pos = jnp.arange(s_len, dtype=jnp.float32)
    split_points = np.cumsum(IN_SPLITS)[:-1].tolist()
    for l in range(DEPTH):
        h = rmsnorm(x, norm_mix[l])
        proj = h @ w_in[l]
        qa, ka, va, qb, kb, vb, ga, gb = jnp.split(proj, split_points, axis=-1)

        qa = rope(rmsnorm(qa.reshape(b, s_len, A_Q_HEADS, HEAD_DIM), a_q_norm[l]), pos)
        ka = rope(rmsnorm(ka.reshape(b, s_len, A_KV_HEADS, HEAD_DIM), a_k_norm[l]), pos)
        va = va.reshape(b, s_len, A_KV_HEADS, HEAD_DIM)
        out_a = window_attention(qa, ka, va, a_sink[l])

        qb = rmsnorm(qb.reshape(b, s_len, B_HEADS, HEAD_DIM), b_q_norm[l])
        kb = rmsnorm(kb.reshape(b, s_len, B_HEADS, HEAD_DIM), b_k_norm[l])
        vb = vb.reshape(b, s_len, B_HEADS, HEAD_DIM)
        out_b = neighbourhood_attention(qb, kb, vb, b_rpb[l])

        merged = (jax.nn.sigmoid(ga) * (out_a @ w_branch_a[l])
                  + jax.nn.sigmoid(gb) * (out_b @ w_branch_b[l]))
        x = x + merged @ w_out[l]

        h2 = rmsnorm(x, norm_ffn[l])
        u = h2 @ w_up[l]
        up = jnp.pad(u, ((0, 0), (1, 1), (0, 0)))
        cw = conv_w[l]
        u = up[:, :-2] * cw[0] + up[:, 1:-1] * cw[1] + up[:, 2:] * cw[2] + conv_b[l]
        gate, val = jnp.split(u, 2, axis=-1)
        x = x + (jax.nn.silu(gate) * val) @ w_down[l]
    return x
```

```python
import functools
import math

import numpy as np
import jax
import jax.numpy as jnp
from jax import lax
from jax.experimental import pallas as pl
from jax.experimental.pallas import tpu as pltpu

D_MODEL = 2048
SEQ = 4096
HEAD_DIM = 128
A_Q_HEADS = 8
A_KV_HEADS = 2
A_GROUP = A_Q_HEADS // A_KV_HEADS
WINDOW = 128
B_HEADS = 8
GRID_W = 64
WIN_H = 8
WIN_W = 16
D_FF = 5632
ROPE_THETA = 10000.0
EPS = 1e-6
NEG = -1e30

A_Q_DIM = A_Q_HEADS * HEAD_DIM
A_KV_DIM = A_KV_HEADS * HEAD_DIM
B_DIM = B_HEADS * HEAD_DIM
IN_COLS = A_Q_DIM + 2 * A_KV_DIM + 3 * B_DIM + 2 * D_MODEL

COL_GA = 0
COL_GB = COL_GA + D_MODEL
COL_QA = COL_GB + D_MODEL
COL_QB = COL_QA + A_Q_DIM
COL_KB = COL_QB + B_DIM
COL_VB = COL_KB + B_DIM
COL_KA = COL_VB + B_DIM
COL_VA = COL_KA + A_KV_DIM

VMEM_LIMIT_BYTES = 56 * 1024 * 1024

BF16 = jnp.bfloat16
F32 = jnp.float32

IN_TM = 1024
IN_TN = 512
IN_NJ = IN_COLS // IN_TN
_J_GATE_END = COL_QA // IN_TN
_J_QA_END = COL_QB // IN_TN
_J_QB_END = COL_KB // IN_TN
_J_KB_END = COL_VB // IN_TN
_J_VB_END = COL_KA // IN_TN
HEADS_PER_TILE = IN_TN // HEAD_DIM


def _head_norm(a, gain):
    ms = jnp.mean(a * a, axis=-1, keepdims=True)
    return a * lax.rsqrt(ms + EPS) * gain


def _rope(y, cos, sin_signed):
    return y * cos + pltpu.roll(y, HEAD_DIM // 2, axis=1) * sin_signed


def _in_proj_kernel(x_ref, g_ref, w_ref, cos_ref, sin_ref, hg_ref, o_ref, h_ref):
    j = pl.program_id(1)

    @pl.when(j == 0)
    def _():
        x = x_ref[...]
        ms = jnp.mean(x * x, axis=-1, keepdims=True)
        h_ref[...] = (x * lax.rsqrt(ms + EPS) * g_ref[...]).astype(BF16)

    acc = jnp.dot(h_ref[...], w_ref[...], preferred_element_type=F32)

    def head(t):
        return acc[:, t * HEAD_DIM:(t + 1) * HEAD_DIM]

    def store_head(t, val):
        o_ref[:, t * HEAD_DIM:(t + 1) * HEAD_DIM] = val.astype(BF16)

    @pl.when(j < _J_GATE_END)
    def _():
        o_ref[...] = (0.5 * jnp.tanh(0.5 * acc) + 0.5).astype(BF16)

    @pl.when((j >= _J_GATE_END) & (j < _J_QA_END))
    def _():
        for t in range(HEADS_PER_TILE):
            store_head(t, _rope(_head_norm(head(t), hg_ref[0:1, :]), cos_ref[...], sin_ref[...]))

    @pl.when((j >= _J_QA_END) & (j < _J_QB_END))
    def _():
        for t in range(HEADS_PER_TILE):
            store_head(t, _head_norm(head(t), hg_ref[2:3, :]))

    @pl.when((j >= _J_QB_END) & (j < _J_KB_END))
    def _():
        for t in range(HEADS_PER_TILE):
            store_head(t, _head_norm(head(t), hg_ref[3:4, :]))

    @pl.when((j >= _J_KB_END) & (j < _J_VB_END))
    def _():
        o_ref[...] = acc.astype(BF16)

    @pl.when(j >= _J_VB_END)
    def _():
        for t in range(A_KV_HEADS):
            store_head(t, _rope(_head_norm(head(t), hg_ref[1:2, :]), cos_ref[...], sin_ref[...]))
        for t in range(A_KV_HEADS, 2 * A_KV_HEADS):
            store_head(t, head(t))


def _in_proj(x2, norm_g, w_bf, cos, sin_signed, head_gains):
    n = x2.shape[0]
    seq_tiles = SEQ // IN_TM
    return pl.pallas_call(
        _in_proj_kernel,
        out_shape=jax.ShapeDtypeStruct((n, IN_COLS), BF16),
        grid=(n // IN_TM, IN_NJ),
        in_specs=[
            pl.BlockSpec((IN_TM, D_MODEL), lambda i, j: (i, 0)),
            pl.BlockSpec((1, D_MODEL), lambda i, j: (0, 0)),
            pl.BlockSpec((D_MODEL, IN_TN), lambda i, j: (0, j)),
            pl.BlockSpec((IN_TM, HEAD_DIM), lambda i, j: (i % seq_tiles, 0)),
            pl.BlockSpec((IN_TM, HEAD_DIM), lambda i, j: (i % seq_tiles, 0)),
            pl.BlockSpec((8, HEAD_DIM), lambda i, j: (0, 0)),
        ],
        out_specs=pl.BlockSpec((IN_TM, IN_TN), lambda i, j: (i, j)),
        scratch_shapes=[pltpu.VMEM((IN_TM, D_MODEL), BF16)],
        compiler_params=pltpu.CompilerParams(
            dimension_semantics=("arbitrary", "arbitrary"),
            vmem_limit_bytes=VMEM_LIMIT_BYTES),
        name="in_proj",
    )(x2, norm_g, w_bf, cos, sin_signed, head_gains)


A_TQ = 512
A_BLK = WINDOW
A_NB = A_TQ // A_BLK
A_CHUNKS = SEQ // A_TQ


def _attn_a_kernel(sink_ref, q_ref, kp_ref, km_ref, kn_ref, vp_ref, vm_ref, vn_ref, o_ref):
    c = pl.program_id(1)
    k_all = jnp.concatenate([kp_ref[...], km_ref[...], kn_ref[...]], axis=0)
    v_all = jnp.concatenate([vp_ref[...], vm_ref[...], vn_ref[...]], axis=0)
    qq = lax.broadcasted_iota(jnp.int32, (A_BLK, 3 * A_BLK), 0)
    kk = lax.broadcasted_iota(jnp.int32, (A_BLK, 3 * A_BLK), 1)
    d = kk - qq
    band = (d >= 0) & (d <= 2 * WINDOW)
    for n in range(A_NB):
        valid = band
        if n == 0:
            valid = valid & ((kk >= A_BLK) | (c > 0))
        if n == A_NB - 1:
            valid = valid & ((kk < 2 * A_BLK) | (c < A_CHUNKS - 1))
        for h in range(A_KV_HEADS):
            k = k_all[n * A_BLK:(n + 3) * A_BLK, h * HEAD_DIM:(h + 1) * HEAD_DIM]
            v = v_all[n * A_BLK:(n + 3) * A_BLK, h * HEAD_DIM:(h + 1) * HEAD_DIM]
            heads = [h * A_GROUP + g for g in range(A_GROUP)]
            qs = jnp.concatenate(
                [q_ref[n * A_BLK:(n + 1) * A_BLK, t * HEAD_DIM:(t + 1) * HEAD_DIM] for t in heads],
                axis=0)
            s = lax.dot_general(qs, k, (((1,), (1,)), ((), ())),
                                preferred_element_type=F32)
            s = jnp.where(valid[None], s.reshape(A_GROUP, A_BLK, 3 * A_BLK), NEG)
            s = s.reshape(A_GROUP * A_BLK, 3 * A_BLK)
            sink = jnp.concatenate(
                [jnp.full((A_BLK, 1), sink_ref[t], F32) for t in heads], axis=0)
            m = jnp.maximum(jnp.max(s, axis=-1, keepdims=True), sink)
            p = jnp.exp(s - m)
            denom = jnp.sum(p, axis=-1, keepdims=True) + jnp.exp(sink - m)
            o = jnp.dot(p.astype(BF16), v, preferred_element_type=F32) / denom
            for g, t in enumerate(heads):
                o_ref[n * A_BLK:(n + 1) * A_BLK, t * HEAD_DIM:(t + 1) * HEAD_DIM] = (
                    o[g * A_BLK:(g + 1) * A_BLK].astype(BF16))


def _attn_a(proj, sink, batch):
    n = proj.shape[0]
    blk_per_seq = SEQ // A_BLK
    q_col = COL_QA // A_Q_DIM
    k_col = COL_KA // A_KV_DIM
    v_col = COL_VA // A_KV_DIM

    def prev_map(col):
        return lambda b, c, *_: (b * blk_per_seq + jnp.maximum(c * A_NB - 1, 0), col)

    def main_map(col):
        return lambda b, c, *_: (b * A_CHUNKS + c, col)

    def next_map(col):
        return lambda b, c, *_: (b * blk_per_seq + jnp.minimum(c * A_NB + A_NB, blk_per_seq - 1), col)

    halo = lambda m: pl.BlockSpec((A_BLK, A_KV_DIM), m)
    main = lambda m: pl.BlockSpec((A_TQ, A_KV_DIM), m)
    return pl.pallas_call(
        _attn_a_kernel,
        out_shape=jax.ShapeDtypeStruct((n, A_Q_DIM), BF16),
        grid_spec=pltpu.PrefetchScalarGridSpec(
            num_scalar_prefetch=1,
            grid=(batch, A_CHUNKS),
            in_specs=[
                pl.BlockSpec((A_TQ, A_Q_DIM), main_map(q_col)),
                halo(prev_map(k_col)), main(main_map(k_col)), halo(next_map(k_col)),
                halo(prev_map(v_col)), main(main_map(v_col)), halo(next_map(v_col)),
            ],
            out_specs=pl.BlockSpec((A_TQ, A_Q_DIM), lambda b, c, *_: (b * A_CHUNKS + c, 0)),
        ),
        compiler_params=pltpu.CompilerParams(
            dimension_semantics=("arbitrary", "arbitrary"),
            vmem_limit_bytes=VMEM_LIMIT_BYTES),
        name="attn_a",
    )(sink, proj, proj, proj, proj, proj, proj, proj)


GRID_ROWS = SEQ // GRID_W
B_QR = 4
B_QC = 32
B_KR = B_QR + WIN_H
B_KC = B_QC + WIN_W
B_NRB = GRID_ROWS // B_QR
B_NCB = GRID_W // B_QC
B_NQ = B_QR * B_QC
B_NK = B_KR * B_KC
B_KBLOCKS = B_KR // B_QR
B_KS_MAX = B_NRB - B_KBLOCKS
B_KC0 = tuple(int(np.clip(cb * B_QC - WIN_W // 2, 0, GRID_W - B_KC)) for cb in range(B_NCB))


def _attn_b_kernel(q_ref, k0_ref, k1_ref, k2_ref, v0_ref, v1_ref, v2_ref, bias_ref, o_ref):
    k_refs = (k0_ref, k1_ref, k2_ref)
    v_refs = (v0_ref, v1_ref, v2_ref)
    for cb in range(B_NCB):
        qc = slice(cb * B_QC, (cb + 1) * B_QC)
        kc = slice(B_KC0[cb], B_KC0[cb] + B_KC)
        for h in range(B_HEADS):
            hd = slice(h * HEAD_DIM, (h + 1) * HEAD_DIM)
            q = q_ref[:, qc, hd].reshape(B_NQ, HEAD_DIM)
            k = jnp.concatenate([r[:, kc, hd] for r in k_refs], axis=0).reshape(B_NK, HEAD_DIM)
            v = jnp.concatenate([r[:, kc, hd] for r in v_refs], axis=0).reshape(B_NK, HEAD_DIM)
            s = lax.dot_general(q, k, (((1,), (1,)), ((), ())),
                                preferred_element_type=F32) + bias_ref[0, cb, h]
            m = jnp.max(s, axis=-1, keepdims=True)
            p = jnp.exp(s - m)
            denom = jnp.sum(p, axis=-1, keepdims=True)
            o = jnp.dot(p.astype(BF16), v, preferred_element_type=F32) / denom
            o_ref[:, qc, hd] = o.reshape(B_QR, B_QC, HEAD_DIM).astype(BF16)


def _attn_b_bias(rpb):
    tabs_r, tabs_c, valid = [], [], []
    for rb in (0, 1, B_NRB - 1):
        ks = int(np.clip(rb - 1, 0, B_KS_MAX))
        qr = rb * B_QR + np.arange(B_QR)
        kr = ks * B_QR + np.arange(B_KR)
        rs = np.clip(qr - WIN_H // 2, 0, GRID_ROWS - WIN_H)
        row_ok = (kr[None, :] >= rs[:, None]) & (kr[None, :] < rs[:, None] + WIN_H)
        idx_r = np.clip(kr[None, :] - qr[:, None] + WIN_H - 1, 0, 2 * WIN_H - 2)
        rows_r, rows_c, rows_v = [], [], []
        for cb in range(B_NCB):
            qc = cb * B_QC + np.arange(B_QC)
            kc = B_KC0[cb] + np.arange(B_KC)
            cs = np.clip(qc - WIN_W // 2, 0, GRID_W - WIN_W)
            col_ok = (kc[None, :] >= cs[:, None]) & (kc[None, :] < cs[:, None] + WIN_W)
            idx_c = np.clip(kc[None, :] - qc[:, None] + WIN_W - 1, 0, 2 * WIN_W - 2)
            shape = (B_QR, B_QC, B_KR, B_KC)
            ok = row_ok[:, None, :, None] & col_ok[None, :, None, :]
            rows_v.append(np.broadcast_to(ok, shape).reshape(B_NQ, B_NK))
            rows_r.append(np.broadcast_to(idx_r[:, None, :, None], shape).reshape(B_NQ, B_NK))
            rows_c.append(np.broadcast_to(idx_c[None, :, None, :], shape).reshape(B_NQ, B_NK))
        tabs_r.append(np.stack(rows_r))
        tabs_c.append(np.stack(rows_c))
        valid.append(np.stack(rows_v))
    idx_r = np.stack(tabs_r).astype(np.int32)
    idx_c = np.stack(tabs_c).astype(np.int32)
    ok = np.stack(valid)
    bias = rpb.astype(F32)[:, idx_r, idx_c]
    bias = jnp.where(jnp.asarray(ok)[None], bias, NEG)
    return bias.transpose(1, 2, 0, 3, 4)


def _attn_b(proj3, bias, batch):
    rows = proj3.shape[0]
    q_col = COL_QB // B_DIM
    k_col = COL_KB // B_DIM
    v_col = COL_VB // B_DIM

    def kv_map(col, t):
        return lambda b, r: (b * B_NRB + jnp.clip(r - 1, 0, B_KS_MAX) + t, 0, col)

    def pattern(r):
        return jnp.where(r == 0, 0, jnp.where(r == B_NRB - 1, 2, 1))

    blk = lambda m: pl.BlockSpec((B_QR, GRID_W, B_DIM), m)
    return pl.pallas_call(
        _attn_b_kernel,
        out_shape=jax.ShapeDtypeStruct((rows, GRID_W, B_DIM), BF16),
        grid=(batch, B_NRB),
        in_specs=[
            blk(lambda b, r: (b * B_NRB + r, 0, q_col)),
            blk(kv_map(k_col, 0)), blk(kv_map(k_col, 1)), blk(kv_map(k_col, 2)),
            blk(kv_map(v_col, 0)), blk(kv_map(v_col, 1)), blk(kv_map(v_col, 2)),
            pl.BlockSpec((1, B_NCB, B_HEADS, B_NQ, B_NK), lambda b, r: (pattern(r), 0, 0, 0, 0)),
        ],
        out_specs=blk(lambda b, r: (b * B_NRB + r, 0, 0)),
        compiler_params=pltpu.CompilerParams(
            dimension_semantics=("arbitrary", "arbitrary"),
            vmem_limit_bytes=VMEM_LIMIT_BYTES),
        name="attn_b",
    )(proj3, proj3, proj3, proj3, proj3, proj3, proj3, bias)


MG_TM = 512


def _merge_kernel(x_ref, oa_ref, ob_ref, ga_ref, gb_ref, wa_ref, wb_ref, wo_ref, g_ref,
                  x1_ref, h2_ref):
    a = jnp.dot(oa_ref[...], wa_ref[...], preferred_element_type=F32)
    b = jnp.dot(ob_ref[...], wb_ref[...], preferred_element_type=F32)
    merged = ga_ref[...].astype(F32) * a + gb_ref[...].astype(F32) * b
    y = jnp.dot(merged.astype(BF16), wo_ref[...], preferred_element_type=F32)
    x1 = x_ref[...] + y
    x1_ref[...] = x1
    ms = jnp.mean(x1 * x1, axis=-1, keepdims=True)
    h2_ref[...] = (x1 * lax.rsqrt(ms + EPS) * g_ref[...]).astype(BF16)


def _merge(x2, out_a, out_b, proj, wa, wb, wo, norm_g):
    n = x2.shape[0]
    resident = lambda shape: pl.BlockSpec(shape, lambda i: (0, 0), pipeline_mode=pl.Buffered(1))
    return pl.pallas_call(
        _merge_kernel,
        out_shape=(jax.ShapeDtypeStruct((n, D_MODEL), F32),
                   jax.ShapeDtypeStruct((n, D_MODEL), BF16)),
        grid=(n // MG_TM,),
        in_specs=[
            pl.BlockSpec((MG_TM, D_MODEL), lambda i: (i, 0)),
            pl.BlockSpec((MG_TM, A_Q_DIM), lambda i: (i, 0)),
            pl.BlockSpec((MG_TM, B_DIM), lambda i: (i, 0)),
            pl.BlockSpec((MG_TM, D_MODEL), lambda i: (i, COL_GA // D_MODEL)),
            pl.BlockSpec((MG_TM, D_MODEL), lambda i: (i, COL_GB // D_MODEL)),
            resident((A_Q_DIM, D_MODEL)),
            resident((B_DIM, D_MODEL)),
            resident((D_MODEL, D_MODEL)),
            resident((1, D_MODEL)),
        ],
        out_specs=(pl.BlockSpec((MG_TM, D_MODEL), lambda i: (i, 0)),
                   pl.BlockSpec((MG_TM, D_MODEL), lambda i: (i, 0))),
        compiler_params=pltpu.CompilerParams(
            dimension_semantics=("arbitrary",),
            vmem_limit_bytes=VMEM_LIMIT_BYTES),
        name="merge",
    )(x2, out_a, out_b, proj, proj, wa, wb, wo, norm_g)


UP_TM = 1024
UP_TN = 512
UP_NJ = D_FF // UP_TN
HALO = 16
UP_ROWS = UP_TM + 2 * HALO


def _ffn_up_kernel(hp_ref, hm_ref, hn_ref, wg_ref, wv_ref, cwg_ref, cwv_ref, cbg_ref, cbv_ref,
                   o_ref, lhs_ref):
    i = pl.program_id(0)
    j = pl.program_id(1)
    tiles_per_seq = SEQ // UP_TM

    @pl.when(j == 0)
    def _():
        first = (i % tiles_per_seq) == 0
        last = (i % tiles_per_seq) == tiles_per_seq - 1
        lhs_ref[0:HALO, :] = jnp.where(first, jnp.zeros_like(hp_ref[...]), hp_ref[...])
        lhs_ref[HALO:HALO + UP_TM, :] = hm_ref[...]
        lhs_ref[HALO + UP_TM:, :] = jnp.where(last, jnp.zeros_like(hn_ref[...]), hn_ref[...])

    lhs = lhs_ref[...]

    def conv(w_ref, cw_ref, cb_ref):
        u = jnp.dot(lhs, w_ref[...], preferred_element_type=F32)
        below = pltpu.roll(u, 1, axis=0)[HALO:HALO + UP_TM]
        above = pltpu.roll(u, UP_ROWS - 1, axis=0)[HALO:HALO + UP_TM]
        mid = u[HALO:HALO + UP_TM]
        return below * cw_ref[0:1, :] + mid * cw_ref[1:2, :] + above * cw_ref[2:3, :] + cb_ref[...]

    gate = conv(wg_ref, cwg_ref, cbg_ref)
    val = conv(wv_ref, cwv_ref, cbv_ref)
    silu = gate * (0.5 * jnp.tanh(0.5 * gate) + 0.5)
    o_ref[...] = (silu * val).astype(BF16)


def _ffn_up(h2, w_up, conv_w, conv_b):
    n = h2.shape[0]
    halo_blocks = UP_TM // HALO
    last_halo = n // HALO - 1
    return pl.pallas_call(
        _ffn_up_kernel,
        out_shape=jax.ShapeDtypeStruct((n, D_FF), BF16),
        grid=(n // UP_TM, UP_NJ),
        in_specs=[
            pl.BlockSpec((HALO, D_MODEL), lambda i, j: (jnp.maximum(i * halo_blocks - 1, 0), 0)),
            pl.BlockSpec((UP_TM, D_MODEL), lambda i, j: (i, 0)),
            pl.BlockSpec((HALO, D_MODEL),
                         lambda i, j: (jnp.minimum((i + 1) * halo_blocks, last_halo), 0)),
            pl.BlockSpec((D_MODEL, UP_TN), lambda i, j: (0, j)),
            pl.BlockSpec((D_MODEL, UP_TN), lambda i, j: (0, UP_NJ + j)),
            pl.BlockSpec((3, UP_TN), lambda i, j: (0, j)),
            pl.BlockSpec((3, UP_TN), lambda i, j: (0, UP_NJ + j)),
            pl.BlockSpec((1, UP_TN), lambda i, j: (0, j)),
            pl.BlockSpec((1, UP_TN), lambda i, j: (0, UP_NJ + j)),
        ],
        out_specs=pl.BlockSpec((UP_TM, UP_TN), lambda i, j: (i, j)),
        scratch_shapes=[pltpu.VMEM((UP_ROWS, D_MODEL), BF16)],
        compiler_params=pltpu.CompilerParams(
            dimension_semantics=("arbitrary", "arbitrary"),
            vmem_limit_bytes=VMEM_LIMIT_BYTES),
        name="ffn_up",
    )(h2, h2, h2, w_up, w_up, conv_w, conv_w, conv_b, conv_b)


DN_TM = 1024
DN_TN = 512


def _ffn_down_kernel(a_ref, w_ref, x_ref, o_ref):
    o_ref[...] = x_ref[...] + jnp.dot(a_ref[...], w_ref[...], preferred_element_type=F32)


def _ffn_down(act, w_down, x1):
    n = act.shape[0]
    return pl.pallas_call(
        _ffn_down_kernel,
        out_shape=jax.ShapeDtypeStruct((n, D_MODEL), F32),
        grid=(n // DN_TM, D_MODEL // DN_TN),
        in_specs=[
            pl.BlockSpec((DN_TM, D_FF), lambda i, j: (i, 0)),
            pl.BlockSpec((D_FF, DN_TN), lambda i, j: (0, j)),
            pl.BlockSpec((DN_TM, DN_TN), lambda i, j: (i, j)),
        ],
        out_specs=pl.BlockSpec((DN_TM, DN_TN), lambda i, j: (i, j)),
        compiler_params=pltpu.CompilerParams(
            dimension_semantics=("arbitrary", "arbitrary"),
            vmem_limit_bytes=VMEM_LIMIT_BYTES),
        name="ffn_down",
    )(act, w_down, x1)


def _rope_tables():
    half = HEAD_DIM // 2
    pos = jnp.arange(SEQ, dtype=F32)
    inv_freq = ROPE_THETA ** (-jnp.arange(half, dtype=F32) * (2.0 / HEAD_DIM))
    ang = pos[:, None] * inv_freq[None, :]
    cos, sin = jnp.cos(ang), jnp.sin(ang)
    return jnp.concatenate([cos, cos], axis=-1), jnp.concatenate([-sin, sin], axis=-1)


def _permute_in_cols(w):
    splits = np.cumsum([A_Q_DIM, A_KV_DIM, A_KV_DIM, B_DIM, B_DIM, B_DIM, D_MODEL])
    qa, ka, va, qb, kb, vb, ga, gb = jnp.split(w, splits.tolist(), axis=-1)
    return jnp.concatenate([ga, gb, qa, qb, kb, vb, ka, va], axis=-1)


def kernel(x, norm_mix, w_in, a_q_norm, a_k_norm, a_sink, b_q_norm, b_k_norm, b_rpb,
           w_branch_a, w_branch_b, w_out, norm_ffn, w_up, conv_w, conv_b, w_down):
    batch, seq, d_model = x.shape
    assert (seq, d_model) == (SEQ, D_MODEL)
    n = batch * seq
    cos, sin_signed = _rope_tables()
    scale = 1.0 / math.sqrt(HEAD_DIM)
    x2 = x.reshape(n, d_model)
    for l in range(norm_mix.shape[0]):
        head_gains = jnp.concatenate([
            a_q_norm[l][None] * scale, a_k_norm[l][None], b_q_norm[l][None] * scale,
            b_k_norm[l][None], jnp.zeros((4, HEAD_DIM), F32)], axis=0)
        proj = _in_proj(x2, norm_mix[l][None], _permute_in_cols(w_in[l]).astype(BF16),
                        cos, sin_signed, head_gains)
        out_a = _attn_a(proj, a_sink[l], batch)
        out_b = _attn_b(proj.reshape(n // GRID_W, GRID_W, IN_COLS), _attn_b_bias(b_rpb[l]), batch)
        x1, h2 = _merge(x2, out_a, out_b.reshape(n, B_DIM), proj,
                        w_branch_a[l].astype(BF16), w_branch_b[l].astype(BF16),
                        w_out[l].astype(BF16), norm_ffn[l][None])
        act = _ffn_up(h2, w_up[l].astype(BF16), conv_w[l], conv_b[l][None])
        x2 = _ffn_down(act, w_down[l].astype(BF16), x1)
    return x2.reshape(batch, seq, d_model)
```

```python
import functools
import math

import numpy as np
import jax
import jax.numpy as jnp
from jax import lax
from jax.experimental import pallas as pl
from jax.experimental.pallas import tpu as pltpu

D_MODEL = 2048
SEQ = 4096
HEAD_DIM = 128
A_Q_HEADS = 8
A_KV_HEADS = 2
A_GROUP = A_Q_HEADS // A_KV_HEADS
WINDOW = 128
B_HEADS = 8
GRID_W = 64
WIN_H = 8
WIN_W = 16
D_FF = 5632
ROPE_THETA = 10000.0
EPS = 1e-6
NEG = -1e30

A_Q_DIM = A_Q_HEADS * HEAD_DIM
A_KV_DIM = A_KV_HEADS * HEAD_DIM
B_DIM = B_HEADS * HEAD_DIM
IN_COLS = A_Q_DIM + 2 * A_KV_DIM + 3 * B_DIM + 2 * D_MODEL

COL_GA = 0
COL_GB = COL_GA + D_MODEL
COL_QA = COL_GB + D_MODEL
COL_QB = COL_QA + A_Q_DIM
COL_KB = COL_QB + B_DIM
COL_VB = COL_KB + B_DIM
COL_KA = COL_VB + B_DIM
COL_VA = COL_KA + A_KV_DIM

VMEM_LIMIT_BYTES = 56 * 1024 * 1024

BF16 = jnp.bfloat16
F32 = jnp.float32

IN_TM = 1024
IN_TN = 512
IN_NJ = IN_COLS // IN_TN
_J_GATE_END = COL_QA // IN_TN
_J_QA_END = COL_QB // IN_TN
_J_QB_END = COL_KB // IN_TN
_J_KB_END = COL_VB // IN_TN
_J_VB_END = COL_KA // IN_TN
HEADS_PER_TILE = IN_TN // HEAD_DIM


def _head_norm(a, gain):
    ms = jnp.mean(a * a, axis=-1, keepdims=True)
    return a * lax.rsqrt(ms + EPS) * gain


def _rope(y, cos, sin_signed):
    return y * cos + pltpu.roll(y, HEAD_DIM // 2, axis=1) * sin_signed


def _in_proj_kernel(x_ref, g_ref, w_ref, cos_ref, sin_ref, hg_ref, o_ref, h_ref):
    j = pl.program_id(1)

    @pl.when(j == 0)
    def _():
        x = x_ref[...]
        ms = jnp.mean(x * x, axis=-1, keepdims=True)
        h_ref[...] = (x * lax.rsqrt(ms + EPS) * g_ref[...]).astype(BF16)

    acc = jnp.dot(h_ref[...], w_ref[...], preferred_element_type=F32)

    def head(t):
        return acc[:, t * HEAD_DIM:(t + 1) * HEAD_DIM]

    def store_head(t, val):
        o_ref[:, t * HEAD_DIM:(t + 1) * HEAD_DIM] = val.astype(BF16)

    @pl.when(j < _J_GATE_END)
    def _():
        o_ref[...] = (0.5 * jnp.tanh(0.5 * acc) + 0.5).astype(BF16)

    @pl.when((j >= _J_GATE_END) & (j < _J_QA_END))
    def _():
        for t in range(HEADS_PER_TILE):
            store_head(t, _rope(_head_norm(head(t), hg_ref[0:1, :]), cos_ref[...], sin_ref[...]))

    @pl.when((j >= _J_QA_END) & (j < _J_QB_END))
    def _():
        for t in range(HEADS_PER_TILE):
            store_head(t, _head_norm(head(t), hg_ref[2:3, :]))

    @pl.when((j >= _J_QB_END) & (j < _J_KB_END))
    def _():
        for t in range(HEADS_PER_TILE):
            store_head(t, _head_norm(head(t), hg_ref[3:4, :]))

    @pl.when((j >= _J_KB_END) & (j < _J_VB_END))
    def _():
        o_ref[...] = acc.astype(BF16)

    @pl.when(j >= _J_VB_END)
    def _():
        for t in range(A_KV_HEADS):
            store_head(t, _rope(_head_norm(head(t), hg_ref[1:2, :]), cos_ref[...], sin_ref[...]))
        for t in range(A_KV_HEADS, 2 * A_KV_HEADS):
            store_head(t, head(t))


def _in_proj(x2, norm_g, w_bf, cos, sin_signed, head_gains):
    n = x2.shape[0]
    seq_tiles = SEQ // IN_TM
    return pl.pallas_call(
        _in_proj_kernel,
        out_shape=jax.ShapeDtypeStruct((n, IN_COLS), BF16),
        grid=(n // IN_TM, IN_NJ),
        in_specs=[
            pl.BlockSpec((IN_TM, D_MODEL), lambda i, j: (i, 0)),
            pl.BlockSpec((1, D_MODEL), lambda i, j: (0, 0)),
            pl.BlockSpec((D_MODEL, IN_TN), lambda i, j: (0, j)),
            pl.BlockSpec((IN_TM, HEAD_DIM), lambda i, j: (i % seq_tiles, 0)),
            pl.BlockSpec((IN_TM, HEAD_DIM), lambda i, j: (i % seq_tiles, 0)),
            pl.BlockSpec((8, HEAD_DIM), lambda i, j: (0, 0)),
        ],
        out_specs=pl.BlockSpec((IN_TM, IN_TN), lambda i, j: (i, j)),
        scratch_shapes=[pltpu.VMEM((IN_TM, D_MODEL), BF16)],
        compiler_params=pltpu.CompilerParams(
            dimension_semantics=("arbitrary", "arbitrary"),
            vmem_limit_bytes=VMEM_LIMIT_BYTES),
        name="in_proj",
    )(x2, norm_g, w_bf, cos, sin_signed, head_gains)


A_TQ = 512
A_BLK = WINDOW
A_NB = A_TQ // A_BLK
A_CHUNKS = SEQ // A_TQ


def _attn_a_kernel(sink_ref, q_ref, kp_ref, km_ref, kn_ref, vp_ref, vm_ref, vn_ref, o_ref):
    c = pl.program_id(1)
    k_all = jnp.concatenate([kp_ref[...], km_ref[...], kn_ref[...]], axis=0)
    v_all = jnp.concatenate([vp_ref[...], vm_ref[...], vn_ref[...]], axis=0)
    qq = lax.broadcasted_iota(jnp.int32, (A_BLK, 3 * A_BLK), 0)
    kk = lax.broadcasted_iota(jnp.int32, (A_BLK, 3 * A_BLK), 1)
    d = kk - qq
    band = (d >= 0) & (d <= 2 * WINDOW)
    for n in range(A_NB):
        valid = band
        if n == 0:
            valid = valid & ((kk >= A_BLK) | (c > 0))
        if n == A_NB - 1:
            valid = valid & ((kk < 2 * A_BLK) | (c < A_CHUNKS - 1))
        for h in range(A_KV_HEADS):
            k = k_all[n * A_BLK:(n + 3) * A_BLK, h * HEAD_DIM:(h + 1) * HEAD_DIM]
            v = v_all[n * A_BLK:(n + 3) * A_BLK, h * HEAD_DIM:(h + 1) * HEAD_DIM]
            heads = [h * A_GROUP + g for g in range(A_GROUP)]
            qs = jnp.concatenate(
                [q_ref[n * A_BLK:(n + 1) * A_BLK, t * HEAD_DIM:(t + 1) * HEAD_DIM] for t in heads],
                axis=0)
            s = lax.dot_general(qs, k, (((1,), (1,)), ((), ())),
                                preferred_element_type=F32)
            s = jnp.where(valid[None], s.reshape(A_GROUP, A_BLK, 3 * A_BLK), NEG)
            s = s.reshape(A_GROUP * A_BLK, 3 * A_BLK)
            sink = jnp.concatenate(
                [jnp.full((A_BLK, 1), sink_ref[t], F32) for t in heads], axis=0)
            m = jnp.maximum(jnp.max(s, axis=-1, keepdims=True), sink)
            p = jnp.exp(s - m)
            denom = jnp.sum(p, axis=-1, keepdims=True) + jnp.exp(sink - m)
            o = jnp.dot(p.astype(BF16), v, preferred_element_type=F32) / denom
            for g, t in enumerate(heads):
                o_ref[n * A_BLK:(n + 1) * A_BLK, t * HEAD_DIM:(t + 1) * HEAD_DIM] = (
                    o[g * A_BLK:(g + 1) * A_BLK].astype(BF16))


def _attn_a(proj, sink, batch):
    n = proj.shape[0]
    blk_per_seq = SEQ // A_BLK
    q_col = COL_QA // A_Q_DIM
    k_col = COL_KA // A_KV_DIM
    v_col = COL_VA // A_KV_DIM

    def prev_map(col):
        return lambda b, c, *_: (b * blk_per_seq + jnp.maximum(c * A_NB - 1, 0), col)

    def main_map(col):
        return lambda b, c, *_: (b * A_CHUNKS + c, col)

    def next_map(col):
        return lambda b, c, *_: (b * blk_per_seq + jnp.minimum(c * A_NB + A_NB, blk_per_seq - 1), col)

    halo = lambda m: pl.BlockSpec((A_BLK, A_KV_DIM), m)
    main = lambda m: pl.BlockSpec((A_TQ, A_KV_DIM), m)
    return pl.pallas_call(
        _attn_a_kernel,
        out_shape=jax.ShapeDtypeStruct((n, A_Q_DIM), BF16),
        grid_spec=pltpu.PrefetchScalarGridSpec(
            num_scalar_prefetch=1,
            grid=(batch, A_CHUNKS),
            in_specs=[
                pl.BlockSpec((A_TQ, A_Q_DIM), main_map(q_col)),
                halo(prev_map(k_col)), main(main_map(k_col)), halo(next_map(k_col)),
                halo(prev_map(v_col)), main(main_map(v_col)), halo(next_map(v_col)),
            ],
            out_specs=pl.BlockSpec((A_TQ, A_Q_DIM), lambda b, c, *_: (b * A_CHUNKS + c, 0)),
        ),
        compiler_params=pltpu.CompilerParams(
            dimension_semantics=("arbitrary", "arbitrary"),
            vmem_limit_bytes=VMEM_LIMIT_BYTES),
        name="attn_a",
    )(sink, proj, proj, proj, proj, proj, proj, proj)


GRID_ROWS = SEQ // GRID_W
B_QR = 4
B_QC = 32
B_KR = B_QR + WIN_H
B_KC = B_QC + WIN_W
B_NRB = GRID_ROWS // B_QR
B_NCB = GRID_W // B_QC
B_NQ = B_QR * B_QC
B_NK = B_KR * B_KC
B_KBLOCKS = B_KR // B_QR
B_KS_MAX = B_NRB - B_KBLOCKS
B_KC0 = tuple(int(np.clip(cb * B_QC - WIN_W // 2, 0, GRID_W - B_KC)) for cb in range(B_NCB))


def _attn_b_kernel(q_ref, k0_ref, k1_ref, k2_ref, v0_ref, v1_ref, v2_ref, bias_ref, o_ref):
    k_refs = (k0_ref, k1_ref, k2_ref)
    v_refs = (v0_ref, v1_ref, v2_ref)
    for cb in range(B_NCB):
        qc = slice(cb * B_QC, (cb + 1) * B_QC)
        kc = slice(B_KC0[cb], B_KC0[cb] + B_KC)
        for h in range(B_HEADS):
            hd = slice(h * HEAD_DIM, (h + 1) * HEAD_DIM)
            q = q_ref[:, qc, hd].reshape(B_NQ, HEAD_DIM)
            k = jnp.concatenate([r[:, kc, hd] for r in k_refs], axis=0).reshape(B_NK, HEAD_DIM)
            v = jnp.concatenate([r[:, kc, hd] for r in v_refs], axis=0).reshape(B_NK, HEAD_DIM)
            s = lax.dot_general(q, k, (((1,), (1,)), ((), ())),
                                preferred_element_type=F32) + bias_ref[0, cb, h]
            m = jnp.max(s, axis=-1, keepdims=True)
            p = jnp.exp(s - m)
            denom = jnp.sum(p, axis=-1, keepdims=True)
            o = jnp.dot(p.astype(BF16), v, preferred_element_type=F32) / denom
            o_ref[:, qc, hd] = o.reshape(B_QR, B_QC, HEAD_DIM).astype(BF16)


def _attn_b_bias(rpb):
    pad_r, pad_c = B_KR, B_KC
    table = jnp.pad(rpb.astype(F32), ((0, 0), (pad_r, pad_r), (pad_c, pad_c)))
    row_ok, by_row = [], []
    for rb in (0, 1, B_NRB - 1):
        ks = int(np.clip(rb - 1, 0, B_KS_MAX))
        qr = rb * B_QR + np.arange(B_QR)
        kr = ks * B_QR + np.arange(B_KR)
        rs = np.clip(qr - WIN_H // 2, 0, GRID_ROWS - WIN_H)
        row_ok.append((kr[None, :] >= rs[:, None]) & (kr[None, :] < rs[:, None] + WIN_H))
        starts = pad_r + kr[0] - qr + WIN_H - 1
        by_row.append(jnp.stack([table[:, s:s + B_KR, :] for s in starts]))
    rows = jnp.stack(by_row)
    col_ok, by_col = [], []
    for cb in range(B_NCB):
        qc = cb * B_QC + np.arange(B_QC)
        kc = B_KC0[cb] + np.arange(B_KC)
        cs = np.clip(qc - WIN_W // 2, 0, GRID_W - WIN_W)
        col_ok.append((kc[None, :] >= cs[:, None]) & (kc[None, :] < cs[:, None] + WIN_W))
        starts = pad_c + kc[0] - qc + WIN_W - 1
        by_col.append(jnp.stack([rows[..., s:s + B_KC] for s in starts]))
    bias = jnp.stack(by_col)
    bias = bias.transpose(2, 0, 4, 3, 1, 5, 6)
    ok = (np.stack(row_ok)[:, None, :, None, :, None]
          & np.stack(col_ok)[None, :, None, :, None, :])
    bias = jnp.where(jnp.asarray(ok)[:, :, None], bias, NEG)
    return bias.reshape(3, B_NCB, B_HEADS, B_NQ, B_NK)


def _attn_b(proj3, bias, batch):
    rows = proj3.shape[0]
    q_col = COL_QB // B_DIM
    k_col = COL_KB // B_DIM
    v_col = COL_VB // B_DIM

    def kv_map(col, t):
        return lambda b, r: (b * B_NRB + jnp.clip(r - 1, 0, B_KS_MAX) + t, 0, col)

    def pattern(r):
        return jnp.where(r == 0, 0, jnp.where(r == B_NRB - 1, 2, 1))

    blk = lambda m: pl.BlockSpec((B_QR, GRID_W, B_DIM), m)
    return pl.pallas_call(
        _attn_b_kernel,
        out_shape=jax.ShapeDtypeStruct((rows, GRID_W, B_DIM), BF16),
        grid=(batch, B_NRB),
        in_specs=[
            blk(lambda b, r: (b * B_NRB + r, 0, q_col)),
            blk(kv_map(k_col, 0)), blk(kv_map(k_col, 1)), blk(kv_map(k_col, 2)),
            blk(kv_map(v_col, 0)), blk(kv_map(v_col, 1)), blk(kv_map(v_col, 2)),
            pl.BlockSpec((1, B_NCB, B_HEADS, B_NQ, B_NK), lambda b, r: (pattern(r), 0, 0, 0, 0)),
        ],
        out_specs=blk(lambda b, r: (b * B_NRB + r, 0, 0)),
        compiler_params=pltpu.CompilerParams(
            dimension_semantics=("arbitrary", "arbitrary"),
            vmem_limit_bytes=VMEM_LIMIT_BYTES),
        name="attn_b",
    )(proj3, proj3, proj3, proj3, proj3, proj3, proj3, bias)


MG_TM = 512


def _merge_kernel(x_ref, oa_ref, ob_ref, ga_ref, gb_ref, wa_ref, wb_ref, wo_ref, g_ref,
                  x1_ref, h2_ref):
    a = jnp.dot(oa_ref[...], wa_ref[...], preferred_element_type=F32)
    b = jnp.dot(ob_ref[...], wb_ref[...], preferred_element_type=F32)
    merged = ga_ref[...].astype(F32) * a + gb_ref[...].astype(F32) * b
    y = jnp.dot(merged.astype(BF16), wo_ref[...], preferred_element_type=F32)
    x1 = x_ref[...] + y
    x1_ref[...] = x1
    ms = jnp.mean(x1 * x1, axis=-1, keepdims=True)
    h2_ref[...] = (x1 * lax.rsqrt(ms + EPS) * g_ref[...]).astype(BF16)


def _merge(x2, out_a, out_b, proj, wa, wb, wo, norm_g):
    n = x2.shape[0]
    resident = lambda shape: pl.BlockSpec(shape, lambda i: (0, 0), pipeline_mode=pl.Buffered(1))
    return pl.pallas_call(
        _merge_kernel,
        out_shape=(jax.ShapeDtypeStruct((n, D_MODEL), F32),
                   jax.ShapeDtypeStruct((n, D_MODEL), BF16)),
        grid=(n // MG_TM,),
        in_specs=[
            pl.BlockSpec((MG_TM, D_MODEL), lambda i: (i, 0)),
            pl.BlockSpec((MG_TM, A_Q_DIM), lambda i: (i, 0)),
            pl.BlockSpec((MG_TM, B_DIM), lambda i: (i, 0)),
            pl.BlockSpec((MG_TM, D_MODEL), lambda i: (i, COL_GA // D_MODEL)),
            pl.BlockSpec((MG_TM, D_MODEL), lambda i: (i, COL_GB // D_MODEL)),
            resident((A_Q_DIM, D_MODEL)),
            resident((B_DIM, D_MODEL)),
            resident((D_MODEL, D_MODEL)),
            resident((1, D_MODEL)),
        ],
        out_specs=(pl.BlockSpec((MG_TM, D_MODEL), lambda i: (i, 0)),
                   pl.BlockSpec((MG_TM, D_MODEL), lambda i: (i, 0))),
        compiler_params=pltpu.CompilerParams(
            dimension_semantics=("arbitrary",),
            vmem_limit_bytes=VMEM_LIMIT_BYTES),
        name="merge",
    )(x2, out_a, out_b, proj, proj, wa, wb, wo, norm_g)


UP_TM = 1024
UP_TN = 512
UP_NJ = D_FF // UP_TN
HALO = 16
UP_ROWS = UP_TM + 2 * HALO


def _ffn_up_kernel(hp_ref, hm_ref, hn_ref, wg_ref, wv_ref, cwg_ref, cwv_ref, cbg_ref, cbv_ref,
                   o_ref, lhs_ref):
    i = pl.program_id(0)
    j = pl.program_id(1)
    tiles_per_seq = SEQ // UP_TM

    @pl.when(j == 0)
    def _():
        first = (i % tiles_per_seq) == 0
        last = (i % tiles_per_seq) == tiles_per_seq - 1
        lhs_ref[0:HALO, :] = jnp.where(first, jnp.zeros_like(hp_ref[...]), hp_ref[...])
        lhs_ref[HALO:HALO + UP_TM, :] = hm_ref[...]
        lhs_ref[HALO + UP_TM:, :] = jnp.where(last, jnp.zeros_like(hn_ref[...]), hn_ref[...])

    lhs = lhs_ref[...]

    def conv(w_ref, cw_ref, cb_ref):
        u = jnp.dot(lhs, w_ref[...], preferred_element_type=F32)
        below = pltpu.roll(u, 1, axis=0)[HALO:HALO + UP_TM]
        above = pltpu.roll(u, UP_ROWS - 1, axis=0)[HALO:HALO + UP_TM]
        mid = u[HALO:HALO + UP_TM]
        return below * cw_ref[0:1, :] + mid * cw_ref[1:2, :] + above * cw_ref[2:3, :] + cb_ref[...]

    gate = conv(wg_ref, cwg_ref, cbg_ref)
    val = conv(wv_ref, cwv_ref, cbv_ref)
    silu = gate * (0.5 * jnp.tanh(0.5 * gate) + 0.5)
    o_ref[...] = (silu * val).astype(BF16)


def _ffn_up(h2, w_up, conv_w, conv_b):
    n = h2.shape[0]
    halo_blocks = UP_TM // HALO
    last_halo = n // HALO - 1
    return pl.pallas_call(
        _ffn_up_kernel,
        out_shape=jax.ShapeDtypeStruct((n, D_FF), BF16),
        grid=(n // UP_TM, UP_NJ),
        in_specs=[
            pl.BlockSpec((HALO, D_MODEL), lambda i, j: (jnp.maximum(i * halo_blocks - 1, 0), 0)),
            pl.BlockSpec((UP_TM, D_MODEL), lambda i, j: (i, 0)),
            pl.BlockSpec((HALO, D_MODEL),
                         lambda i, j: (jnp.minimum((i + 1) * halo_blocks, last_halo), 0)),
            pl.BlockSpec((D_MODEL, UP_TN), lambda i, j: (0, j)),
            pl.BlockSpec((D_MODEL, UP_TN), lambda i, j: (0, UP_NJ + j)),
            pl.BlockSpec((3, UP_TN), lambda i, j: (0, j)),
            pl.BlockSpec((3, UP_TN), lambda i, j: (0, UP_NJ + j)),
            pl.BlockSpec((1, UP_TN), lambda i, j: (0, j)),
            pl.BlockSpec((1, UP_TN), lambda i, j: (0, UP_NJ + j)),
        ],
        out_specs=pl.BlockSpec((UP_TM, UP_TN), lambda i, j: (i, j)),
        scratch_shapes=[pltpu.VMEM((UP_ROWS, D_MODEL), BF16)],
        compiler_params=pltpu.CompilerParams(
            dimension_semantics=("arbitrary", "arbitrary"),
            vmem_limit_bytes=VMEM_LIMIT_BYTES),
        name="ffn_up",
    )(h2, h2, h2, w_up, w_up, conv_w, conv_w, conv_b, conv_b)


DN_TM = 1024
DN_TN = 512


def _ffn_down_kernel(a_ref, w_ref, x_ref, o_ref):
    o_ref[...] = x_ref[...] + jnp.dot(a_ref[...], w_ref[...], preferred_element_type=F32)


def _ffn_down(act, w_down, x1):
    n = act.shape[0]
    return pl.pallas_call(
        _ffn_down_kernel,
        out_shape=jax.ShapeDtypeStruct((n, D_MODEL), F32),
        grid=(n // DN_TM, D_MODEL // DN_TN),
        in_specs=[
            pl.BlockSpec((DN_TM, D_FF), lambda i, j: (i, 0)),
            pl.BlockSpec((D_FF, DN_TN), lambda i, j: (0, j)),
            pl.BlockSpec((DN_TM, DN_TN), lambda i, j: (i, j)),
        ],
        out_specs=pl.BlockSpec((DN_TM, DN_TN), lambda i, j: (i, j)),
        compiler_params=pltpu.CompilerParams(
            dimension_semantics=("arbitrary", "arbitrary"),
            vmem_limit_bytes=VMEM_LIMIT_BYTES),
        name="ffn_down",
    )(act, w_down, x1)


def _rope_tables():
    half = HEAD_DIM // 2
    pos = jnp.arange(SEQ, dtype=F32)
    inv_freq = ROPE_THETA ** (-jnp.arange(half, dtype=F32) * (2.0 / HEAD_DIM))
    ang = pos[:, None] * inv_freq[None, :]
    cos, sin = jnp.cos(ang), jnp.sin(ang)
    return jnp.concatenate([cos, cos], axis=-1), jnp.concatenate([-sin, sin], axis=-1)


def _permute_in_cols(w):
    splits = np.cumsum([A_Q_DIM, A_KV_DIM, A_KV_DIM, B_DIM, B_DIM, B_DIM, D_MODEL])
    qa, ka, va, qb, kb, vb, ga, gb = jnp.split(w, splits.tolist(), axis=-1)
    return jnp.concatenate([ga, gb, qa, qb, kb, vb, ka, va], axis=-1)


def kernel(x, norm_mix, w_in, a_q_norm, a_k_norm, a_sink, b_q_norm, b_k_norm, b_rpb,
           w_branch_a, w_branch_b, w_out, norm_ffn, w_up, conv_w, conv_b, w_down):
    batch, seq, d_model = x.shape
    assert (seq, d_model) == (SEQ, D_MODEL)
    n = batch * seq
    cos, sin_signed = _rope_tables()
    scale = 1.0 / math.sqrt(HEAD_DIM)
    x2 = x.reshape(n, d_model)
    for l in range(norm_mix.shape[0]):
        head_gains = jnp.concatenate([
            a_q_norm[l][None] * scale, a_k_norm[l][None], b_q_norm[l][None] * scale,
            b_k_norm[l][None], jnp.zeros((4, HEAD_DIM), F32)], axis=0)
        proj = _in_proj(x2, norm_mix[l][None], _permute_in_cols(w_in[l]).astype(BF16),
                        cos, sin_signed, head_gains)
        out_a = _attn_a(proj, a_sink[l], batch)
        out_b = _attn_b(proj.reshape(n // GRID_W, GRID_W, IN_COLS), _attn_b_bias(b_rpb[l]), batch)
        x1, h2 = _merge(x2, out_a, out_b.reshape(n, B_DIM), proj,
                        w_branch_a[l].astype(BF16), w_branch_b[l].astype(BF16),
                        w_out[l].astype(BF16), norm_ffn[l][None])
        act = _ffn_up(h2, w_up[l].astype(BF16), conv_w[l], conv_b[l][None])
        x2 = _ffn_down(act, w_down[l].astype(BF16), x1)
    return x2.reshape(batch, seq, d_model)
```

```python
import functools
import math

import numpy as np
import jax
import jax.numpy as jnp
from jax import lax
from jax.experimental import pallas as pl
from jax.experimental.pallas import tpu as pltpu

D_MODEL = 2048
SEQ = 4096
HEAD_DIM = 128
A_Q_HEADS = 8
A_KV_HEADS = 2
A_GROUP = A_Q_HEADS // A_KV_HEADS
WINDOW = 128
B_HEADS = 8
GRID_W = 64
WIN_H = 8
WIN_W = 16
D_FF = 5632
ROPE_THETA = 10000.0
EPS = 1e-6
NEG = -1e30

A_Q_DIM = A_Q_HEADS * HEAD_DIM
A_KV_DIM = A_KV_HEADS * HEAD_DIM
B_DIM = B_HEADS * HEAD_DIM
IN_COLS = A_Q_DIM + 2 * A_KV_DIM + 3 * B_DIM + 2 * D_MODEL

COL_GA = 0
COL_GB = COL_GA + D_MODEL
COL_QA = COL_GB + D_MODEL
COL_QB = COL_QA + A_Q_DIM
COL_KB = COL_QB + B_DIM
COL_VB = COL_KB + B_DIM
COL_KA = COL_VB + B_DIM
COL_VA = COL_KA + A_KV_DIM

VMEM_LIMIT_BYTES = 56 * 1024 * 1024

BF16 = jnp.bfloat16
F32 = jnp.float32

IN_TM = 1024
IN_TN = 512
IN_NJ = IN_COLS // IN_TN
_J_GATE_END = COL_QA // IN_TN
_J_QA_END = COL_QB // IN_TN
_J_QB_END = COL_KB // IN_TN
_J_KB_END = COL_VB // IN_TN
_J_VB_END = COL_KA // IN_TN
HEADS_PER_TILE = IN_TN // HEAD_DIM


def _head_norm(a, gain):
    ms = jnp.mean(a * a, axis=-1, keepdims=True)
    return a * lax.rsqrt(ms + EPS) * gain


def _rope(y, cos, sin_signed):
    return y * cos + pltpu.roll(y, HEAD_DIM // 2, axis=1) * sin_signed


def _in_proj_kernel(x_ref, g_ref, w_ref, cos_ref, sin_ref, hg_ref, o_ref, h_ref):
    j = pl.program_id(1)

    @pl.when(j == 0)
    def _():
        x = x_ref[...]
        ms = jnp.mean(x * x, axis=-1, keepdims=True)
        h_ref[...] = (x * lax.rsqrt(ms + EPS) * g_ref[...]).astype(BF16)

    acc = jnp.dot(h_ref[...], w_ref[...], preferred_element_type=F32)

    def head(t):
        return acc[:, t * HEAD_DIM:(t + 1) * HEAD_DIM]

    def store_head(t, val):
        o_ref[:, t * HEAD_DIM:(t + 1) * HEAD_DIM] = val.astype(BF16)

    @pl.when(j < _J_GATE_END)
    def _():
        o_ref[...] = (0.5 * jnp.tanh(0.5 * acc) + 0.5).astype(BF16)

    @pl.when((j >= _J_GATE_END) & (j < _J_QA_END))
    def _():
        for t in range(HEADS_PER_TILE):
            store_head(t, _rope(_head_norm(head(t), hg_ref[0:1, :]), cos_ref[...], sin_ref[...]))

    @pl.when((j >= _J_QA_END) & (j < _J_QB_END))
    def _():
        for t in range(HEADS_PER_TILE):
            store_head(t, _head_norm(head(t), hg_ref[2:3, :]))

    @pl.when((j >= _J_QB_END) & (j < _J_KB_END))
    def _():
        for t in range(HEADS_PER_TILE):
            store_head(t, _head_norm(head(t), hg_ref[3:4, :]))

    @pl.when((j >= _J_KB_END) & (j < _J_VB_END))
    def _():
        o_ref[...] = acc.astype(BF16)

    @pl.when(j >= _J_VB_END)
    def _():
        for t in range(A_KV_HEADS):
            store_head(t, _rope(_head_norm(head(t), hg_ref[1:2, :]), cos_ref[...], sin_ref[...]))
        for t in range(A_KV_HEADS, 2 * A_KV_HEADS):
            store_head(t, head(t))


def _in_proj(x2, norm_g, w_bf, cos, sin_signed, head_gains):
    n = x2.shape[0]
    seq_tiles = SEQ // IN_TM
    return pl.pallas_call(
        _in_proj_kernel,
        out_shape=jax.ShapeDtypeStruct((n, IN_COLS), BF16),
        grid=(n // IN_TM, IN_NJ),
        in_specs=[
            pl.BlockSpec((IN_TM, D_MODEL), lambda i, j: (i, 0)),
            pl.BlockSpec((1, D_MODEL), lambda i, j: (0, 0)),
            pl.BlockSpec((D_MODEL, IN_TN), lambda i, j: (0, j)),
            pl.BlockSpec((IN_TM, HEAD_DIM), lambda i, j: (i % seq_tiles, 0)),
            pl.BlockSpec((IN_TM, HEAD_DIM), lambda i, j: (i % seq_tiles, 0)),
            pl.BlockSpec((8, HEAD_DIM), lambda i, j: (0, 0)),
        ],
        out_specs=pl.BlockSpec((IN_TM, IN_TN), lambda i, j: (i, j)),
        scratch_shapes=[pltpu.VMEM((IN_TM, D_MODEL), BF16)],
        compiler_params=pltpu.CompilerParams(
            dimension_semantics=("arbitrary", "arbitrary"),
            vmem_limit_bytes=VMEM_LIMIT_BYTES),
        name="in_proj",
    )(x2, norm_g, w_bf, cos, sin_signed, head_gains)


A_TQ = 512
A_BLK = WINDOW
A_NB = A_TQ // A_BLK
A_CHUNKS = SEQ // A_TQ


def _attn_a_kernel(sink_ref, q_ref, kp_ref, km_ref, kn_ref, vp_ref, vm_ref, vn_ref, o_ref):
    c = pl.program_id(1)
    k_all = jnp.concatenate([kp_ref[...], km_ref[...], kn_ref[...]], axis=0)
    v_all = jnp.concatenate([vp_ref[...], vm_ref[...], vn_ref[...]], axis=0)
    qq = lax.broadcasted_iota(jnp.int32, (A_BLK, 3 * A_BLK), 0)
    kk = lax.broadcasted_iota(jnp.int32, (A_BLK, 3 * A_BLK), 1)
    d = kk - qq
    band = (d >= 0) & (d <= 2 * WINDOW)
    ones = jnp.ones((3 * A_BLK, HEAD_DIM), BF16)
    for n in range(A_NB):
        valid = band
        if n == 0:
            valid = valid & ((kk >= A_BLK) | (c > 0))
        if n == A_NB - 1:
            valid = valid & ((kk < 2 * A_BLK) | (c < A_CHUNKS - 1))
        for h in range(A_KV_HEADS):
            k = k_all[n * A_BLK:(n + 3) * A_BLK, h * HEAD_DIM:(h + 1) * HEAD_DIM]
            v = v_all[n * A_BLK:(n + 3) * A_BLK, h * HEAD_DIM:(h + 1) * HEAD_DIM]
            heads = [h * A_GROUP + g for g in range(A_GROUP)]
            qs = jnp.concatenate(
                [q_ref[n * A_BLK:(n + 1) * A_BLK, t * HEAD_DIM:(t + 1) * HEAD_DIM] for t in heads],
                axis=0)
            s = lax.dot_general(qs, k, (((1,), (1,)), ((), ())),
                                preferred_element_type=F32)
            ps, sink_terms = [], []
            for g, t in enumerate(heads):
                sg = jnp.where(valid, s[g * A_BLK:(g + 1) * A_BLK], NEG)
                m = jnp.maximum(jnp.max(sg, axis=-1, keepdims=True), sink_ref[t])
                ps.append(jnp.exp(sg - m).astype(BF16))
                sink_terms.append(jnp.exp(sink_ref[t] - m))
            o = jnp.dot(jnp.concatenate(ps, axis=0), jnp.concatenate([v, ones], axis=1),
                        preferred_element_type=F32)
            for g, t in enumerate(heads):
                og = o[g * A_BLK:(g + 1) * A_BLK]
                og = og[:, :HEAD_DIM] / (og[:, HEAD_DIM:] + sink_terms[g])
                o_ref[n * A_BLK:(n + 1) * A_BLK, t * HEAD_DIM:(t + 1) * HEAD_DIM] = og.astype(BF16)


def _attn_a(proj, sink, batch):
    n = proj.shape[0]
    blk_per_seq = SEQ // A_BLK
    q_col = COL_QA // A_Q_DIM
    k_col = COL_KA // A_KV_DIM
    v_col = COL_VA // A_KV_DIM

    def prev_map(col):
        return lambda b, c, *_: (b * blk_per_seq + jnp.maximum(c * A_NB - 1, 0), col)

    def main_map(col):
        return lambda b, c, *_: (b * A_CHUNKS + c, col)

    def next_map(col):
        return lambda b, c, *_: (b * blk_per_seq + jnp.minimum(c * A_NB + A_NB, blk_per_seq - 1), col)

    halo = lambda m: pl.BlockSpec((A_BLK, A_KV_DIM), m)
    main = lambda m: pl.BlockSpec((A_TQ, A_KV_DIM), m)
    return pl.pallas_call(
        _attn_a_kernel,
        out_shape=jax.ShapeDtypeStruct((n, A_Q_DIM), BF16),
        grid_spec=pltpu.PrefetchScalarGridSpec(
            num_scalar_prefetch=1,
            grid=(batch, A_CHUNKS),
            in_specs=[
                pl.BlockSpec((A_TQ, A_Q_DIM), main_map(q_col)),
                halo(prev_map(k_col)), main(main_map(k_col)), halo(next_map(k_col)),
                halo(prev_map(v_col)), main(main_map(v_col)), halo(next_map(v_col)),
            ],
            out_specs=pl.BlockSpec((A_TQ, A_Q_DIM), lambda b, c, *_: (b * A_CHUNKS + c, 0)),
        ),
        compiler_params=pltpu.CompilerParams(
            dimension_semantics=("arbitrary", "arbitrary"),
            vmem_limit_bytes=VMEM_LIMIT_BYTES),
        name="attn_a",
    )(sink, proj, proj, proj, proj, proj, proj, proj)


GRID_ROWS = SEQ // GRID_W
B_QR = 4
B_QC = GRID_W
B_KR = B_QR + WIN_H
B_KC = min(B_QC + WIN_W, GRID_W)
B_NRB = GRID_ROWS // B_QR
B_NCB = GRID_W // B_QC
B_NQ = B_QR * B_QC
B_NK = B_KR * B_KC
B_KBLOCKS = B_KR // B_QR
B_KS_MAX = B_NRB - B_KBLOCKS
B_KC0 = tuple(int(np.clip(cb * B_QC - WIN_W // 2, 0, GRID_W - B_KC)) for cb in range(B_NCB))


def _attn_b_kernel(q_ref, k0_ref, k1_ref, k2_ref, v0_ref, v1_ref, v2_ref, bias_ref, o_ref):
    k_refs = (k0_ref, k1_ref, k2_ref)
    v_refs = (v0_ref, v1_ref, v2_ref)
    ones = jnp.ones((B_NK, HEAD_DIM), BF16)
    for cb in range(B_NCB):
        qc = slice(cb * B_QC, (cb + 1) * B_QC)
        kc = slice(B_KC0[cb], B_KC0[cb] + B_KC)
        for h in range(B_HEADS):
            hd = slice(h * HEAD_DIM, (h + 1) * HEAD_DIM)
            q = q_ref[:, qc, hd].reshape(B_NQ, HEAD_DIM)
            k = jnp.concatenate([r[:, kc, hd] for r in k_refs], axis=0).reshape(B_NK, HEAD_DIM)
            v = jnp.concatenate([r[:, kc, hd] for r in v_refs], axis=0).reshape(B_NK, HEAD_DIM)
            s = lax.dot_general(q, k, (((1,), (1,)), ((), ())),
                                preferred_element_type=F32) + bias_ref[0, cb, h]
            m = jnp.max(s, axis=-1, keepdims=True)
            p = jnp.exp(s - m).astype(BF16)
            o = jnp.dot(p, jnp.concatenate([v, ones], axis=1), preferred_element_type=F32)
            o = o[:, :HEAD_DIM] / o[:, HEAD_DIM:]
            o_ref[:, qc, hd] = o.reshape(B_QR, B_QC, HEAD_DIM).astype(BF16)


def _attn_b_bias(rpb):
    pad_r, pad_c = B_KR, B_KC
    table = jnp.pad(rpb.astype(F32), ((0, 0), (pad_r, pad_r), (pad_c, pad_c)))
    row_ok, by_row = [], []
    for rb in (0, 1, B_NRB - 1):
        ks = int(np.clip(rb - 1, 0, B_KS_MAX))
        qr = rb * B_QR + np.arange(B_QR)
        kr = ks * B_QR + np.arange(B_KR)
        rs = np.clip(qr - WIN_H // 2, 0, GRID_ROWS - WIN_H)
        row_ok.append((kr[None, :] >= rs[:, None]) & (kr[None, :] < rs[:, None] + WIN_H))
        starts = pad_r + kr[0] - qr + WIN_H - 1
        by_row.append(jnp.stack([table[:, s:s + B_KR, :] for s in starts]))
    rows = jnp.stack(by_row)
    col_ok, by_col = [], []
    for cb in range(B_NCB):
        qc = cb * B_QC + np.arange(B_QC)
        kc = B_KC0[cb] + np.arange(B_KC)
        cs = np.clip(qc - WIN_W // 2, 0, GRID_W - WIN_W)
        col_ok.append((kc[None, :] >= cs[:, None]) & (kc[None, :] < cs[:, None] + WIN_W))
        starts = pad_c + kc[0] - qc + WIN_W - 1
        by_col.append(jnp.stack([rows[..., s:s + B_KC] for s in starts]))
    bias = jnp.stack(by_col)
    bias = bias.transpose(2, 0, 4, 3, 1, 5, 6)
    ok = (np.stack(row_ok)[:, None, :, None, :, None]
          & np.stack(col_ok)[None, :, None, :, None, :])
    bias = jnp.where(jnp.asarray(ok)[:, :, None], bias, NEG)
    return bias.reshape(3, B_NCB, B_HEADS, B_NQ, B_NK)


def _attn_b(proj3, bias, batch):
    rows = proj3.shape[0]
    q_col = COL_QB // B_DIM
    k_col = COL_KB // B_DIM
    v_col = COL_VB // B_DIM

    def kv_map(col, t):
        return lambda b, r: (b * B_NRB + jnp.clip(r - 1, 0, B_KS_MAX) + t, 0, col)

    def pattern(r):
        return jnp.where(r == 0, 0, jnp.where(r == B_NRB - 1, 2, 1))

    blk = lambda m: pl.BlockSpec((B_QR, GRID_W, B_DIM), m)
    return pl.pallas_call(
        _attn_b_kernel,
        out_shape=jax.ShapeDtypeStruct((rows, GRID_W, B_DIM), BF16),
        grid=(batch, B_NRB),
        in_specs=[
            blk(lambda b, r: (b * B_NRB + r, 0, q_col)),
            blk(kv_map(k_col, 0)), blk(kv_map(k_col, 1)), blk(kv_map(k_col, 2)),
            blk(kv_map(v_col, 0)), blk(kv_map(v_col, 1)), blk(kv_map(v_col, 2)),
            pl.BlockSpec((1, B_NCB, B_HEADS, B_NQ, B_NK), lambda b, r: (pattern(r), 0, 0, 0, 0)),
        ],
        out_specs=blk(lambda b, r: (b * B_NRB + r, 0, 0)),
        compiler_params=pltpu.CompilerParams(
            dimension_semantics=("arbitrary", "arbitrary"),
            vmem_limit_bytes=VMEM_LIMIT_BYTES),
        name="attn_b",
    )(proj3, proj3, proj3, proj3, proj3, proj3, proj3, bias)


MG_TM = 512


def _merge_kernel(x_ref, oa_ref, ob_ref, ga_ref, gb_ref, wa_ref, wb_ref, wo_ref, g_ref,
                  x1_ref, h2_ref):
    a = jnp.dot(oa_ref[...], wa_ref[...], preferred_element_type=F32)
    b = jnp.dot(ob_ref[...], wb_ref[...], preferred_element_type=F32)
    merged = ga_ref[...].astype(F32) * a + gb_ref[...].astype(F32) * b
    y = jnp.dot(merged.astype(BF16), wo_ref[...], preferred_element_type=F32)
    x1 = x_ref[...] + y
    x1_ref[...] = x1
    ms = jnp.mean(x1 * x1, axis=-1, keepdims=True)
    h2_ref[...] = (x1 * lax.rsqrt(ms + EPS) * g_ref[...]).astype(BF16)


def _merge(x2, out_a, out_b, proj, wa, wb, wo, norm_g):
    n = x2.shape[0]
    resident = lambda shape: pl.BlockSpec(shape, lambda i: (0, 0), pipeline_mode=pl.Buffered(1))
    return pl.pallas_call(
        _merge_kernel,
        out_shape=(jax.ShapeDtypeStruct((n, D_MODEL), F32),
                   jax.ShapeDtypeStruct((n, D_MODEL), BF16)),
        grid=(n // MG_TM,),
        in_specs=[
            pl.BlockSpec((MG_TM, D_MODEL), lambda i: (i, 0)),
            pl.BlockSpec((MG_TM, A_Q_DIM), lambda i: (i, 0)),
            pl.BlockSpec((MG_TM, B_DIM), lambda i: (i, 0)),
            pl.BlockSpec((MG_TM, D_MODEL), lambda i: (i, COL_GA // D_MODEL)),
            pl.BlockSpec((MG_TM, D_MODEL), lambda i: (i, COL_GB // D_MODEL)),
            resident((A_Q_DIM, D_MODEL)),
            resident((B_DIM, D_MODEL)),
            resident((D_MODEL, D_MODEL)),
            resident((1, D_MODEL)),
        ],
        out_specs=(pl.BlockSpec((MG_TM, D_MODEL), lambda i: (i, 0)),
                   pl.BlockSpec((MG_TM, D_MODEL), lambda i: (i, 0))),
        compiler_params=pltpu.CompilerParams(
            dimension_semantics=("arbitrary",),
            vmem_limit_bytes=VMEM_LIMIT_BYTES),
        name="merge",
    )(x2, out_a, out_b, proj, proj, wa, wb, wo, norm_g)


UP_TM = 1024
UP_TN = 512
UP_NJ = D_FF // UP_TN
HALO = 16
UP_ROWS = UP_TM + 2 * HALO


def _ffn_up_kernel(hp_ref, hm_ref, hn_ref, wg_ref, wv_ref, cwg_ref, cwv_ref, cbg_ref, cbv_ref,
                   o_ref, lhs_ref):
    i = pl.program_id(0)
    j = pl.program_id(1)
    tiles_per_seq = SEQ // UP_TM

    @pl.when(j == 0)
    def _():
        first = (i % tiles_per_seq) == 0
        last = (i % tiles_per_seq) == tiles_per_seq - 1
        lhs_ref[0:HALO, :] = jnp.where(first, jnp.zeros_like(hp_ref[...]), hp_ref[...])
        lhs_ref[HALO:HALO + UP_TM, :] = hm_ref[...]
        lhs_ref[HALO + UP_TM:, :] = jnp.where(last, jnp.zeros_like(hn_ref[...]), hn_ref[...])

    lhs = lhs_ref[...]

    def conv(w_ref, cw_ref, cb_ref):
        u = jnp.dot(lhs, w_ref[...], preferred_element_type=F32)
        below = pltpu.roll(u, 1, axis=0)[HALO:HALO + UP_TM]
        above = pltpu.roll(u, UP_ROWS - 1, axis=0)[HALO:HALO + UP_TM]
        mid = u[HALO:HALO + UP_TM]
        return below * cw_ref[0:1, :] + mid * cw_ref[1:2, :] + above * cw_ref[2:3, :] + cb_ref[...]

    gate = conv(wg_ref, cwg_ref, cbg_ref)
    val = conv(wv_ref, cwv_ref, cbv_ref)
    silu = gate * (0.5 * jnp.tanh(0.5 * gate) + 0.5)
    o_ref[...] = (silu * val).astype(BF16)


def _ffn_up(h2, w_up, conv_w, conv_b):
    n = h2.shape[0]
    halo_blocks = UP_TM // HALO
    last_halo = n // HALO - 1
    return pl.pallas_call(
        _ffn_up_kernel,
        out_shape=jax.ShapeDtypeStruct((n, D_FF), BF16),
        grid=(n // UP_TM, UP_NJ),
        in_specs=[
            pl.BlockSpec((HALO, D_MODEL), lambda i, j: (jnp.maximum(i * halo_blocks - 1, 0), 0)),
            pl.BlockSpec((UP_TM, D_MODEL), lambda i, j: (i, 0)),
            pl.BlockSpec((HALO, D_MODEL),
                         lambda i, j: (jnp.minimum((i + 1) * halo_blocks, last_halo), 0)),
            pl.BlockSpec((D_MODEL, UP_TN), lambda i, j: (0, j)),
            pl.BlockSpec((D_MODEL, UP_TN), lambda i, j: (0, UP_NJ + j)),
            pl.BlockSpec((3, UP_TN), lambda i, j: (0, j)),
            pl.BlockSpec((3, UP_TN), lambda i, j: (0, UP_NJ + j)),
            pl.BlockSpec((1, UP_TN), lambda i, j: (0, j)),
            pl.BlockSpec((1, UP_TN), lambda i, j: (0, UP_NJ + j)),
        ],
        out_specs=pl.BlockSpec((UP_TM, UP_TN), lambda i, j: (i, j)),
        scratch_shapes=[pltpu.VMEM((UP_ROWS, D_MODEL), BF16)],
        compiler_params=pltpu.CompilerParams(
            dimension_semantics=("arbitrary", "arbitrary"),
            vmem_limit_bytes=VMEM_LIMIT_BYTES),
        name="ffn_up",
    )(h2, h2, h2, w_up, w_up, conv_w, conv_w, conv_b, conv_b)


DN_TM = 1024
DN_TN = 512


def _ffn_down_kernel(a_ref, w_ref, x_ref, o_ref):
    o_ref[...] = x_ref[...] + jnp.dot(a_ref[...], w_ref[...], preferred_element_type=F32)


def _ffn_down(act, w_down, x1):
    n = act.shape[0]
    return pl.pallas_call(
        _ffn_down_kernel,
        out_shape=jax.ShapeDtypeStruct((n, D_MODEL), F32),
        grid=(n // DN_TM, D_MODEL // DN_TN),
        in_specs=[
            pl.BlockSpec((DN_TM, D_FF), lambda i, j: (i, 0)),
            pl.BlockSpec((D_FF, DN_TN), lambda i, j: (0, j)),
            pl.BlockSpec((DN_TM, DN_TN), lambda i, j: (i, j)),
        ],
        out_specs=pl.BlockSpec((DN_TM, DN_TN), lambda i, j: (i, j)),
        compiler_params=pltpu.CompilerParams(
            dimension_semantics=("arbitrary", "arbitrary"),
            vmem_limit_bytes=VMEM_LIMIT_BYTES),
        name="ffn_down",
    )(act, w_down, x1)


def _rope_tables():
    half = HEAD_DIM // 2
    pos = jnp.arange(SEQ, dtype=F32)
    inv_freq = ROPE_THETA ** (-jnp.arange(half, dtype=F32) * (2.0 / HEAD_DIM))
    ang = pos[:, None] * inv_freq[None, :]
    cos, sin = jnp.cos(ang), jnp.sin(ang)
    return jnp.concatenate([cos, cos], axis=-1), jnp.concatenate([-sin, sin], axis=-1)


def _permute_in_cols(w):
    splits = np.cumsum([A_Q_DIM, A_KV_DIM, A_KV_DIM, B_DIM, B_DIM, B_DIM, D_MODEL])
    qa, ka, va, qb, kb, vb, ga, gb = jnp.split(w, splits.tolist(), axis=-1)
    return jnp.concatenate([ga, gb, qa, qb, kb, vb, ka, va], axis=-1)


def kernel(x, norm_mix, w_in, a_q_norm, a_k_norm, a_sink, b_q_norm, b_k_norm, b_rpb,
           w_branch_a, w_branch_b, w_out, norm_ffn, w_up, conv_w, conv_b, w_down):
    batch, seq, d_model = x.shape
    assert (seq, d_model) == (SEQ, D_MODEL)
    n = batch * seq
    cos, sin_signed = _rope_tables()
    scale = 1.0 / math.sqrt(HEAD_DIM)
    x2 = x.reshape(n, d_model)
    for l in range(norm_mix.shape[0]):
        head_gains = jnp.concatenate([
            a_q_norm[l][None] * scale, a_k_norm[l][None], b_q_norm[l][None] * scale,
            b_k_norm[l][None], jnp.zeros((4, HEAD_DIM), F32)], axis=0)
        proj = _in_proj(x2, norm_mix[l][None], _permute_in_cols(w_in[l]).astype(BF16),
                        cos, sin_signed, head_gains)
        out_a = _attn_a(proj, a_sink[l], batch)
        out_b = _attn_b(proj.reshape(n // GRID_W, GRID_W, IN_COLS), _attn_b_bias(b_rpb[l]), batch)
        x1, h2 = _merge(x2, out_a, out_b.reshape(n, B_DIM), proj,
                        w_branch_a[l].astype(BF16), w_branch_b[l].astype(BF16),
                        w_out[l].astype(BF16), norm_ffn[l][None])
        act = _ffn_up(h2, w_up[l].astype(BF16), conv_w[l], conv_b[l][None])
        x2 = _ffn_down(act, w_down[l].astype(BF16), x1)
    return x2.reshape(batch, seq, d_model)
```

```python
import math

import numpy as np
import jax
import jax.numpy as jnp
from jax import lax
from jax.experimental import pallas as pl
from jax.experimental.pallas import tpu as pltpu

D_MODEL = 2048
SEQ = 4096
HEAD_DIM = 128
A_Q_HEADS = 8
A_KV_HEADS = 2
A_GROUP = A_Q_HEADS // A_KV_HEADS
WINDOW = 128
B_HEADS = 8
GRID_W = 64
WIN_H = 8
WIN_W = 16
D_FF = 5632
ROPE_THETA = 10000.0
EPS = 1e-6
NEG = -1e30

A_Q_DIM = A_Q_HEADS * HEAD_DIM
A_KV_DIM = A_KV_HEADS * HEAD_DIM
B_DIM = B_HEADS * HEAD_DIM

VMEM_LIMIT_BYTES = 56 * 1024 * 1024
MXU_COLS = 256
SUB = MXU_COLS
HEADS_PER_SUB = SUB // HEAD_DIM

BF16 = jnp.bfloat16
F32 = jnp.float32


def _params(*semantics):
    return pltpu.CompilerParams(dimension_semantics=semantics, vmem_limit_bytes=VMEM_LIMIT_BYTES)


def _resident(shape):
    return pl.BlockSpec(shape, lambda *_: (0,) * len(shape), pipeline_mode=pl.Buffered(1))


def _rmsnorm(x, gain):
    ms = jnp.mean(x * x, axis=-1, keepdims=True)
    return x * lax.rsqrt(ms + EPS) * gain


def _rope(y, cos, sin_signed):
    return y * cos + pltpu.roll(y, HEAD_DIM // 2, axis=1) * sin_signed


def _sigmoid(x):
    return 0.5 * jnp.tanh(0.5 * x) + 0.5


IN_TM = 1024
GATE_TN = 1024
GATE_DIM = 2 * D_MODEL
ROPE_DIM = A_Q_DIM + A_KV_DIM
NORM_DIM = 2 * B_DIM
PLAIN_DIM = B_DIM + A_KV_DIM


def _in_gate_kernel(x_ref, g_ref, w_ref, o_ref, h_ref):
    @pl.when(pl.program_id(1) == 0)
    def _():
        h_ref[...] = _rmsnorm(x_ref[...], g_ref[...]).astype(BF16)

    lhs = h_ref[...]
    for k in range(GATE_TN // SUB):
        cols = slice(k * SUB, (k + 1) * SUB)
        acc = jnp.dot(lhs, w_ref[:, cols], preferred_element_type=F32)
        o_ref[:, cols] = _sigmoid(acc).astype(BF16)


def _in_gate(x2, norm_g, w_gate):
    n = x2.shape[0]
    return pl.pallas_call(
        _in_gate_kernel,
        out_shape=(jax.ShapeDtypeStruct((n, GATE_DIM), BF16),
                   jax.ShapeDtypeStruct((n, D_MODEL), BF16)),
        grid=(n // IN_TM, GATE_DIM // GATE_TN),
        in_specs=[
            pl.BlockSpec((IN_TM, D_MODEL), lambda i, j: (i, 0)),
            pl.BlockSpec((1, D_MODEL), lambda i, j: (0, 0)),
            pl.BlockSpec((D_MODEL, GATE_TN), lambda i, j: (0, j)),
        ],
        out_specs=(pl.BlockSpec((IN_TM, GATE_TN), lambda i, j: (i, j)),
                   pl.BlockSpec((IN_TM, D_MODEL), lambda i, j: (i, 0))),
        compiler_params=_params("arbitrary", "arbitrary"),
        name="in_gate",
    )(x2, norm_g, w_gate)


def _store_heads(o_ref, k, acc, fn):
    for t in range(HEADS_PER_SUB):
        col = k * SUB + t * HEAD_DIM
        o_ref[:, col:col + HEAD_DIM] = fn(acc[:, t * HEAD_DIM:(t + 1) * HEAD_DIM]).astype(BF16)


def _in_rope_kernel(h_ref, w_ref, cos_ref, sin_ref, gq_ref, gk_ref, o_ref):
    lhs = h_ref[...]
    for k in range(ROPE_DIM // SUB):
        acc = jnp.dot(lhs, w_ref[:, k * SUB:(k + 1) * SUB], preferred_element_type=F32)
        gain_ref = gq_ref if k * SUB < A_Q_DIM else gk_ref
        _store_heads(o_ref, k, acc,
                     lambda a: _rope(_rmsnorm(a, gain_ref[...]), cos_ref[...], sin_ref[...]))


def _in_norm_kernel(h_ref, w_ref, gq_ref, gk_ref, o_ref):
    lhs = h_ref[...]
    for k in range(NORM_DIM // SUB):
        acc = jnp.dot(lhs, w_ref[:, k * SUB:(k + 1) * SUB], preferred_element_type=F32)
        gain_ref = gq_ref if k * SUB < B_DIM else gk_ref
        _store_heads(o_ref, k, acc, lambda a: _rmsnorm(a, gain_ref[...]))


def _in_plain_kernel(h_ref, w_ref, o_ref):
    lhs = h_ref[...]
    for k in range(PLAIN_DIM // SUB):
        cols = slice(k * SUB, (k + 1) * SUB)
        o_ref[:, cols] = jnp.dot(lhs, w_ref[:, cols], preferred_element_type=F32).astype(BF16)


def _in_heads(body, name, h, w, extra, extra_specs):
    n, width = h.shape[0], w.shape[1]
    return pl.pallas_call(
        body,
        out_shape=jax.ShapeDtypeStruct((n, width), BF16),
        grid=(n // IN_TM,),
        in_specs=[pl.BlockSpec((IN_TM, D_MODEL), lambda i: (i, 0)),
                  _resident((D_MODEL, width))] + extra_specs,
        out_specs=pl.BlockSpec((IN_TM, width), lambda i: (i, 0)),
        compiler_params=_params("arbitrary"),
        name=name,
    )(h, w, *extra)


A_TQ = 512
A_BLK = WINDOW
A_NB = A_TQ // A_BLK
A_CHUNKS = SEQ // A_TQ


def _attn_a_kernel(sink_ref, q_ref, kp_ref, km_ref, kn_ref, vp_ref, vm_ref, vn_ref, o_ref):
    c = pl.program_id(1)
    k_all = jnp.concatenate([kp_ref[...], km_ref[...], kn_ref[...]], axis=0)
    v_all = jnp.concatenate([vp_ref[...], vm_ref[...], vn_ref[...]], axis=0)
    qq = lax.broadcasted_iota(jnp.int32, (A_BLK, 3 * A_BLK), 0)
    kk = lax.broadcasted_iota(jnp.int32, (A_BLK, 3 * A_BLK), 1)
    d = kk - qq
    band = (d >= 0) & (d <= 2 * WINDOW)
    ones = jnp.ones((3 * A_BLK, HEAD_DIM), BF16)
    for n in range(A_NB):
        valid = band
        if n == 0:
            valid = valid & ((kk >= A_BLK) | (c > 0))
        if n == A_NB - 1:
            valid = valid & ((kk < 2 * A_BLK) | (c < A_CHUNKS - 1))
        for h in range(A_KV_HEADS):
            k = k_all[n * A_BLK:(n + 3) * A_BLK, h * HEAD_DIM:(h + 1) * HEAD_DIM]
            v = v_all[n * A_BLK:(n + 3) * A_BLK, h * HEAD_DIM:(h + 1) * HEAD_DIM]
            heads = [h * A_GROUP + g for g in range(A_GROUP)]
            qs = jnp.concatenate(
                [q_ref[n * A_BLK:(n + 1) * A_BLK, t * HEAD_DIM:(t + 1) * HEAD_DIM] for t in heads],
                axis=0)
            s = lax.dot_general(qs, k, (((1,), (1,)), ((), ())),
                                preferred_element_type=F32)
            ps, sink_terms = [], []
            for g, t in enumerate(heads):
                sg = jnp.where(valid, s[g * A_BLK:(g + 1) * A_BLK], NEG)
                m = jnp.maximum(jnp.max(sg, axis=-1, keepdims=True), sink_ref[t])
                ps.append(jnp.exp(sg - m).astype(BF16))
                sink_terms.append(jnp.exp(sink_ref[t] - m))
            o = jnp.dot(jnp.concatenate(ps, axis=0), jnp.concatenate([v, ones], axis=1),
                        preferred_element_type=F32)
            for g, t in enumerate(heads):
                og = o[g * A_BLK:(g + 1) * A_BLK]
                og = og[:, :HEAD_DIM] / (og[:, HEAD_DIM:] + sink_terms[g])
                o_ref[n * A_BLK:(n + 1) * A_BLK, t * HEAD_DIM:(t + 1) * HEAD_DIM] = og.astype(BF16)


def _attn_a(qk, vals, sink, batch):
    n = qk.shape[0]
    blk_per_seq = SEQ // A_BLK
    k_col = A_Q_DIM // A_KV_DIM
    v_col = B_DIM // A_KV_DIM

    def prev_map(col):
        return lambda b, c, *_: (b * blk_per_seq + jnp.maximum(c * A_NB - 1, 0), col)

    def main_map(col):
        return lambda b, c, *_: (b * A_CHUNKS + c, col)

    def next_map(col):
        return lambda b, c, *_: (b * blk_per_seq + jnp.minimum(c * A_NB + A_NB, blk_per_seq - 1), col)

    halo = lambda m: pl.BlockSpec((A_BLK, A_KV_DIM), m)
    main = lambda m: pl.BlockSpec((A_TQ, A_KV_DIM), m)
    return pl.pallas_call(
        _attn_a_kernel,
        out_shape=jax.ShapeDtypeStruct((n, A_Q_DIM), BF16),
        grid_spec=pltpu.PrefetchScalarGridSpec(
            num_scalar_prefetch=1,
            grid=(batch, A_CHUNKS),
            in_specs=[
                pl.BlockSpec((A_TQ, A_Q_DIM), main_map(0)),
                halo(prev_map(k_col)), main(main_map(k_col)), halo(next_map(k_col)),
                halo(prev_map(v_col)), main(main_map(v_col)), halo(next_map(v_col)),
            ],
            out_specs=pl.BlockSpec((A_TQ, A_Q_DIM), lambda b, c, *_: (b * A_CHUNKS + c, 0)),
        ),
        compiler_params=_params("arbitrary", "arbitrary"),
        name="attn_a",
    )(sink, qk, qk, qk, qk, vals, vals, vals)


GRID_ROWS = SEQ // GRID_W
B_QR = 4
B_KR = B_QR + WIN_H
B_NRB = GRID_ROWS // B_QR
B_NQ = B_QR * GRID_W
B_NK = B_KR * GRID_W
B_KBLOCKS = B_KR // B_QR
B_KS_MAX = B_NRB - B_KBLOCKS
B_PATTERN_ROWBLOCKS = (0, 1, B_NRB - 1)


def _attn_b_kernel(q_ref, k0_ref, k1_ref, k2_ref, v0_ref, v1_ref, v2_ref, bias_ref, o_ref):
    k_refs = (k0_ref, k1_ref, k2_ref)
    v_refs = (v0_ref, v1_ref, v2_ref)
    ones = jnp.ones((B_NK, HEAD_DIM), BF16)
    for h in range(B_HEADS):
        hd = slice(h * HEAD_DIM, (h + 1) * HEAD_DIM)
        q = q_ref[:, :, hd].reshape(B_NQ, HEAD_DIM)
        k = jnp.concatenate([r[:, :, hd] for r in k_refs], axis=0).reshape(B_NK, HEAD_DIM)
        v = jnp.concatenate([r[:, :, hd] for r in v_refs], axis=0).reshape(B_NK, HEAD_DIM)
        s = lax.dot_general(q, k, (((1,), (1,)), ((), ())),
                            preferred_element_type=F32) + bias_ref[0, h]
        m = jnp.max(s, axis=-1, keepdims=True)
        p = jnp.exp(s - m).astype(BF16)
        o = jnp.dot(p, jnp.concatenate([v, ones], axis=1), preferred_element_type=F32)
        o = o[:, :HEAD_DIM] / o[:, HEAD_DIM:]
        o_ref[:, :, hd] = o.reshape(B_QR, GRID_W, HEAD_DIM).astype(BF16)


def _attn_b_bias(rpb):
    span = 2 * GRID_W
    lead = GRID_W - WIN_W
    strip = jnp.pad(rpb.astype(F32), ((0, 0), (B_KR, B_KR), (lead, span - lead - rpb.shape[2])))
    heads, rows_p = strip.shape[0], strip.shape[1]
    flat = jnp.broadcast_to(strip[:, :, None, :], (heads, rows_p, GRID_W, span))
    flat = flat.reshape(heads, rows_p, GRID_W * span)[:, :, :GRID_W * (span - 1)]
    toeplitz = flat.reshape(heads, rows_p, GRID_W, span - 1)[..., GRID_W - 1:]

    qc = np.arange(GRID_W)
    cs = np.clip(qc - WIN_W // 2, 0, GRID_W - WIN_W)
    col_ok = (qc[None, :] >= cs[:, None]) & (qc[None, :] < cs[:, None] + WIN_W)
    row_ok, per_pattern = [], []
    for rb in B_PATTERN_ROWBLOCKS:
        ks = int(np.clip(rb - 1, 0, B_KS_MAX))
        qr = rb * B_QR + np.arange(B_QR)
        kr = ks * B_QR + np.arange(B_KR)
        rs = np.clip(qr - WIN_H // 2, 0, GRID_ROWS - WIN_H)
        row_ok.append((kr[None, :] >= rs[:, None]) & (kr[None, :] < rs[:, None] + WIN_H))
        starts = B_KR + kr[0] - qr + WIN_H - 1
        per_pattern.append(jnp.stack([toeplitz[:, s:s + B_KR] for s in starts]))
    bias = jnp.stack(per_pattern).transpose(0, 2, 1, 4, 3, 5)
    ok = np.stack(row_ok)[:, None, :, None, :, None] & col_ok[None, None, None, :, None, :]
    bias = jnp.where(jnp.asarray(ok), bias, NEG)
    return bias.reshape(len(B_PATTERN_ROWBLOCKS), B_HEADS, B_NQ, B_NK)


def _attn_b(qk3, vals3, bias, batch):
    rows = qk3.shape[0]

    def kv_map(col, t):
        return lambda b, r: (b * B_NRB + jnp.clip(r - 1, 0, B_KS_MAX) + t, 0, col)

    def pattern(r):
        return jnp.where(r == 0, 0, jnp.where(r == B_NRB - 1, 2, 1))

    blk = lambda m: pl.BlockSpec((B_QR, GRID_W, B_DIM), m)
    return pl.pallas_call(
        _attn_b_kernel,
        out_shape=jax.ShapeDtypeStruct((rows, GRID_W, B_DIM), BF16),
        grid=(batch, B_NRB),
        in_specs=[
            blk(lambda b, r: (b * B_NRB + r, 0, 0)),
            blk(kv_map(1, 0)), blk(kv_map(1, 1)), blk(kv_map(1, 2)),
            blk(kv_map(0, 0)), blk(kv_map(0, 1)), blk(kv_map(0, 2)),
            pl.BlockSpec((1, B_HEADS, B_NQ, B_NK), lambda b, r: (pattern(r), 0, 0, 0)),
        ],
        out_specs=blk(lambda b, r: (b * B_NRB + r, 0, 0)),
        compiler_params=_params("arbitrary", "arbitrary"),
        name="attn_b",
    )(qk3, qk3, qk3, qk3, vals3, vals3, vals3, bias)


MG_TM = 512


def _merge_kernel(x_ref, oa_ref, ob_ref, ga_ref, gb_ref, wa_ref, wb_ref, wo_ref, g_ref,
                  x1_ref, h2_ref):
    a = jnp.dot(oa_ref[...], wa_ref[...], preferred_element_type=F32)
    b = jnp.dot(ob_ref[...], wb_ref[...], preferred_element_type=F32)
    merged = ga_ref[...].astype(F32) * a + gb_ref[...].astype(F32) * b
    y = jnp.dot(merged.astype(BF16), wo_ref[...], preferred_element_type=F32)
    x1 = x_ref[...] + y
    x1_ref[...] = x1
    h2_ref[...] = _rmsnorm(x1, g_ref[...]).astype(BF16)


def _merge(x2, out_a, out_b, gates, wa, wb, wo, norm_g):
    n = x2.shape[0]
    row_block = lambda width, col=0: pl.BlockSpec((MG_TM, width), lambda i: (i, col))
    return pl.pallas_call(
        _merge_kernel,
        out_shape=(jax.ShapeDtypeStruct((n, D_MODEL), F32),
                   jax.ShapeDtypeStruct((n, D_MODEL), BF16)),
        grid=(n // MG_TM,),
        in_specs=[
            row_block(D_MODEL), row_block(A_Q_DIM), row_block(B_DIM),
            row_block(D_MODEL, 0), row_block(D_MODEL, 1),
            _resident((A_Q_DIM, D_MODEL)), _resident((B_DIM, D_MODEL)),
            _resident((D_MODEL, D_MODEL)), _resident((1, D_MODEL)),
        ],
        out_specs=(row_block(D_MODEL), row_block(D_MODEL)),
        compiler_params=_params("arbitrary"),
        name="merge",
    )(x2, out_a, out_b, gates, gates, wa, wb, wo, norm_g)


UP_TM = 1024
UP_NJ = 4
UP_TN = D_FF // UP_NJ
UP_LANES = SUB // 2
UP_NSUB = UP_TN // UP_LANES
HALO = 16
UP_ROWS = UP_TM + 2 * HALO


def _ffn_up_kernel(hp_ref, hm_ref, hn_ref, w_ref, cw_ref, cb_ref, o_ref, lhs_ref, u_ref):
    i = pl.program_id(0)
    tiles_per_seq = SEQ // UP_TM

    @pl.when(pl.program_id(1) == 0)
    def _():
        first = (i % tiles_per_seq) == 0
        last = (i % tiles_per_seq) == tiles_per_seq - 1
        lhs_ref[0:HALO, :] = jnp.where(first, jnp.zeros_like(hp_ref[...]), hp_ref[...])
        lhs_ref[HALO:HALO + UP_TM, :] = hm_ref[...]
        lhs_ref[HALO + UP_TM:, :] = jnp.where(last, jnp.zeros_like(hn_ref[...]), hn_ref[...])

    lhs = lhs_ref[...]
    for k in range(UP_NSUB):
        cols = slice(k * SUB, (k + 1) * SUB)
        u = u_ref.at[k % 2]
        u[...] = jnp.dot(lhs, w_ref[:, cols], preferred_element_type=F32)
        below = u[HALO - 1:HALO - 1 + UP_TM, :]
        mid = u[HALO:HALO + UP_TM, :]
        above = u[HALO + 1:HALO + 1 + UP_TM, :]
        y = mid * cw_ref[1:2, cols] + cb_ref[:, cols]
        y = y + below * cw_ref[0:1, cols]
        y = y + above * cw_ref[2:3, cols]
        half_gate, val = 0.5 * y[:, :UP_LANES], y[:, UP_LANES:]
        silu = half_gate * jnp.tanh(half_gate) + half_gate
        o_ref[:, k * UP_LANES:(k + 1) * UP_LANES] = (silu * val).astype(BF16)


def _interleave_gate_val(a):
    lead = a.shape[:-1]
    a = a.reshape(*lead, 2, D_FF // UP_LANES, UP_LANES)
    return jnp.swapaxes(a, -3, -2).reshape(*lead, 2 * D_FF)


def _ffn_up(h2, w_up, conv_w, conv_b):
    n = h2.shape[0]
    halo_blocks = UP_TM // HALO
    last_halo = n // HALO - 1
    return pl.pallas_call(
        _ffn_up_kernel,
        out_shape=jax.ShapeDtypeStruct((n, D_FF), BF16),
        grid=(n // UP_TM, UP_NJ),
        in_specs=[
            pl.BlockSpec((HALO, D_MODEL), lambda i, j: (jnp.maximum(i * halo_blocks - 1, 0), 0)),
            pl.BlockSpec((UP_TM, D_MODEL), lambda i, j: (i, 0)),
            pl.BlockSpec((HALO, D_MODEL),
                         lambda i, j: (jnp.minimum((i + 1) * halo_blocks, last_halo), 0)),
            pl.BlockSpec((D_MODEL, 2 * UP_TN), lambda i, j: (0, j)),
            pl.BlockSpec((3, 2 * UP_TN), lambda i, j: (0, j)),
            pl.BlockSpec((1, 2 * UP_TN), lambda i, j: (0, j)),
        ],
        out_specs=pl.BlockSpec((UP_TM, UP_TN), lambda i, j: (i, j)),
        scratch_shapes=[pltpu.VMEM((UP_ROWS, D_MODEL), BF16),
                        pltpu.VMEM((2, UP_ROWS, SUB), F32)],
        compiler_params=_params("arbitrary", "arbitrary"),
        name="ffn_up",
    )(h2, h2, h2, w_up, conv_w, conv_b)


DN_TM = 1024
DN_TN = 512


def _ffn_down_kernel(a_ref, w_ref, x_ref, o_ref):
    o_ref[...] = x_ref[...] + jnp.dot(a_ref[...], w_ref[...], preferred_element_type=F32)


def _ffn_down(act, w_down, x1):
    n = act.shape[0]
    return pl.pallas_call(
        _ffn_down_kernel,
        out_shape=jax.ShapeDtypeStruct((n, D_MODEL), F32),
        grid=(n // DN_TM, D_MODEL // DN_TN),
        in_specs=[
            pl.BlockSpec((DN_TM, D_FF), lambda i, j: (i, 0)),
            pl.BlockSpec((D_FF, DN_TN), lambda i, j: (0, j)),
            pl.BlockSpec((DN_TM, DN_TN), lambda i, j: (i, j)),
        ],
        out_specs=pl.BlockSpec((DN_TM, DN_TN), lambda i, j: (i, j)),
        compiler_params=_params("arbitrary", "arbitrary"),
        name="ffn_down",
    )(act, w_down, x1)


def _rope_tables():
    half = HEAD_DIM // 2
    pos = jnp.arange(SEQ, dtype=F32)
    inv_freq = ROPE_THETA ** (-jnp.arange(half, dtype=F32) * (2.0 / HEAD_DIM))
    ang = pos[:, None] * inv_freq[None, :]
    cos, sin = jnp.cos(ang), jnp.sin(ang)
    return jnp.concatenate([cos, cos], axis=-1), jnp.concatenate([-sin, sin], axis=-1)


def _split_in_weights(w):
    splits = np.cumsum([A_Q_DIM, A_KV_DIM, A_KV_DIM, B_DIM, B_DIM, B_DIM, D_MODEL])
    qa, ka, va, qb, kb, vb, ga, gb = jnp.split(w, splits.tolist(), axis=-1)
    cat = lambda *parts: jnp.concatenate(parts, axis=-1).astype(BF16)
    return cat(ga, gb), cat(qa, ka), cat(qb, kb), cat(vb, va)


def kernel(x, norm_mix, w_in, a_q_norm, a_k_norm, a_sink, b_q_norm, b_k_norm, b_rpb,
           w_branch_a, w_branch_b, w_out, norm_ffn, w_up, conv_w, conv_b, w_down):
    batch, seq, d_model = x.shape
    assert (seq, d_model) == (SEQ, D_MODEL)
    n = batch * seq
    cos, sin_signed = _rope_tables()
    scale = 1.0 / math.sqrt(HEAD_DIM)
    seq_tiles = SEQ // IN_TM
    pos_spec = pl.BlockSpec((IN_TM, HEAD_DIM), lambda i: (i % seq_tiles, 0))
    gain_spec = _resident((1, HEAD_DIM))
    x2 = x.reshape(n, d_model)
    for l in range(norm_mix.shape[0]):
        w_gate, w_rope, w_norm, w_plain = _split_in_weights(w_in[l])
        gates, h = _in_gate(x2, norm_mix[l][None], w_gate)
        qk_a = _in_heads(_in_rope_kernel, "in_rope", h, w_rope,
                         (cos, sin_signed, a_q_norm[l][None] * scale, a_k_norm[l][None]),
                         [pos_spec, pos_spec, gain_spec, gain_spec])
        qk_b = _in_heads(_in_norm_kernel, "in_norm", h, w_norm,
                         (b_q_norm[l][None] * scale, b_k_norm[l][None]), [gain_spec, gain_spec])
        vals = _in_heads(_in_plain_kernel, "in_plain", h, w_plain, (), [])
        out_a = _attn_a(qk_a, vals, a_sink[l], batch)
        out_b = _attn_b(qk_b.reshape(n // GRID_W, GRID_W, NORM_DIM),
                        vals.reshape(n // GRID_W, GRID_W, PLAIN_DIM), _attn_b_bias(b_rpb[l]), batch)
        x1, h2 = _merge(x2, out_a, out_b.reshape(n, B_DIM), gates,
                        w_branch_a[l].astype(BF16), w_branch_b[l].astype(BF16),
                        w_out[l].astype(BF16), norm_ffn[l][None])
        act = _ffn_up(h2, _interleave_gate_val(w_up[l]).astype(BF16),
                      _interleave_gate_val(conv_w[l]), _interleave_gate_val(conv_b[l][None]))
        x2 = _ffn_down(act, w_down[l].astype(BF16), x1)
    return x2.reshape(batch, seq, d_model)
```

```python
import math

import numpy as np
import jax
import jax.numpy as jnp
from jax import lax
from jax.experimental import pallas as pl
from jax.experimental.pallas import tpu as pltpu

D_MODEL = 2048
SEQ = 4096
HEAD_DIM = 128
A_Q_HEADS = 8
A_KV_HEADS = 2
A_GROUP = A_Q_HEADS // A_KV_HEADS
WINDOW = 128
B_HEADS = 8
GRID_W = 64
WIN_H = 8
WIN_W = 16
D_FF = 5632
ROPE_THETA = 10000.0
EPS = 1e-6
NEG = -1e30

A_Q_DIM = A_Q_HEADS * HEAD_DIM
A_KV_DIM = A_KV_HEADS * HEAD_DIM
B_DIM = B_HEADS * HEAD_DIM

VMEM_LIMIT_BYTES = 56 * 1024 * 1024
MXU_COLS = 256
SUB = MXU_COLS
HEADS_PER_SUB = SUB // HEAD_DIM

BF16 = jnp.bfloat16
F32 = jnp.float32


def _params(*semantics):
    return pltpu.CompilerParams(dimension_semantics=semantics, vmem_limit_bytes=VMEM_LIMIT_BYTES)


def _resident(shape):
    return pl.BlockSpec(shape, lambda *_: (0,) * len(shape), pipeline_mode=pl.Buffered(1))


def _rmsnorm(x, gain):
    ms = jnp.mean(x * x, axis=-1, keepdims=True)
    return x * lax.rsqrt(ms + EPS) * gain


def _rope(y, cos, sin_signed):
    return y * cos + pltpu.roll(y, HEAD_DIM // 2, axis=1) * sin_signed


def _sigmoid(x):
    return 0.5 * jnp.tanh(0.5 * x) + 0.5


IN_TM = 1024
GATE_TM = 512
GATE_DIM = 2 * D_MODEL
ROPE_DIM = A_Q_DIM + A_KV_DIM
NORM_DIM = 2 * B_DIM
PLAIN_DIM = B_DIM + A_KV_DIM


def _in_gate_kernel(x_ref, g_ref, w_ref, o_ref, h_ref):
    h_ref[...] = _rmsnorm(x_ref[...], g_ref[...]).astype(BF16)
    lhs = h_ref[...]
    for k in range(GATE_DIM // SUB):
        cols = slice(k * SUB, (k + 1) * SUB)
        acc = jnp.dot(lhs, w_ref[:, cols], preferred_element_type=F32)
        o_ref[:, cols] = _sigmoid(acc).astype(BF16)


def _in_gate(x2, norm_g, w_gate):
    n = x2.shape[0]
    row_block = lambda width: pl.BlockSpec((GATE_TM, width), lambda i: (i, 0))
    return pl.pallas_call(
        _in_gate_kernel,
        out_shape=(jax.ShapeDtypeStruct((n, GATE_DIM), BF16),
                   jax.ShapeDtypeStruct((n, D_MODEL), BF16)),
        grid=(n // GATE_TM,),
        in_specs=[row_block(D_MODEL), _resident((1, D_MODEL)), _resident((D_MODEL, GATE_DIM))],
        out_specs=(row_block(GATE_DIM), row_block(D_MODEL)),
        compiler_params=_params("arbitrary"),
        name="in_gate",
    )(x2, norm_g, w_gate)


def _store_heads(o_ref, k, acc, fn):
    for t in range(HEADS_PER_SUB):
        col = k * SUB + t * HEAD_DIM
        o_ref[:, col:col + HEAD_DIM] = fn(acc[:, t * HEAD_DIM:(t + 1) * HEAD_DIM]).astype(BF16)


def _in_rope_kernel(h_ref, w_ref, cos_ref, sin_ref, gq_ref, gk_ref, o_ref):
    lhs = h_ref[...]
    for k in range(ROPE_DIM // SUB):
        acc = jnp.dot(lhs, w_ref[:, k * SUB:(k + 1) * SUB], preferred_element_type=F32)
        gain_ref = gq_ref if k * SUB < A_Q_DIM else gk_ref
        _store_heads(o_ref, k, acc,
                     lambda a: _rope(_rmsnorm(a, gain_ref[...]), cos_ref[...], sin_ref[...]))


def _in_norm_kernel(h_ref, w_ref, gq_ref, gk_ref, o_ref):
    lhs = h_ref[...]
    for k in range(NORM_DIM // SUB):
        acc = jnp.dot(lhs, w_ref[:, k * SUB:(k + 1) * SUB], preferred_element_type=F32)
        gain_ref = gq_ref if k * SUB < B_DIM else gk_ref
        _store_heads(o_ref, k, acc, lambda a: _rmsnorm(a, gain_ref[...]))


def _in_plain_kernel(h_ref, wvb_ref, wva_ref, o_ref):
    lhs = h_ref[...]
    for k in range(PLAIN_DIM // SUB):
        cols = slice(k * SUB, (k + 1) * SUB)
        w = wvb_ref[:, cols] if k * SUB < B_DIM else wva_ref[:, k * SUB - B_DIM:(k + 1) * SUB - B_DIM]
        o_ref[:, cols] = jnp.dot(lhs, w, preferred_element_type=F32).astype(BF16)


def _in_heads(body, name, h, weights, extra, extra_specs):
    n = h.shape[0]
    width = sum(w.shape[1] for w in weights)
    return pl.pallas_call(
        body,
        out_shape=jax.ShapeDtypeStruct((n, width), BF16),
        grid=(n // IN_TM,),
        in_specs=([pl.BlockSpec((IN_TM, D_MODEL), lambda i: (i, 0))]
                  + [_resident(w.shape) for w in weights] + extra_specs),
        out_specs=pl.BlockSpec((IN_TM, width), lambda i: (i, 0)),
        compiler_params=_params("arbitrary"),
        name=name,
    )(h, *weights, *extra)


A_TQ = 512
A_BLK = WINDOW
A_NB = A_TQ // A_BLK
A_CHUNKS = SEQ // A_TQ


def _attn_a_kernel(sink_ref, q_ref, kp_ref, km_ref, kn_ref, vp_ref, vm_ref, vn_ref, o_ref):
    c = pl.program_id(1)
    k_all = jnp.concatenate([kp_ref[...], km_ref[...], kn_ref[...]], axis=0)
    v_all = jnp.concatenate([vp_ref[...], vm_ref[...], vn_ref[...]], axis=0)
    qq = lax.broadcasted_iota(jnp.int32, (A_BLK, 3 * A_BLK), 0)
    kk = lax.broadcasted_iota(jnp.int32, (A_BLK, 3 * A_BLK), 1)
    d = kk - qq
    band = (d >= 0) & (d <= 2 * WINDOW)
    ones = jnp.ones((3 * A_BLK, HEAD_DIM), BF16)
    for n in range(A_NB):
        valid = band
        if n == 0:
            valid = valid & ((kk >= A_BLK) | (c > 0))
        if n == A_NB - 1:
            valid = valid & ((kk < 2 * A_BLK) | (c < A_CHUNKS - 1))
        for h in range(A_KV_HEADS):
            k = k_all[n * A_BLK:(n + 3) * A_BLK, h * HEAD_DIM:(h + 1) * HEAD_DIM]
            v = v_all[n * A_BLK:(n + 3) * A_BLK, h * HEAD_DIM:(h + 1) * HEAD_DIM]
            heads = [h * A_GROUP + g for g in range(A_GROUP)]
            qs = jnp.concatenate(
                [q_ref[n * A_BLK:(n + 1) * A_BLK, t * HEAD_DIM:(t + 1) * HEAD_DIM] for t in heads],
                axis=0)
            s = lax.dot_general(qs, k, (((1,), (1,)), ((), ())),
                                preferred_element_type=F32)
            ps, sink_terms = [], []
            for g, t in enumerate(heads):
                sg = jnp.where(valid, s[g * A_BLK:(g + 1) * A_BLK], NEG)
                m = jnp.maximum(jnp.max(sg, axis=-1, keepdims=True), sink_ref[t])
                ps.append(jnp.exp(sg - m).astype(BF16))
                sink_terms.append(jnp.exp(sink_ref[t] - m))
            o = jnp.dot(jnp.concatenate(ps, axis=0), jnp.concatenate([v, ones], axis=1),
                        preferred_element_type=F32)
            for g, t in enumerate(heads):
                og = o[g * A_BLK:(g + 1) * A_BLK]
                og = og[:, :HEAD_DIM] / (og[:, HEAD_DIM:] + sink_terms[g])
                o_ref[n * A_BLK:(n + 1) * A_BLK, t * HEAD_DIM:(t + 1) * HEAD_DIM] = og.astype(BF16)


def _attn_a(qk, vals, sink, batch):
    n = qk.shape[0]
    blk_per_seq = SEQ // A_BLK
    k_col = A_Q_DIM // A_KV_DIM
    v_col = B_DIM // A_KV_DIM

    def prev_map(col):
        return lambda b, c, *_: (b * blk_per_seq + jnp.maximum(c * A_NB - 1, 0), col)

    def main_map(col):
        return lambda b, c, *_: (b * A_CHUNKS + c, col)

    def next_map(col):
        return lambda b, c, *_: (b * blk_per_seq + jnp.minimum(c * A_NB + A_NB, blk_per_seq - 1), col)

    halo = lambda m: pl.BlockSpec((A_BLK, A_KV_DIM), m)
    main = lambda m: pl.BlockSpec((A_TQ, A_KV_DIM), m)
    return pl.pallas_call(
        _attn_a_kernel,
        out_shape=jax.ShapeDtypeStruct((n, A_Q_DIM), BF16),
        grid_spec=pltpu.PrefetchScalarGridSpec(
            num_scalar_prefetch=1,
            grid=(batch, A_CHUNKS),
            in_specs=[
                pl.BlockSpec((A_TQ, A_Q_DIM), main_map(0)),
                halo(prev_map(k_col)), main(main_map(k_col)), halo(next_map(k_col)),
                halo(prev_map(v_col)), main(main_map(v_col)), halo(next_map(v_col)),
            ],
            out_specs=pl.BlockSpec((A_TQ, A_Q_DIM), lambda b, c, *_: (b * A_CHUNKS + c, 0)),
        ),
        compiler_params=_params("arbitrary", "arbitrary"),
        name="attn_a",
    )(sink, qk, qk, qk, qk, vals, vals, vals)


GRID_ROWS = SEQ // GRID_W
B_QR = 4
B_KR = B_QR + WIN_H
B_NRB = GRID_ROWS // B_QR
B_NQ = B_QR * GRID_W
B_NK = B_KR * GRID_W
B_KBLOCKS = B_KR // B_QR
B_KS_MAX = B_NRB - B_KBLOCKS
B_PATTERN_ROWBLOCKS = (0, 1, B_NRB - 1)


def _attn_b_row_windows():
    starts, row_ok = [], []
    for rb in B_PATTERN_ROWBLOCKS:
        ks = int(np.clip(rb - 1, 0, B_KS_MAX))
        qr = rb * B_QR + np.arange(B_QR)
        kr = ks * B_QR + np.arange(B_KR)
        rs = np.clip(qr - WIN_H // 2, 0, GRID_ROWS - WIN_H)
        row_ok.append((kr[None, :] >= rs[:, None]) & (kr[None, :] < rs[:, None] + WIN_H))
        starts.append(kr[0] - qr + WIN_H - 1)
    return np.stack(starts), np.stack(row_ok)


B_ROW_START, B_ROW_OK = _attn_b_row_windows()
B_STRIP_LO = max(0, -int(B_ROW_START.min()))
B_NSTRIP = int(B_ROW_START.max()) + B_KR - 1 + B_STRIP_LO
B_LANES = 2 * GRID_W


def _attn_b_build_bias(strip_ref, bias_ref):
    qc = lax.broadcasted_iota(jnp.int32, (GRID_W, B_LANES), 0)
    lane = lax.broadcasted_iota(jnp.int32, (GRID_W, B_LANES), 1)
    kc = lane % GRID_W
    cs = jnp.clip(qc - WIN_W // 2, 0, GRID_W - WIN_W)
    col_ok = (kc >= cs) & (kc < cs + WIN_W)
    first_row = lane < GRID_W
    for h in range(B_HEADS):
        tiles = {}

        def tile(i):
            if i not in tiles:
                rows = jnp.broadcast_to(strip_ref[h, i:i + 1, :], (GRID_W, B_LANES))
                toeplitz = pltpu.roll(rows, 0, axis=1, stride=1, stride_axis=0)
                tiles[i] = jnp.where(col_ok, toeplitz, NEG)
            return tiles[i]

        for p in range(len(B_PATTERN_ROWBLOCKS)):
            for qr in range(B_QR):
                for c in range(B_KR // 2):
                    ok0, ok1 = B_ROW_OK[p, qr, 2 * c], B_ROW_OK[p, qr, 2 * c + 1]
                    if ok0 or ok1:
                        t = tile(int(B_ROW_START[p, qr]) + 2 * c + B_STRIP_LO)
                        if not ok1:
                            t = jnp.where(first_row, t, NEG)
                        if not ok0:
                            t = jnp.where(first_row, NEG, t)
                    else:
                        t = jnp.full((GRID_W, B_LANES), NEG, F32)
                    bias_ref[p, h, qr * GRID_W:(qr + 1) * GRID_W, c * B_LANES:(c + 1) * B_LANES] = t


def _attn_b_kernel(q_ref, k0_ref, k1_ref, k2_ref, v0_ref, v1_ref, v2_ref, strip_ref, o_ref,
                   bias_ref):
    r = pl.program_id(1)

    @pl.when((pl.program_id(0) == 0) & (r == 0))
    def _():
        _attn_b_build_bias(strip_ref, bias_ref)

    pattern = jnp.where(r == 0, 0, jnp.where(r == B_NRB - 1, 2, 1))
    k_refs = (k0_ref, k1_ref, k2_ref)
    v_refs = (v0_ref, v1_ref, v2_ref)
    ones = jnp.ones((B_NK, HEAD_DIM), BF16)
    for h in range(B_HEADS):
        hd = slice(h * HEAD_DIM, (h + 1) * HEAD_DIM)
        q = q_ref[:, :, hd].reshape(B_NQ, HEAD_DIM)
        k = jnp.concatenate([ref[:, :, hd] for ref in k_refs], axis=0).reshape(B_NK, HEAD_DIM)
        v = jnp.concatenate([ref[:, :, hd] for ref in v_refs], axis=0).reshape(B_NK, HEAD_DIM)
        s = lax.dot_general(q, k, (((1,), (1,)), ((), ())),
                            preferred_element_type=F32) + bias_ref[pattern, h]
        m = jnp.max(s, axis=-1, keepdims=True)
        p = jnp.exp(s - m).astype(BF16)
        o = jnp.dot(p, jnp.concatenate([v, ones], axis=1), preferred_element_type=F32)
        o = o[:, :HEAD_DIM] / o[:, HEAD_DIM:]
        o_ref[:, :, hd] = o.reshape(B_QR, GRID_W, HEAD_DIM).astype(BF16)


def _attn_b_strips(rpb):
    rows = B_NSTRIP + 1
    table = jnp.pad(rpb.astype(F32), ((0, 0), (B_STRIP_LO, rows - B_STRIP_LO - rpb.shape[1]), (0, 0)))
    first, second = table[:, :-1], table[:, 1:]
    gap = jnp.zeros(first.shape[:2] + (GRID_W - 2 * WIN_W + 1,), F32)
    strips = jnp.concatenate(
        [first[..., WIN_W - 1:], gap, second, gap, first[..., :WIN_W - 1]], axis=-1)
    assert strips.shape == (B_HEADS, B_NSTRIP, B_LANES)
    return strips


def _attn_b(qk3, vals3, strips, batch):
    rows = qk3.shape[0]

    def kv_map(col, t):
        return lambda b, r: (b * B_NRB + jnp.clip(r - 1, 0, B_KS_MAX) + t, 0, col)

    blk = lambda m: pl.BlockSpec((B_QR, GRID_W, B_DIM), m)
    return pl.pallas_call(
        _attn_b_kernel,
        out_shape=jax.ShapeDtypeStruct((rows, GRID_W, B_DIM), BF16),
        grid=(batch, B_NRB),
        in_specs=[
            blk(lambda b, r: (b * B_NRB + r, 0, 0)),
            blk(kv_map(1, 0)), blk(kv_map(1, 1)), blk(kv_map(1, 2)),
            blk(kv_map(0, 0)), blk(kv_map(0, 1)), blk(kv_map(0, 2)),
            _resident((B_HEADS, B_NSTRIP, B_LANES)),
        ],
        out_specs=blk(lambda b, r: (b * B_NRB + r, 0, 0)),
        scratch_shapes=[pltpu.VMEM((len(B_PATTERN_ROWBLOCKS), B_HEADS, B_NQ, B_NK), F32)],
        compiler_params=_params("arbitrary", "arbitrary"),
        name="attn_b",
    )(qk3, qk3, qk3, qk3, vals3, vals3, vals3, strips)


MG_TM = 512


def _merge_kernel(x_ref, oa_ref, ob_ref, ga_ref, gb_ref, wa_ref, wb_ref, wo_ref, g_ref,
                  x1_ref, h2_ref):
    a = jnp.dot(oa_ref[...], wa_ref[...], preferred_element_type=F32)
    b = jnp.dot(ob_ref[...], wb_ref[...], preferred_element_type=F32)
    merged = ga_ref[...].astype(F32) * a + gb_ref[...].astype(F32) * b
    y = jnp.dot(merged.astype(BF16), wo_ref[...], preferred_element_type=F32)
    x1 = x_ref[...] + y
    x1_ref[...] = x1
    h2_ref[...] = _rmsnorm(x1, g_ref[...]).astype(BF16)


def _merge(x2, out_a, out_b, gates, wa, wb, wo, norm_g):
    n = x2.shape[0]
    row_block = lambda width, col=0: pl.BlockSpec((MG_TM, width), lambda i: (i, col))
    return pl.pallas_call(
        _merge_kernel,
        out_shape=(jax.ShapeDtypeStruct((n, D_MODEL), F32),
                   jax.ShapeDtypeStruct((n, D_MODEL), BF16)),
        grid=(n // MG_TM,),
        in_specs=[
            row_block(D_MODEL), row_block(A_Q_DIM), row_block(B_DIM),
            row_block(D_MODEL, 0), row_block(D_MODEL, 1),
            _resident((A_Q_DIM, D_MODEL)), _resident((B_DIM, D_MODEL)),
            _resident((D_MODEL, D_MODEL)), _resident((1, D_MODEL)),
        ],
        out_specs=(row_block(D_MODEL), row_block(D_MODEL)),
        compiler_params=_params("arbitrary"),
        name="merge",
    )(x2, out_a, out_b, gates, gates, wa, wb, wo, norm_g)


UP_TM = 1024
UP_NJ = 4
UP_TN = D_FF // UP_NJ
UP_LANES = SUB // 2
UP_NSUB = UP_TN // UP_LANES
HALO = 16
UP_ROWS = UP_TM + 2 * HALO


def _ffn_up_kernel(hp_ref, hm_ref, hn_ref, wg_ref, wv_ref, cwg_ref, cwv_ref, cbg_ref, cbv_ref,
                   o_ref, lhs_ref, u_ref):
    i = pl.program_id(0)
    tiles_per_seq = SEQ // UP_TM

    @pl.when(pl.program_id(1) == 0)
    def _():
        first = (i % tiles_per_seq) == 0
        last = (i % tiles_per_seq) == tiles_per_seq - 1
        lhs_ref[0:HALO, :] = jnp.where(first, jnp.zeros_like(hp_ref[...]), hp_ref[...])
        lhs_ref[HALO:HALO + UP_TM, :] = hm_ref[...]
        lhs_ref[HALO + UP_TM:, :] = jnp.where(last, jnp.zeros_like(hn_ref[...]), hn_ref[...])

    def conv(u, lanes, cw_ref, cb_ref, cols):
        below = u[HALO - 1:HALO - 1 + UP_TM, lanes]
        mid = u[HALO:HALO + UP_TM, lanes]
        above = u[HALO + 1:HALO + 1 + UP_TM, lanes]
        y = mid * cw_ref[1:2, cols] + cb_ref[:, cols]
        y = y + below * cw_ref[0:1, cols]
        return y + above * cw_ref[2:3, cols]

    lhs = lhs_ref[...]
    for k in range(UP_NSUB):
        cols = slice(k * UP_LANES, (k + 1) * UP_LANES)
        u = u_ref.at[k % 2]
        w = jnp.concatenate([wg_ref[:, cols], wv_ref[:, cols]], axis=1)
        u[...] = jnp.dot(lhs, w, preferred_element_type=F32)
        half_gate = 0.5 * conv(u, slice(0, UP_LANES), cwg_ref, cbg_ref, cols)
        val = conv(u, slice(UP_LANES, SUB), cwv_ref, cbv_ref, cols)
        silu = half_gate * jnp.tanh(half_gate) + half_gate
        o_ref[:, cols] = (silu * val).astype(BF16)


def _ffn_up(h2, w_up, conv_w, conv_b):
    n = h2.shape[0]
    halo_blocks = UP_TM // HALO
    last_halo = n // HALO - 1
    return pl.pallas_call(
        _ffn_up_kernel,
        out_shape=jax.ShapeDtypeStruct((n, D_FF), BF16),
        grid=(n // UP_TM, UP_NJ),
        in_specs=[
            pl.BlockSpec((HALO, D_MODEL), lambda i, j: (jnp.maximum(i * halo_blocks - 1, 0), 0)),
            pl.BlockSpec((UP_TM, D_MODEL), lambda i, j: (i, 0)),
            pl.BlockSpec((HALO, D_MODEL),
                         lambda i, j: (jnp.minimum((i + 1) * halo_blocks, last_halo), 0)),
            pl.BlockSpec((D_MODEL, UP_TN), lambda i, j: (0, j)),
            pl.BlockSpec((D_MODEL, UP_TN), lambda i, j: (0, UP_NJ + j)),
            pl.BlockSpec((3, UP_TN), lambda i, j: (0, j)),
            pl.BlockSpec((3, UP_TN), lambda i, j: (0, UP_NJ + j)),
            pl.BlockSpec((1, UP_TN), lambda i, j: (0, j)),
            pl.BlockSpec((1, UP_TN), lambda i, j: (0, UP_NJ + j)),
        ],
        out_specs=pl.BlockSpec((UP_TM, UP_TN), lambda i, j: (i, j)),
        scratch_shapes=[pltpu.VMEM((UP_ROWS, D_MODEL), BF16),
                        pltpu.VMEM((2, UP_ROWS, SUB), F32)],
        compiler_params=_params("arbitrary", "arbitrary"),
        name="ffn_up",
    )(h2, h2, h2, w_up, w_up, conv_w, conv_w, conv_b, conv_b)


DN_TM = 1024
DN_TN = 512


def _ffn_down_kernel(a_ref, w_ref, x_ref, o_ref):
    o_ref[...] = x_ref[...] + jnp.dot(a_ref[...], w_ref[...], preferred_element_type=F32)


def _ffn_down(act, w_down, x1):
    n = act.shape[0]
    return pl.pallas_call(
        _ffn_down_kernel,
        out_shape=jax.ShapeDtypeStruct((n, D_MODEL), F32),
        grid=(n // DN_TM, D_MODEL // DN_TN),
        in_specs=[
            pl.BlockSpec((DN_TM, D_FF), lambda i, j: (i, 0)),
            pl.BlockSpec((D_FF, DN_TN), lambda i, j: (0, j)),
            pl.BlockSpec((DN_TM, DN_TN), lambda i, j: (i, j)),
        ],
        out_specs=pl.BlockSpec((DN_TM, DN_TN), lambda i, j: (i, j)),
        compiler_params=_params("arbitrary", "arbitrary"),
        name="ffn_down",
    )(act, w_down, x1)


def _rope_tables():
    half = HEAD_DIM // 2
    pos = jnp.arange(SEQ, dtype=F32)
    inv_freq = ROPE_THETA ** (-jnp.arange(half, dtype=F32) * (2.0 / HEAD_DIM))
    ang = pos[:, None] * inv_freq[None, :]
    cos, sin = jnp.cos(ang), jnp.sin(ang)
    return jnp.concatenate([cos, cos], axis=-1), jnp.concatenate([-sin, sin], axis=-1)


def _split_in_weights(w):
    c_va = ROPE_DIM
    c_qb = c_va + A_KV_DIM
    c_vb = c_qb + NORM_DIM
    c_g = c_vb + B_DIM
    cut = lambda lo, hi: w[:, lo:hi].astype(BF16)
    return cut(c_g, c_g + GATE_DIM), cut(0, c_va), cut(c_qb, c_vb), cut(c_vb, c_g), cut(c_va, c_qb)


def kernel(x, norm_mix, w_in, a_q_norm, a_k_norm, a_sink, b_q_norm, b_k_norm, b_rpb,
           w_branch_a, w_branch_b, w_out, norm_ffn, w_up, conv_w, conv_b, w_down):
    batch, seq, d_model = x.shape
    assert (seq, d_model) == (SEQ, D_MODEL)
    n = batch * seq
    cos, sin_signed = _rope_tables()
    scale = 1.0 / math.sqrt(HEAD_DIM)
    seq_tiles = SEQ // IN_TM
    pos_spec = pl.BlockSpec((IN_TM, HEAD_DIM), lambda i: (i % seq_tiles, 0))
    gain_spec = _resident((1, HEAD_DIM))
    x2 = x.reshape(n, d_model)
    for l in range(norm_mix.shape[0]):
        w_gate, w_rope, w_norm, w_vb, w_va = _split_in_weights(w_in[l])
        gates, h = _in_gate(x2, norm_mix[l][None], w_gate)
        qk_a = _in_heads(_in_rope_kernel, "in_rope", h, (w_rope,),
                         (cos, sin_signed, a_q_norm[l][None] * scale, a_k_norm[l][None]),
                         [pos_spec, pos_spec, gain_spec, gain_spec])
        qk_b = _in_heads(_in_norm_kernel, "in_norm", h, (w_norm,),
                         (b_q_norm[l][None] * scale, b_k_norm[l][None]), [gain_spec, gain_spec])
        vals = _in_heads(_in_plain_kernel, "in_plain", h, (w_vb, w_va), (), [])
        out_a = _attn_a(qk_a, vals, a_sink[l], batch)
        out_b = _attn_b(qk_b.reshape(n // GRID_W, GRID_W, NORM_DIM),
                        vals.reshape(n // GRID_W, GRID_W, PLAIN_DIM), _attn_b_strips(b_rpb[l]), batch)
        x1, h2 = _merge(x2, out_a, out_b.reshape(n, B_DIM), gates,
                        w_branch_a[l].astype(BF16), w_branch_b[l].astype(BF16),
                        w_out[l].astype(BF16), norm_ffn[l][None])
        act = _ffn_up(h2, w_up[l].astype(BF16), conv_w[l], conv_b[l][None])
        x2 = _ffn_down(act, w_down[l].astype(BF16), x1)
    return x2.reshape(batch, seq, d_model)
```

```python
import math

import numpy as np
import jax
import jax.numpy as jnp
from jax import lax
from jax.experimental import pallas as pl
from jax.experimental.pallas import tpu as pltpu

D_MODEL = 2048
SEQ = 4096
HEAD_DIM = 128
A_Q_HEADS = 8
A_KV_HEADS = 2
A_GROUP = A_Q_HEADS // A_KV_HEADS
WINDOW = 128
B_HEADS = 8
GRID_W = 64
WIN_H = 8
WIN_W = 16
D_FF = 5632
ROPE_THETA = 10000.0
EPS = 1e-6
NEG = -1e30

A_Q_DIM = A_Q_HEADS * HEAD_DIM
A_KV_DIM = A_KV_HEADS * HEAD_DIM
B_DIM = B_HEADS * HEAD_DIM

VMEM_LIMIT_BYTES = 56 * 1024 * 1024
MXU_COLS = 256
SUB = MXU_COLS
HEADS_PER_SUB = SUB // HEAD_DIM

BF16 = jnp.bfloat16
F32 = jnp.float32


def _params(*semantics):
    return pltpu.CompilerParams(dimension_semantics=semantics, vmem_limit_bytes=VMEM_LIMIT_BYTES)


def _resident(shape):
    return pl.BlockSpec(shape, lambda *_: (0,) * len(shape), pipeline_mode=pl.Buffered(1))


def _rmsnorm(x, gain):
    ms = jnp.mean(x * x, axis=-1, keepdims=True)
    return x * lax.rsqrt(ms + EPS) * gain


class _CastJob:
    def __init__(self, src, steps, col_ranges=None):
        rows, cols = src.shape
        self.src = src
        self.col_ranges = col_ranges or [(0, cols)]
        self.slab = rows // steps
        assert self.slab * steps == rows
        self.span = max(hi for _, hi in self.col_ranges)

    def in_spec(self):
        return pl.BlockSpec((self.slab, self.span), lambda i: (i, 0))

    def out_specs(self):
        return [pl.BlockSpec((self.slab, hi - lo), lambda i: (i, 0)) for lo, hi in self.col_ranges]

    def out_shapes(self):
        return [jax.ShapeDtypeStruct((self.src.shape[0], hi - lo), BF16) for lo, hi in self.col_ranges]

    def run(self, src_ref, out_refs):
        for (lo, hi), out_ref in zip(self.col_ranges, out_refs):
            out_ref[...] = src_ref[:, lo:hi].astype(BF16)


def _with_casts(body, jobs, n_in, n_out):
    n_cast_out = sum(len(j.col_ranges) for j in jobs)

    def wrapped(*refs):
        ins = refs[:n_in]
        srcs = refs[n_in:n_in + len(jobs)]
        outs = refs[n_in + len(jobs):n_in + len(jobs) + n_out]
        cast_outs = list(refs[n_in + len(jobs) + n_out:n_in + len(jobs) + n_out + n_cast_out])
        scratch = refs[n_in + len(jobs) + n_out + n_cast_out:]
        for job, src_ref in zip(jobs, srcs):
            job.run(src_ref, [cast_outs.pop(0) for _ in job.col_ranges])
        body(*ins, *outs, *scratch)

    return wrapped


def _rope(y, cos, sin_signed):
    return y * cos + pltpu.roll(y, HEAD_DIM // 2, axis=1) * sin_signed


def _sigmoid(x):
    return 0.5 * jnp.tanh(0.5 * x) + 0.5


IN_TM = 1024
GATE_TM = 512
GATE_DIM = 2 * D_MODEL
ROPE_DIM = A_Q_DIM + A_KV_DIM
NORM_DIM = 2 * B_DIM
PLAIN_DIM = B_DIM + A_KV_DIM


def _in_gate_kernel(x_ref, g_ref, w_ref, o_ref, h_ref):
    h_ref[...] = _rmsnorm(x_ref[...], g_ref[...]).astype(BF16)
    lhs = h_ref[...]
    for k in range(GATE_DIM // SUB):
        cols = slice(k * SUB, (k + 1) * SUB)
        acc = jnp.dot(lhs, w_ref[:, cols], preferred_element_type=F32)
        o_ref[:, cols] = _sigmoid(acc).astype(BF16)


def _in_gate(x2, norm_g, w_gate, casts):
    n = x2.shape[0]
    steps = n // GATE_TM
    jobs = [_CastJob(src, steps, ranges) for src, ranges in casts]
    row_block = lambda width: pl.BlockSpec((GATE_TM, width), lambda i: (i, 0))
    return pl.pallas_call(
        _with_casts(_in_gate_kernel, jobs, n_in=3, n_out=2),
        out_shape=[jax.ShapeDtypeStruct((n, GATE_DIM), BF16),
                   jax.ShapeDtypeStruct((n, D_MODEL), BF16)]
                  + [s for j in jobs for s in j.out_shapes()],
        grid=(steps,),
        in_specs=[row_block(D_MODEL), _resident((1, D_MODEL)), _resident((D_MODEL, GATE_DIM))]
                 + [j.in_spec() for j in jobs],
        out_specs=[row_block(GATE_DIM), row_block(D_MODEL)]
                  + [s for j in jobs for s in j.out_specs()],
        compiler_params=_params("arbitrary"),
        name="in_gate",
    )(x2, norm_g, w_gate, *[j.src for j in jobs])


def _store_heads(o_ref, k, acc, fn):
    for t in range(HEADS_PER_SUB):
        col = k * SUB + t * HEAD_DIM
        o_ref[:, col:col + HEAD_DIM] = fn(acc[:, t * HEAD_DIM:(t + 1) * HEAD_DIM]).astype(BF16)


def _in_rope_kernel(h_ref, w_ref, cos_ref, sin_ref, gq_ref, gk_ref, o_ref):
    lhs = h_ref[...]
    for k in range(ROPE_DIM // SUB):
        acc = jnp.dot(lhs, w_ref[:, k * SUB:(k + 1) * SUB], preferred_element_type=F32)
        gain_ref = gq_ref if k * SUB < A_Q_DIM else gk_ref
        _store_heads(o_ref, k, acc,
                     lambda a: _rope(_rmsnorm(a, gain_ref[...]), cos_ref[...], sin_ref[...]))


def _in_norm_kernel(h_ref, w_ref, gq_ref, gk_ref, o_ref):
    lhs = h_ref[...]
    for k in range(NORM_DIM // SUB):
        acc = jnp.dot(lhs, w_ref[:, k * SUB:(k + 1) * SUB], preferred_element_type=F32)
        gain_ref = gq_ref if k * SUB < B_DIM else gk_ref
        _store_heads(o_ref, k, acc, lambda a: _rmsnorm(a, gain_ref[...]))


def _in_plain_kernel(h_ref, wvb_ref, wva_ref, o_ref):
    lhs = h_ref[...]
    for k in range(PLAIN_DIM // SUB):
        cols = slice(k * SUB, (k + 1) * SUB)
        w = wvb_ref[:, cols] if k * SUB < B_DIM else wva_ref[:, k * SUB - B_DIM:(k + 1) * SUB - B_DIM]
        o_ref[:, cols] = jnp.dot(lhs, w, preferred_element_type=F32).astype(BF16)


def _in_heads(body, name, h, weights, extra, extra_specs, casts):
    n = h.shape[0]
    steps = n // IN_TM
    width = sum(w.shape[1] for w in weights)
    jobs = [_CastJob(src, steps) for src in casts]
    return pl.pallas_call(
        _with_casts(body, jobs, n_in=1 + len(weights) + len(extra), n_out=1),
        out_shape=[jax.ShapeDtypeStruct((n, width), BF16)] + [s for j in jobs for s in j.out_shapes()],
        grid=(steps,),
        in_specs=([pl.BlockSpec((IN_TM, D_MODEL), lambda i: (i, 0))]
                  + [_resident(w.shape) for w in weights] + extra_specs
                  + [j.in_spec() for j in jobs]),
        out_specs=([pl.BlockSpec((IN_TM, width), lambda i: (i, 0))]
                   + [s for j in jobs for s in j.out_specs()]),
        compiler_params=_params("arbitrary"),
        name=name,
    )(h, *weights, *extra, *[j.src for j in jobs])


A_TQ = 512
A_BLK = WINDOW
A_NB = A_TQ // A_BLK
A_CHUNKS = SEQ // A_TQ


def _attn_a_kernel(sink_ref, q_ref, kp_ref, km_ref, kn_ref, vp_ref, vm_ref, vn_ref, o_ref):
    c = pl.program_id(1)
    k_all = jnp.concatenate([kp_ref[...], km_ref[...], kn_ref[...]], axis=0)
    v_all = jnp.concatenate([vp_ref[...], vm_ref[...], vn_ref[...]], axis=0)
    qq = lax.broadcasted_iota(jnp.int32, (A_BLK, 3 * A_BLK), 0)
    kk = lax.broadcasted_iota(jnp.int32, (A_BLK, 3 * A_BLK), 1)
    d = kk - qq
    band = (d >= 0) & (d <= 2 * WINDOW)
    ones = jnp.ones((3 * A_BLK, HEAD_DIM), BF16)
    for n in range(A_NB):
        valid = band
        if n == 0:
            valid = valid & ((kk >= A_BLK) | (c > 0))
        if n == A_NB - 1:
            valid = valid & ((kk < 2 * A_BLK) | (c < A_CHUNKS - 1))
        for h in range(A_KV_HEADS):
            k = k_all[n * A_BLK:(n + 3) * A_BLK, h * HEAD_DIM:(h + 1) * HEAD_DIM]
            v = v_all[n * A_BLK:(n + 3) * A_BLK, h * HEAD_DIM:(h + 1) * HEAD_DIM]
            heads = [h * A_GROUP + g for g in range(A_GROUP)]
            qs = jnp.concatenate(
                [q_ref[n * A_BLK:(n + 1) * A_BLK, t * HEAD_DIM:(t + 1) * HEAD_DIM] for t in heads],
                axis=0)
            s = lax.dot_general(qs, k, (((1,), (1,)), ((), ())),
                                preferred_element_type=F32)
            ps, sink_terms = [], []
            for g, t in enumerate(heads):
                sg = jnp.where(valid, s[g * A_BLK:(g + 1) * A_BLK], NEG)
                m = jnp.maximum(jnp.max(sg, axis=-1, keepdims=True), sink_ref[t])
                ps.append(jnp.exp(sg - m).astype(BF16))
                sink_terms.append(jnp.exp(sink_ref[t] - m))
            o = jnp.dot(jnp.concatenate(ps, axis=0), jnp.concatenate([v, ones], axis=1),
                        preferred_element_type=F32)
            for g, t in enumerate(heads):
                og = o[g * A_BLK:(g + 1) * A_BLK]
                og = og[:, :HEAD_DIM] / (og[:, HEAD_DIM:] + sink_terms[g])
                o_ref[n * A_BLK:(n + 1) * A_BLK, t * HEAD_DIM:(t + 1) * HEAD_DIM] = og.astype(BF16)


def _attn_a(qk, vals, sink, batch):
    n = qk.shape[0]
    blk_per_seq = SEQ // A_BLK
    k_col = A_Q_DIM // A_KV_DIM
    v_col = B_DIM // A_KV_DIM

    def prev_map(col):
        return lambda b, c, *_: (b * blk_per_seq + jnp.maximum(c * A_NB - 1, 0), col)

    def main_map(col):
        return lambda b, c, *_: (b * A_CHUNKS + c, col)

    def next_map(col):
        return lambda b, c, *_: (b * blk_per_seq + jnp.minimum(c * A_NB + A_NB, blk_per_seq - 1), col)

    halo = lambda m: pl.BlockSpec((A_BLK, A_KV_DIM), m)
    main = lambda m: pl.BlockSpec((A_TQ, A_KV_DIM), m)
    return pl.pallas_call(
        _attn_a_kernel,
        out_shape=jax.ShapeDtypeStruct((n, A_Q_DIM), BF16),
        grid_spec=pltpu.PrefetchScalarGridSpec(
            num_scalar_prefetch=1,
            grid=(batch, A_CHUNKS),
            in_specs=[
                pl.BlockSpec((A_TQ, A_Q_DIM), main_map(0)),
                halo(prev_map(k_col)), main(main_map(k_col)), halo(next_map(k_col)),
                halo(prev_map(v_col)), main(main_map(v_col)), halo(next_map(v_col)),
            ],
            out_specs=pl.BlockSpec((A_TQ, A_Q_DIM), lambda b, c, *_: (b * A_CHUNKS + c, 0)),
        ),
        compiler_params=_params("arbitrary", "arbitrary"),
        name="attn_a",
    )(sink, qk, qk, qk, qk, vals, vals, vals)


GRID_ROWS = SEQ // GRID_W
B_QR = 4
B_KR = B_QR + WIN_H
B_NRB = GRID_ROWS // B_QR
B_NQ = B_QR * GRID_W
B_NK = B_KR * GRID_W
B_KBLOCKS = B_KR // B_QR
B_KS_MAX = B_NRB - B_KBLOCKS
B_PATTERN_ROWBLOCKS = (0, 1, B_NRB - 1)


def _attn_b_row_windows():
    starts, row_ok = [], []
    for rb in B_PATTERN_ROWBLOCKS:
        ks = int(np.clip(rb - 1, 0, B_KS_MAX))
        qr = rb * B_QR + np.arange(B_QR)
        kr = ks * B_QR + np.arange(B_KR)
        rs = np.clip(qr - WIN_H // 2, 0, GRID_ROWS - WIN_H)
        row_ok.append((kr[None, :] >= rs[:, None]) & (kr[None, :] < rs[:, None] + WIN_H))
        starts.append(kr[0] - qr + WIN_H - 1)
    return np.stack(starts), np.stack(row_ok)


B_ROW_START, B_ROW_OK = _attn_b_row_windows()
B_STRIP_LO = max(0, -int(B_ROW_START.min()))
B_NSTRIP = int(B_ROW_START.max()) + B_KR - 1 + B_STRIP_LO
B_LANES = 2 * GRID_W


def _attn_b_build_bias(strip_ref, bias_ref):
    qc = lax.broadcasted_iota(jnp.int32, (GRID_W, B_LANES), 0)
    lane = lax.broadcasted_iota(jnp.int32, (GRID_W, B_LANES), 1)
    kc = lane % GRID_W
    cs = jnp.clip(qc - WIN_W // 2, 0, GRID_W - WIN_W)
    col_ok = (kc >= cs) & (kc < cs + WIN_W)
    first_row = lane < GRID_W
    for h in range(B_HEADS):
        tiles = {}

        def tile(i):
            if i not in tiles:
                rows = jnp.broadcast_to(strip_ref[h, i:i + 1, :], (GRID_W, B_LANES))
                toeplitz = pltpu.roll(rows, 0, axis=1, stride=1, stride_axis=0)
                tiles[i] = jnp.where(col_ok, toeplitz, NEG)
            return tiles[i]

        for p in range(len(B_PATTERN_ROWBLOCKS)):
            for qr in range(B_QR):
                for c in range(B_KR // 2):
                    ok0, ok1 = B_ROW_OK[p, qr, 2 * c], B_ROW_OK[p, qr, 2 * c + 1]
                    if ok0 or ok1:
                        t = tile(int(B_ROW_START[p, qr]) + 2 * c + B_STRIP_LO)
                        if not ok1:
                            t = jnp.where(first_row, t, NEG)
                        if not ok0:
                            t = jnp.where(first_row, NEG, t)
                    else:
                        t = jnp.full((GRID_W, B_LANES), NEG, F32)
                    bias_ref[p, h, qr * GRID_W:(qr + 1) * GRID_W, c * B_LANES:(c + 1) * B_LANES] = t


def _attn_b_kernel(q_ref, k0_ref, k1_ref, k2_ref, v0_ref, v1_ref, v2_ref, strip_ref, o_ref,
                   bias_ref):
    r = pl.program_id(1)

    @pl.when((pl.program_id(0) == 0) & (r == 0))
    def _():
        _attn_b_build_bias(strip_ref, bias_ref)

    pattern = jnp.where(r == 0, 0, jnp.where(r == B_NRB - 1, 2, 1))
    k_refs = (k0_ref, k1_ref, k2_ref)
    v_refs = (v0_ref, v1_ref, v2_ref)
    ones = jnp.ones((B_NK, HEAD_DIM), BF16)
    for h in range(B_HEADS):
        hd = slice(h * HEAD_DIM, (h + 1) * HEAD_DIM)
        q = q_ref[:, :, hd].reshape(B_NQ, HEAD_DIM)
        k = jnp.concatenate([ref[:, :, hd] for ref in k_refs], axis=0).reshape(B_NK, HEAD_DIM)
        v = jnp.concatenate([ref[:, :, hd] for ref in v_refs], axis=0).reshape(B_NK, HEAD_DIM)
        s = lax.dot_general(q, k, (((1,), (1,)), ((), ())),
                            preferred_element_type=F32) + bias_ref[pattern, h]
        m = jnp.max(s, axis=-1, keepdims=True)
        p = jnp.exp(s - m).astype(BF16)
        o = jnp.dot(p, jnp.concatenate([v, ones], axis=1), preferred_element_type=F32)
        o = o[:, :HEAD_DIM] / o[:, HEAD_DIM:]
        o_ref[:, :, hd] = o.reshape(B_QR, GRID_W, HEAD_DIM).astype(BF16)


def _attn_b_strips(rpb):
    rows = B_NSTRIP + 1
    table = jnp.pad(rpb.astype(F32), ((0, 0), (B_STRIP_LO, rows - B_STRIP_LO - rpb.shape[1]), (0, 0)))
    first, second = table[:, :-1], table[:, 1:]
    gap = jnp.zeros(first.shape[:2] + (GRID_W - 2 * WIN_W + 1,), F32)
    strips = jnp.concatenate(
        [first[..., WIN_W - 1:], gap, second, gap, first[..., :WIN_W - 1]], axis=-1)
    assert strips.shape == (B_HEADS, B_NSTRIP, B_LANES)
    return strips


def _attn_b(qk3, vals3, strips, batch):
    rows = qk3.shape[0]

    def kv_map(col, t):
        return lambda b, r: (b * B_NRB + jnp.clip(r - 1, 0, B_KS_MAX) + t, 0, col)

    blk = lambda m: pl.BlockSpec((B_QR, GRID_W, B_DIM), m)
    return pl.pallas_call(
        _attn_b_kernel,
        out_shape=jax.ShapeDtypeStruct((rows, GRID_W, B_DIM), BF16),
        grid=(batch, B_NRB),
        in_specs=[
            blk(lambda b, r: (b * B_NRB + r, 0, 0)),
            blk(kv_map(1, 0)), blk(kv_map(1, 1)), blk(kv_map(1, 2)),
            blk(kv_map(0, 0)), blk(kv_map(0, 1)), blk(kv_map(0, 2)),
            _resident((B_HEADS, B_NSTRIP, B_LANES)),
        ],
        out_specs=blk(lambda b, r: (b * B_NRB + r, 0, 0)),
        scratch_shapes=[pltpu.VMEM((len(B_PATTERN_ROWBLOCKS), B_HEADS, B_NQ, B_NK), F32)],
        compiler_params=_params("arbitrary", "arbitrary"),
        name="attn_b",
    )(qk3, qk3, qk3, qk3, vals3, vals3, vals3, strips)


MG_TM = 512


def _merge_kernel(x_ref, oa_ref, ob_ref, ga_ref, gb_ref, wa_ref, wb_ref, wo_ref, g_ref,
                  x1_ref, h2_ref):
    a = jnp.dot(oa_ref[...], wa_ref[...], preferred_element_type=F32)
    b = jnp.dot(ob_ref[...], wb_ref[...], preferred_element_type=F32)
    merged = ga_ref[...].astype(F32) * a + gb_ref[...].astype(F32) * b
    y = jnp.dot(merged.astype(BF16), wo_ref[...], preferred_element_type=F32)
    x1 = x_ref[...] + y
    x1_ref[...] = x1
    h2_ref[...] = _rmsnorm(x1, g_ref[...]).astype(BF16)


def _merge(x2, out_a, out_b, gates, wa, wb, wo, norm_g):
    n = x2.shape[0]
    row_block = lambda width, col=0: pl.BlockSpec((MG_TM, width), lambda i: (i, col))
    return pl.pallas_call(
        _merge_kernel,
        out_shape=(jax.ShapeDtypeStruct((n, D_MODEL), F32),
                   jax.ShapeDtypeStruct((n, D_MODEL), BF16)),
        grid=(n // MG_TM,),
        in_specs=[
            row_block(D_MODEL), row_block(A_Q_DIM), row_block(B_DIM),
            row_block(D_MODEL, 0), row_block(D_MODEL, 1),
            _resident((A_Q_DIM, D_MODEL)), _resident((B_DIM, D_MODEL)),
            _resident((D_MODEL, D_MODEL)), _resident((1, D_MODEL)),
        ],
        out_specs=(row_block(D_MODEL), row_block(D_MODEL)),
        compiler_params=_params("arbitrary"),
        name="merge",
    )(x2, out_a, out_b, gates, gates, wa, wb, wo, norm_g)


UP_TM = 1024
UP_NJ = 4
UP_TN = D_FF // UP_NJ
UP_LANES = SUB // 2
UP_NSUB = UP_TN // UP_LANES
HALO = 16
UP_ROWS = UP_TM + 2 * HALO


def _ffn_up_kernel(hp_ref, hm_ref, hn_ref, wg_ref, wv_ref, cwg_ref, cwv_ref, cbg_ref, cbv_ref,
                   o_ref, lhs_ref, u_ref):
    i = pl.program_id(0)
    tiles_per_seq = SEQ // UP_TM

    @pl.when(pl.program_id(1) == 0)
    def _():
        first = (i % tiles_per_seq) == 0
        last = (i % tiles_per_seq) == tiles_per_seq - 1
        lhs_ref[0:HALO, :] = jnp.where(first, jnp.zeros_like(hp_ref[...]), hp_ref[...])
        lhs_ref[HALO:HALO + UP_TM, :] = hm_ref[...]
        lhs_ref[HALO + UP_TM:, :] = jnp.where(last, jnp.zeros_like(hn_ref[...]), hn_ref[...])

    def conv(u, lanes, cw_ref, cb_ref, cols):
        below = u[HALO - 1:HALO - 1 + UP_TM, lanes]
        mid = u[HALO:HALO + UP_TM, lanes]
        above = u[HALO + 1:HALO + 1 + UP_TM, lanes]
        y = mid * cw_ref[1:2, cols] + cb_ref[:, cols]
        y = y + below * cw_ref[0:1, cols]
        return y + above * cw_ref[2:3, cols]

    lhs = lhs_ref[...]
    for k in range(UP_NSUB):
        cols = slice(k * UP_LANES, (k + 1) * UP_LANES)
        u = u_ref.at[k % 2]
        w = jnp.concatenate([wg_ref[:, cols], wv_ref[:, cols]], axis=1)
        u[...] = jnp.dot(lhs, w, preferred_element_type=F32)
        half_gate = 0.5 * conv(u, slice(0, UP_LANES), cwg_ref, cbg_ref, cols)
        val = conv(u, slice(UP_LANES, SUB), cwv_ref, cbv_ref, cols)
        silu = half_gate * jnp.tanh(half_gate) + half_gate
        o_ref[:, cols] = (silu * val).astype(BF16)


def _ffn_up(h2, w_up, conv_w, conv_b):
    n = h2.shape[0]
    halo_blocks = UP_TM // HALO
    last_halo = n // HALO - 1
    return pl.pallas_call(
        _ffn_up_kernel,
        out_shape=jax.ShapeDtypeStruct((n, D_FF), BF16),
        grid=(n // UP_TM, UP_NJ),
        in_specs=[
            pl.BlockSpec((HALO, D_MODEL), lambda i, j: (jnp.maximum(i * halo_blocks - 1, 0), 0)),
            pl.BlockSpec((UP_TM, D_MODEL), lambda i, j: (i, 0)),
            pl.BlockSpec((HALO, D_MODEL),
                         lambda i, j: (jnp.minimum((i + 1) * halo_blocks, last_halo), 0)),
            pl.BlockSpec((D_MODEL, UP_TN), lambda i, j: (0, j)),
            pl.BlockSpec((D_MODEL, UP_TN), lambda i, j: (0, UP_NJ + j)),
            pl.BlockSpec((3, UP_TN), lambda i, j: (0, j)),
            pl.BlockSpec((3, UP_TN), lambda i, j: (0, UP_NJ + j)),
            pl.BlockSpec((1, UP_TN), lambda i, j: (0, j)),
            pl.BlockSpec((1, UP_TN), lambda i, j: (0, UP_NJ + j)),
        ],
        out_specs=pl.BlockSpec((UP_TM, UP_TN), lambda i, j: (i, j)),
        scratch_shapes=[pltpu.VMEM((UP_ROWS, D_MODEL), BF16),
                        pltpu.VMEM((2, UP_ROWS, SUB), F32)],
        compiler_params=_params("arbitrary", "arbitrary"),
        name="ffn_up",
    )(h2, h2, h2, w_up, w_up, conv_w, conv_w, conv_b, conv_b)


DN_TM = 1024
DN_TN = 512


def _ffn_down_kernel(a_ref, w_ref, x_ref, o_ref):
    o_ref[...] = x_ref[...] + jnp.dot(a_ref[...], w_ref[...], preferred_element_type=F32)


def _ffn_down(act, w_down, x1):
    n = act.shape[0]
    return pl.pallas_call(
        _ffn_down_kernel,
        out_shape=jax.ShapeDtypeStruct((n, D_MODEL), F32),
        grid=(n // DN_TM, D_MODEL // DN_TN),
        in_specs=[
            pl.BlockSpec((DN_TM, D_FF), lambda i, j: (i, 0)),
            pl.BlockSpec((D_FF, DN_TN), lambda i, j: (0, j)),
            pl.BlockSpec((DN_TM, DN_TN), lambda i, j: (i, j)),
        ],
        out_specs=pl.BlockSpec((DN_TM, DN_TN), lambda i, j: (i, j)),
        compiler_params=_params("arbitrary", "arbitrary"),
        name="ffn_down",
    )(act, w_down, x1)


def _rope_tables():
    half = HEAD_DIM // 2
    pos = jnp.arange(SEQ, dtype=F32)
    inv_freq = ROPE_THETA ** (-jnp.arange(half, dtype=F32) * (2.0 / HEAD_DIM))
    ang = pos[:, None] * inv_freq[None, :]
    cos, sin = jnp.cos(ang), jnp.sin(ang)
    return jnp.concatenate([cos, cos], axis=-1), jnp.concatenate([-sin, sin], axis=-1)


IN_COL_VA = ROPE_DIM
IN_COL_QB = IN_COL_VA + A_KV_DIM
IN_COL_VB = IN_COL_QB + NORM_DIM
IN_COL_GATE = IN_COL_VB + B_DIM
IN_HEAD_RANGES = [(0, IN_COL_VA), (IN_COL_QB, IN_COL_VB), (IN_COL_VB, IN_COL_GATE),
                  (IN_COL_VA, IN_COL_QB)]


def kernel(x, norm_mix, w_in, a_q_norm, a_k_norm, a_sink, b_q_norm, b_k_norm, b_rpb,
           w_branch_a, w_branch_b, w_out, norm_ffn, w_up, conv_w, conv_b, w_down):
    batch, seq, d_model = x.shape
    assert (seq, d_model) == (SEQ, D_MODEL)
    n = batch * seq
    cos, sin_signed = _rope_tables()
    scale = 1.0 / math.sqrt(HEAD_DIM)
    seq_tiles = SEQ // IN_TM
    pos_spec = pl.BlockSpec((IN_TM, HEAD_DIM), lambda i: (i % seq_tiles, 0))
    gain_spec = _resident((1, HEAD_DIM))
    x2 = x.reshape(n, d_model)
    for l in range(norm_mix.shape[0]):
        w_gate = w_in[l][:, IN_COL_GATE:].astype(BF16)
        gates, h, w_rope, w_norm, w_vb, w_va, w_up_bf = _in_gate(
            x2, norm_mix[l][None], w_gate, [(w_in[l], IN_HEAD_RANGES), (w_up[l], None)])
        qk_a, wa_bf, wb_bf = _in_heads(
            _in_rope_kernel, "in_rope", h, (w_rope,),
            (cos, sin_signed, a_q_norm[l][None] * scale, a_k_norm[l][None]),
            [pos_spec, pos_spec, gain_spec, gain_spec], [w_branch_a[l], w_branch_b[l]])
        qk_b, w_down_bf = _in_heads(
            _in_norm_kernel, "in_norm", h, (w_norm,),
            (b_q_norm[l][None] * scale, b_k_norm[l][None]), [gain_spec, gain_spec], [w_down[l]])
        vals, w_out_bf = _in_heads(_in_plain_kernel, "in_plain", h, (w_vb, w_va), (), [], [w_out[l]])
        out_a = _attn_a(qk_a, vals, a_sink[l], batch)
        out_b = _attn_b(qk_b.reshape(n // GRID_W, GRID_W, NORM_DIM),
                        vals.reshape(n // GRID_W, GRID_W, PLAIN_DIM), _attn_b_strips(b_rpb[l]), batch)
        x1, h2 = _merge(x2, out_a, out_b.reshape(n, B_DIM), gates, wa_bf, wb_bf, w_out_bf,
                        norm_ffn[l][None])
        act = _ffn_up(h2, w_up_bf, conv_w[l], conv_b[l][None])
        x2 = _ffn_down(act, w_down_bf, x1)
    return x2.reshape(batch, seq, d_model)
```

```python
import math

import numpy as np
import jax
import jax.numpy as jnp
from jax import lax
from jax.experimental import pallas as pl
from jax.experimental.pallas import tpu as pltpu

D_MODEL = 2048
SEQ = 4096
HEAD_DIM = 128
A_Q_HEADS = 8
A_KV_HEADS = 2
A_GROUP = A_Q_HEADS // A_KV_HEADS
WINDOW = 128
B_HEADS = 8
GRID_W = 64
WIN_H = 8
WIN_W = 16
D_FF = 5632
ROPE_THETA = 10000.0
EPS = 1e-6
NEG = -1e30

A_Q_DIM = A_Q_HEADS * HEAD_DIM
A_KV_DIM = A_KV_HEADS * HEAD_DIM
B_DIM = B_HEADS * HEAD_DIM

VMEM_LIMIT_BYTES = 56 * 1024 * 1024
MXU_COLS = 256
SUB = MXU_COLS
HEADS_PER_SUB = SUB // HEAD_DIM

BF16 = jnp.bfloat16
F32 = jnp.float32


def _params(*semantics):
    return pltpu.CompilerParams(dimension_semantics=semantics, vmem_limit_bytes=VMEM_LIMIT_BYTES)


def _resident(shape):
    return pl.BlockSpec(shape, lambda *_: (0,) * len(shape), pipeline_mode=pl.Buffered(1))


def _rmsnorm(x, gain):
    ms = jnp.mean(x * x, axis=-1, keepdims=True)
    return x * lax.rsqrt(ms + EPS) * gain


class _CastJob:
    def __init__(self, src, steps, outputs=None):
        rows, cols = src.shape
        self.src = src
        self.outputs = outputs or [[(0, cols)]]
        self.slab = rows // steps
        assert self.slab * steps == rows
        self.span = max(hi for pieces in self.outputs for _, hi in pieces)
        self.widths = [sum(hi - lo for lo, hi in pieces) for pieces in self.outputs]

    def in_spec(self):
        return pl.BlockSpec((self.slab, self.span), lambda i: (i, 0))

    def out_specs(self):
        return [pl.BlockSpec((self.slab, width), lambda i: (i, 0)) for width in self.widths]

    def out_shapes(self):
        return [jax.ShapeDtypeStruct((self.src.shape[0], width), BF16) for width in self.widths]

    def run(self, src_ref, out_refs):
        for pieces, out_ref in zip(self.outputs, out_refs):
            parts = [src_ref[:, lo:hi] for lo, hi in pieces]
            value = parts[0] if len(parts) == 1 else jnp.concatenate(parts, axis=1)
            out_ref[...] = value.astype(BF16)


def _with_casts(body, jobs, n_in, n_out):
    n_cast_out = sum(len(j.outputs) for j in jobs)

    def wrapped(*refs):
        ins = refs[:n_in]
        srcs = refs[n_in:n_in + len(jobs)]
        outs = refs[n_in + len(jobs):n_in + len(jobs) + n_out]
        cast_outs = list(refs[n_in + len(jobs) + n_out:n_in + len(jobs) + n_out + n_cast_out])
        scratch = refs[n_in + len(jobs) + n_out + n_cast_out:]
        for job, src_ref in zip(jobs, srcs):
            job.run(src_ref, [cast_outs.pop(0) for _ in job.outputs])
        body(*ins, *outs, *scratch)

    return wrapped


def _sigmoid(x):
    return 0.5 * jnp.tanh(0.5 * x) + 0.5


IN_TM = 1024
GATE_TM = 512
GATE_DIM = 2 * D_MODEL
ROPE_DIM = A_Q_DIM + A_KV_DIM
NORM_DIM = 2 * B_DIM
PLAIN_DIM = B_DIM + A_KV_DIM


def _in_gate_kernel(x_ref, g_ref, w_ref, o_ref, h_ref):
    h_ref[...] = _rmsnorm(x_ref[...], g_ref[...]).astype(BF16)
    lhs = h_ref[...]
    for k in range(GATE_DIM // SUB):
        cols = slice(k * SUB, (k + 1) * SUB)
        acc = jnp.dot(lhs, w_ref[:, cols], preferred_element_type=F32)
        o_ref[:, cols] = _sigmoid(acc).astype(BF16)


def _in_gate(x2, norm_g, w_gate, casts):
    n = x2.shape[0]
    steps = n // GATE_TM
    jobs = [_CastJob(src, steps, ranges) for src, ranges in casts]
    row_block = lambda width: pl.BlockSpec((GATE_TM, width), lambda i: (i, 0))
    return pl.pallas_call(
        _with_casts(_in_gate_kernel, jobs, n_in=3, n_out=2),
        out_shape=[jax.ShapeDtypeStruct((n, GATE_DIM), BF16),
                   jax.ShapeDtypeStruct((n, D_MODEL), BF16)]
                  + [s for j in jobs for s in j.out_shapes()],
        grid=(steps,),
        in_specs=[row_block(D_MODEL), _resident((1, D_MODEL)), _resident((D_MODEL, GATE_DIM))]
                 + [j.in_spec() for j in jobs],
        out_specs=[row_block(GATE_DIM), row_block(D_MODEL)]
                  + [s for j in jobs for s in j.out_specs()],
        compiler_params=_params("arbitrary"),
        name="in_gate",
    )(x2, norm_g, w_gate, *[j.src for j in jobs])


def _store_heads(o_ref, k, acc, fn):
    for t in range(HEADS_PER_SUB):
        col = k * SUB + t * HEAD_DIM
        o_ref[:, col:col + HEAD_DIM] = fn(acc[:, t * HEAD_DIM:(t + 1) * HEAD_DIM]).astype(BF16)


def _pair_columns(head_a, head_b):
    half = HEAD_DIM // 2
    a, b = head_a * HEAD_DIM, head_b * HEAD_DIM
    return [(a, a + half), (b, b + half), (a + half, a + HEAD_DIM), (b + half, b + HEAD_DIM)]


ROPE_PAIRS = [(p, p + A_GROUP) for p in range(A_GROUP)] + [(A_Q_HEADS, A_Q_HEADS + 1)]
assert A_KV_HEADS == 2 and len(ROPE_PAIRS) * SUB == ROPE_DIM
ROPE_COLUMNS = [piece for pair in ROPE_PAIRS for piece in _pair_columns(*pair)]
HALF_LANES = HEAD_DIM // 2


def _in_rope_kernel(h_ref, w_ref, cos_ref, sin_ref, gq_ref, gk_ref, o_ref):
    lhs = h_ref[...]
    first_head = lax.broadcasted_iota(jnp.int32, (IN_TM, HEAD_DIM), 1) < HALF_LANES
    for k in range(ROPE_DIM // SUB):
        acc = jnp.dot(lhs, w_ref[:, k * SUB:(k + 1) * SUB], preferred_element_type=F32)
        gain_ref = gq_ref if k * SUB < A_Q_DIM else gk_ref
        lo, hi = acc[:, :HEAD_DIM], acc[:, HEAD_DIM:]
        sq = lo * lo + hi * hi
        ss_a = jnp.sum(jnp.where(first_head, sq, 0.0), axis=-1, keepdims=True)
        ss_b = jnp.sum(jnp.where(first_head, 0.0, sq), axis=-1, keepdims=True)
        inv = jnp.where(first_head, lax.rsqrt(ss_a * (1.0 / HEAD_DIM) + EPS),
                        lax.rsqrt(ss_b * (1.0 / HEAD_DIM) + EPS))
        lo = lo * inv * gain_ref[0:1, :]
        hi = hi * inv * gain_ref[1:2, :]
        cos, sin = cos_ref[...], sin_ref[...]
        o_ref[:, k * SUB:k * SUB + HEAD_DIM] = (lo * cos - hi * sin).astype(BF16)
        o_ref[:, k * SUB + HEAD_DIM:(k + 1) * SUB] = (hi * cos + lo * sin).astype(BF16)


def _in_norm_kernel(h_ref, w_ref, gq_ref, gk_ref, o_ref):
    lhs = h_ref[...]
    for k in range(NORM_DIM // SUB):
        acc = jnp.dot(lhs, w_ref[:, k * SUB:(k + 1) * SUB], preferred_element_type=F32)
        gain_ref = gq_ref if k * SUB < B_DIM else gk_ref
        _store_heads(o_ref, k, acc, lambda a: _rmsnorm(a, gain_ref[...]))


def _in_plain_kernel(h_ref, wvb_ref, wva_ref, o_ref):
    lhs = h_ref[...]
    for k in range(PLAIN_DIM // SUB):
        cols = slice(k * SUB, (k + 1) * SUB)
        w = wvb_ref[:, cols] if k * SUB < B_DIM else wva_ref[:, k * SUB - B_DIM:(k + 1) * SUB - B_DIM]
        o_ref[:, cols] = jnp.dot(lhs, w, preferred_element_type=F32).astype(BF16)


def _in_heads(body, name, h, weights, extra, extra_specs, casts):
    n = h.shape[0]
    steps = n // IN_TM
    width = sum(w.shape[1] for w in weights)
    jobs = [_CastJob(src, steps) for src in casts]
    return pl.pallas_call(
        _with_casts(body, jobs, n_in=1 + len(weights) + len(extra), n_out=1),
        out_shape=[jax.ShapeDtypeStruct((n, width), BF16)] + [s for j in jobs for s in j.out_shapes()],
        grid=(steps,),
        in_specs=([pl.BlockSpec((IN_TM, D_MODEL), lambda i: (i, 0))]
                  + [_resident(w.shape) for w in weights] + extra_specs
                  + [j.in_spec() for j in jobs]),
        out_specs=([pl.BlockSpec((IN_TM, width), lambda i: (i, 0))]
                   + [s for j in jobs for s in j.out_specs()]),
        compiler_params=_params("arbitrary"),
        name=name,
    )(h, *weights, *extra, *[j.src for j in jobs])


A_TQ = 512
A_BLK = WINDOW
A_NB = A_TQ // A_BLK
A_CHUNKS = SEQ // A_TQ


def _attn_a_kernel(sink_ref, q_ref, kp_ref, km_ref, kn_ref, vp_ref, vm_ref, vn_ref, o_ref):
    c = pl.program_id(1)
    k_all = jnp.concatenate([kp_ref[...], km_ref[...], kn_ref[...]], axis=0)
    v_all = jnp.concatenate([vp_ref[...], vm_ref[...], vn_ref[...]], axis=0)
    qq = lax.broadcasted_iota(jnp.int32, (A_BLK, 3 * A_BLK), 0)
    kk = lax.broadcasted_iota(jnp.int32, (A_BLK, 3 * A_BLK), 1)
    d = kk - qq
    band = (d >= 0) & (d <= 2 * WINDOW)
    ones = jnp.ones((3 * A_BLK, HEAD_DIM), BF16)
    lane = lax.broadcasted_iota(jnp.int32, (A_BLK, SUB), 1) % HEAD_DIM
    own_lanes = (lane < HALF_LANES, lane >= HALF_LANES)
    for n in range(A_NB):
        valid = band
        if n == 0:
            valid = valid & ((kk >= A_BLK) | (c > 0))
        if n == A_NB - 1:
            valid = valid & ((kk < 2 * A_BLK) | (c < A_CHUNKS - 1))
        k = k_all[n * A_BLK:(n + 3) * A_BLK, :]
        q_pairs = [q_ref[n * A_BLK:(n + 1) * A_BLK, p * SUB:(p + 1) * SUB] for p in range(A_GROUP)]
        qs = jnp.concatenate([jnp.where(own_lanes[h], q, jnp.zeros_like(q))
                              for h in range(A_KV_HEADS) for q in q_pairs], axis=0)
        s_all = lax.dot_general(qs, k, (((1,), (1,)), ((), ())),
                                preferred_element_type=F32)
        for h in range(A_KV_HEADS):
            v = v_all[n * A_BLK:(n + 3) * A_BLK, h * HEAD_DIM:(h + 1) * HEAD_DIM]
            heads = [h * A_GROUP + g for g in range(A_GROUP)]
            s = s_all[h * A_GROUP * A_BLK:(h + 1) * A_GROUP * A_BLK]
            ps, sink_terms = [], []
            for g, t in enumerate(heads):
                sg = jnp.where(valid, s[g * A_BLK:(g + 1) * A_BLK], NEG)
                m = jnp.maximum(jnp.max(sg, axis=-1, keepdims=True), sink_ref[t])
                ps.append(jnp.exp(sg - m).astype(BF16))
                sink_terms.append(jnp.exp(sink_ref[t] - m))
            o = jnp.dot(jnp.concatenate(ps, axis=0), jnp.concatenate([v, ones], axis=1),
                        preferred_element_type=F32)
            for g, t in enumerate(heads):
                og = o[g * A_BLK:(g + 1) * A_BLK]
                og = og[:, :HEAD_DIM] / (og[:, HEAD_DIM:] + sink_terms[g])
                o_ref[n * A_BLK:(n + 1) * A_BLK, t * HEAD_DIM:(t + 1) * HEAD_DIM] = og.astype(BF16)


def _attn_a(qk, vals, sink, batch):
    n = qk.shape[0]
    blk_per_seq = SEQ // A_BLK
    k_col = A_Q_DIM // A_KV_DIM
    v_col = B_DIM // A_KV_DIM

    def prev_map(col):
        return lambda b, c, *_: (b * blk_per_seq + jnp.maximum(c * A_NB - 1, 0), col)

    def main_map(col):
        return lambda b, c, *_: (b * A_CHUNKS + c, col)

    def next_map(col):
        return lambda b, c, *_: (b * blk_per_seq + jnp.minimum(c * A_NB + A_NB, blk_per_seq - 1), col)

    halo = lambda m: pl.BlockSpec((A_BLK, A_KV_DIM), m)
    main = lambda m: pl.BlockSpec((A_TQ, A_KV_DIM), m)
    return pl.pallas_call(
        _attn_a_kernel,
        out_shape=jax.ShapeDtypeStruct((n, A_Q_DIM), BF16),
        grid_spec=pltpu.PrefetchScalarGridSpec(
            num_scalar_prefetch=1,
            grid=(batch, A_CHUNKS),
            in_specs=[
                pl.BlockSpec((A_TQ, A_Q_DIM), main_map(0)),
                halo(prev_map(k_col)), main(main_map(k_col)), halo(next_map(k_col)),
                halo(prev_map(v_col)), main(main_map(v_col)), halo(next_map(v_col)),
            ],
            out_specs=pl.BlockSpec((A_TQ, A_Q_DIM), lambda b, c, *_: (b * A_CHUNKS + c, 0)),
        ),
        compiler_params=_params("arbitrary", "arbitrary"),
        name="attn_a",
    )(sink, qk, qk, qk, qk, vals, vals, vals)


GRID_ROWS = SEQ // GRID_W
B_QR = 4
B_KR = B_QR + WIN_H
B_NRB = GRID_ROWS // B_QR
B_NQ = B_QR * GRID_W
B_NK = B_KR * GRID_W
B_KBLOCKS = B_KR // B_QR
B_KS_MAX = B_NRB - B_KBLOCKS
B_PATTERN_ROWBLOCKS = (0, 1, B_NRB - 1)


def _attn_b_row_windows():
    starts, row_ok = [], []
    for rb in B_PATTERN_ROWBLOCKS:
        ks = int(np.clip(rb - 1, 0, B_KS_MAX))
        qr = rb * B_QR + np.arange(B_QR)
        kr = ks * B_QR + np.arange(B_KR)
        rs = np.clip(qr - WIN_H // 2, 0, GRID_ROWS - WIN_H)
        row_ok.append((kr[None, :] >= rs[:, None]) & (kr[None, :] < rs[:, None] + WIN_H))
        starts.append(kr[0] - qr + WIN_H - 1)
    return np.stack(starts), np.stack(row_ok)


B_ROW_START, B_ROW_OK = _attn_b_row_windows()
B_STRIP_LO = max(0, -int(B_ROW_START.min()))
B_NSTRIP = int(B_ROW_START.max()) + B_KR - 1 + B_STRIP_LO
B_LANES = 2 * GRID_W


def _attn_b_build_bias(strip_ref, bias_ref):
    qc = lax.broadcasted_iota(jnp.int32, (GRID_W, B_LANES), 0)
    lane = lax.broadcasted_iota(jnp.int32, (GRID_W, B_LANES), 1)
    kc = lane % GRID_W
    cs = jnp.clip(qc - WIN_W // 2, 0, GRID_W - WIN_W)
    col_ok = (kc >= cs) & (kc < cs + WIN_W)
    first_row = lane < GRID_W
    for h in range(B_HEADS):
        tiles = {}

        def tile(i):
            if i not in tiles:
                rows = jnp.broadcast_to(strip_ref[h, i:i + 1, :], (GRID_W, B_LANES))
                toeplitz = pltpu.roll(rows, 0, axis=1, stride=1, stride_axis=0)
                tiles[i] = jnp.where(col_ok, toeplitz, NEG)
            return tiles[i]

        for p in range(len(B_PATTERN_ROWBLOCKS)):
            for qr in range(B_QR):
                for c in range(B_KR // 2):
                    ok0, ok1 = B_ROW_OK[p, qr, 2 * c], B_ROW_OK[p, qr, 2 * c + 1]
                    if ok0 or ok1:
                        t = tile(int(B_ROW_START[p, qr]) + 2 * c + B_STRIP_LO)
                        if not ok1:
                            t = jnp.where(first_row, t, NEG)
                        if not ok0:
                            t = jnp.where(first_row, NEG, t)
                    else:
                        t = jnp.full((GRID_W, B_LANES), NEG, F32)
                    bias_ref[p, h, qr * GRID_W:(qr + 1) * GRID_W, c * B_LANES:(c + 1) * B_LANES] = t


def _attn_b_kernel(q_ref, k0_ref, k1_ref, k2_ref, v0_ref, v1_ref, v2_ref, strip_ref, o_ref,
                   bias_ref):
    r = pl.program_id(1)

    @pl.when((pl.program_id(0) == 0) & (r == 0))
    def _():
        _attn_b_build_bias(strip_ref, bias_ref)

    pattern = jnp.where(r == 0, 0, jnp.where(r == B_NRB - 1, 2, 1))
    k_refs = (k0_ref, k1_ref, k2_ref)
    v_refs = (v0_ref, v1_ref, v2_ref)
    ones = jnp.ones((B_NK, HEAD_DIM), BF16)
    for h in range(B_HEADS):
        hd = slice(h * HEAD_DIM, (h + 1) * HEAD_DIM)
        q = q_ref[:, :, hd].reshape(B_NQ, HEAD_DIM)
        k = jnp.concatenate([ref[:, :, hd] for ref in k_refs], axis=0).reshape(B_NK, HEAD_DIM)
        v = jnp.concatenate([ref[:, :, hd] for ref in v_refs], axis=0).reshape(B_NK, HEAD_DIM)
        s = lax.dot_general(q, k, (((1,), (1,)), ((), ())),
                            preferred_element_type=F32) + bias_ref[pattern, h]
        m = jnp.max(s, axis=-1, keepdims=True)
        p = jnp.exp(s - m).astype(BF16)
        o = jnp.dot(p, jnp.concatenate([v, ones], axis=1), preferred_element_type=F32)
        o = o[:, :HEAD_DIM] / o[:, HEAD_DIM:]
        o_ref[:, :, hd] = o.reshape(B_QR, GRID_W, HEAD_DIM).astype(BF16)


def _attn_b_strips(rpb):
    rows = B_NSTRIP + 1
    table = jnp.pad(rpb.astype(F32), ((0, 0), (B_STRIP_LO, rows - B_STRIP_LO - rpb.shape[1]), (0, 0)))
    first, second = table[:, :-1], table[:, 1:]
    gap = jnp.zeros(first.shape[:2] + (GRID_W - 2 * WIN_W + 1,), F32)
    strips = jnp.concatenate(
        [first[..., WIN_W - 1:], gap, second, gap, first[..., :WIN_W - 1]], axis=-1)
    assert strips.shape == (B_HEADS, B_NSTRIP, B_LANES)
    return strips


def _attn_b(qk3, vals3, strips, batch):
    rows = qk3.shape[0]

    def kv_map(col, t):
        return lambda b, r: (b * B_NRB + jnp.clip(r - 1, 0, B_KS_MAX) + t, 0, col)

    blk = lambda m: pl.BlockSpec((B_QR, GRID_W, B_DIM), m)
    return pl.pallas_call(
        _attn_b_kernel,
        out_shape=jax.ShapeDtypeStruct((rows, GRID_W, B_DIM), BF16),
        grid=(batch, B_NRB),
        in_specs=[
            blk(lambda b, r: (b * B_NRB + r, 0, 0)),
            blk(kv_map(1, 0)), blk(kv_map(1, 1)), blk(kv_map(1, 2)),
            blk(kv_map(0, 0)), blk(kv_map(0, 1)), blk(kv_map(0, 2)),
            _resident((B_HEADS, B_NSTRIP, B_LANES)),
        ],
        out_specs=blk(lambda b, r: (b * B_NRB + r, 0, 0)),
        scratch_shapes=[pltpu.VMEM((len(B_PATTERN_ROWBLOCKS), B_HEADS, B_NQ, B_NK), F32)],
        compiler_params=_params("arbitrary", "arbitrary"),
        name="attn_b",
    )(qk3, qk3, qk3, qk3, vals3, vals3, vals3, strips)


MG_TM = 512


def _merge_kernel(x_ref, oa_ref, ob_ref, ga_ref, gb_ref, wa_ref, wb_ref, wo_ref, g_ref,
                  x1_ref, h2_ref):
    a = jnp.dot(oa_ref[...], wa_ref[...], preferred_element_type=F32)
    b = jnp.dot(ob_ref[...], wb_ref[...], preferred_element_type=F32)
    merged = ga_ref[...].astype(F32) * a + gb_ref[...].astype(F32) * b
    y = jnp.dot(merged.astype(BF16), wo_ref[...], preferred_element_type=F32)
    x1 = x_ref[...] + y
    x1_ref[...] = x1
    h2_ref[...] = _rmsnorm(x1, g_ref[...]).astype(BF16)


def _merge(x2, out_a, out_b, gates, wa, wb, wo, norm_g):
    n = x2.shape[0]
    row_block = lambda width, col=0: pl.BlockSpec((MG_TM, width), lambda i: (i, col))
    return pl.pallas_call(
        _merge_kernel,
        out_shape=(jax.ShapeDtypeStruct((n, D_MODEL), F32),
                   jax.ShapeDtypeStruct((n, D_MODEL), BF16)),
        grid=(n // MG_TM,),
        in_specs=[
            row_block(D_MODEL), row_block(A_Q_DIM), row_block(B_DIM),
            row_block(D_MODEL, 0), row_block(D_MODEL, 1),
            _resident((A_Q_DIM, D_MODEL)), _resident((B_DIM, D_MODEL)),
            _resident((D_MODEL, D_MODEL)), _resident((1, D_MODEL)),
        ],
        out_specs=(row_block(D_MODEL), row_block(D_MODEL)),
        compiler_params=_params("arbitrary"),
        name="merge",
    )(x2, out_a, out_b, gates, gates, wa, wb, wo, norm_g)


UP_TM = 1024
UP_NJ = 4
UP_TN = D_FF // UP_NJ
UP_LANES = SUB // 2
UP_NSUB = UP_TN // UP_LANES
HALO = 16
UP_ROWS = UP_TM + 2 * HALO


def _ffn_up_kernel(hp_ref, hm_ref, hn_ref, wg_ref, wv_ref, cwg_ref, cwv_ref, cbg_ref, cbv_ref,
                   o_ref, lhs_ref, u_ref):
    i = pl.program_id(0)
    tiles_per_seq = SEQ // UP_TM

    @pl.when(pl.program_id(1) == 0)
    def _():
        first = (i % tiles_per_seq) == 0
        last = (i % tiles_per_seq) == tiles_per_seq - 1
        lhs_ref[0:HALO, :] = jnp.where(first, jnp.zeros_like(hp_ref[...]), hp_ref[...])
        lhs_ref[HALO:HALO + UP_TM, :] = hm_ref[...]
        lhs_ref[HALO + UP_TM:, :] = jnp.where(last, jnp.zeros_like(hn_ref[...]), hn_ref[...])

    def conv(u, lanes, cw_ref, cb_ref, cols):
        below = u[HALO - 1:HALO - 1 + UP_TM, lanes]
        mid = u[HALO:HALO + UP_TM, lanes]
        above = u[HALO + 1:HALO + 1 + UP_TM, lanes]
        y = mid * cw_ref[1:2, cols] + cb_ref[:, cols]
        y = y + below * cw_ref[0:1, cols]
        return y + above * cw_ref[2:3, cols]

    lhs = lhs_ref[...]
    for k in range(UP_NSUB):
        cols = slice(k * UP_LANES, (k + 1) * UP_LANES)
        u = u_ref.at[k % 2]
        w = jnp.concatenate([wg_ref[:, cols], wv_ref[:, cols]], axis=1)
        u[...] = jnp.dot(lhs, w, preferred_element_type=F32)
        half_gate = 0.5 * conv(u, slice(0, UP_LANES), cwg_ref, cbg_ref, cols)
        val = conv(u, slice(UP_LANES, SUB), cwv_ref, cbv_ref, cols)
        silu = half_gate * jnp.tanh(half_gate) + half_gate
        o_ref[:, cols] = (silu * val).astype(BF16)


def _ffn_up(h2, w_up, conv_w, conv_b):
    n = h2.shape[0]
    halo_blocks = UP_TM // HALO
    last_halo = n // HALO - 1
    return pl.pallas_call(
        _ffn_up_kernel,
        out_shape=jax.ShapeDtypeStruct((n, D_FF), BF16),
        grid=(n // UP_TM, UP_NJ),
        in_specs=[
            pl.BlockSpec((HALO, D_MODEL), lambda i, j: (jnp.maximum(i * halo_blocks - 1, 0), 0)),
            pl.BlockSpec((UP_TM, D_MODEL), lambda i, j: (i, 0)),
            pl.BlockSpec((HALO, D_MODEL),
                         lambda i, j: (jnp.minimum((i + 1) * halo_blocks, last_halo), 0)),
            pl.BlockSpec((D_MODEL, UP_TN), lambda i, j: (0, j)),
            pl.BlockSpec((D_MODEL, UP_TN), lambda i, j: (0, UP_NJ + j)),
            pl.BlockSpec((3, UP_TN), lambda i, j: (0, j)),
            pl.BlockSpec((3, UP_TN), lambda i, j: (0, UP_NJ + j)),
            pl.BlockSpec((1, UP_TN), lambda i, j: (0, j)),
            pl.BlockSpec((1, UP_TN), lambda i, j: (0, UP_NJ + j)),
        ],
        out_specs=pl.BlockSpec((UP_TM, UP_TN), lambda i, j: (i, j)),
        scratch_shapes=[pltpu.VMEM((UP_ROWS, D_MODEL), BF16),
                        pltpu.VMEM((2, UP_ROWS, SUB), F32)],
        compiler_params=_params("arbitrary", "arbitrary"),
        name="ffn_up",
    )(h2, h2, h2, w_up, w_up, conv_w, conv_w, conv_b, conv_b)


DN_TM = 1024
DN_TN = 512


def _ffn_down_kernel(a_ref, w_ref, x_ref, o_ref):
    o_ref[...] = x_ref[...] + jnp.dot(a_ref[...], w_ref[...], preferred_element_type=F32)


def _ffn_down(act, w_down, x1):
    n = act.shape[0]
    return pl.pallas_call(
        _ffn_down_kernel,
        out_shape=jax.ShapeDtypeStruct((n, D_MODEL), F32),
        grid=(n // DN_TM, D_MODEL // DN_TN),
        in_specs=[
            pl.BlockSpec((DN_TM, D_FF), lambda i, j: (i, 0)),
            pl.BlockSpec((D_FF, DN_TN), lambda i, j: (0, j)),
            pl.BlockSpec((DN_TM, DN_TN), lambda i, j: (i, j)),
        ],
        out_specs=pl.BlockSpec((DN_TM, DN_TN), lambda i, j: (i, j)),
        compiler_params=_params("arbitrary", "arbitrary"),
        name="ffn_down",
    )(act, w_down, x1)


def _rope_tables():
    half = HEAD_DIM // 2
    pos = jnp.arange(SEQ, dtype=F32)
    inv_freq = ROPE_THETA ** (-jnp.arange(half, dtype=F32) * (2.0 / HEAD_DIM))
    ang = pos[:, None] * inv_freq[None, :]
    cos, sin = jnp.cos(ang), jnp.sin(ang)
    return jnp.concatenate([cos, cos], axis=-1), jnp.concatenate([sin, sin], axis=-1)


def _paired_gain(gain):
    lo, hi = gain[:HALF_LANES], gain[HALF_LANES:]
    return jnp.stack([jnp.concatenate([lo, lo]), jnp.concatenate([hi, hi])])


IN_COL_VA = ROPE_DIM
IN_COL_QB = IN_COL_VA + A_KV_DIM
IN_COL_VB = IN_COL_QB + NORM_DIM
IN_COL_GATE = IN_COL_VB + B_DIM
IN_HEAD_OUTPUTS = [ROPE_COLUMNS, [(IN_COL_QB, IN_COL_VB)], [(IN_COL_VB, IN_COL_GATE)],
                   [(IN_COL_VA, IN_COL_QB)]]


def kernel(x, norm_mix, w_in, a_q_norm, a_k_norm, a_sink, b_q_norm, b_k_norm, b_rpb,
           w_branch_a, w_branch_b, w_out, norm_ffn, w_up, conv_w, conv_b, w_down):
    batch, seq, d_model = x.shape
    assert (seq, d_model) == (SEQ, D_MODEL)
    n = batch * seq
    cos, sin = _rope_tables()
    scale = 1.0 / math.sqrt(HEAD_DIM)
    seq_tiles = SEQ // IN_TM
    pos_spec = pl.BlockSpec((IN_TM, HEAD_DIM), lambda i: (i % seq_tiles, 0))
    gain_spec = _resident((1, HEAD_DIM))
    pair_gain_spec = _resident((2, HEAD_DIM))
    x2 = x.reshape(n, d_model)
    for l in range(norm_mix.shape[0]):
        w_gate = w_in[l][:, IN_COL_GATE:].astype(BF16)
        gates, h, w_rope, w_norm, w_vb, w_va, w_up_bf = _in_gate(
            x2, norm_mix[l][None], w_gate, [(w_in[l], IN_HEAD_OUTPUTS), (w_up[l], None)])
        qk_a, wa_bf, wb_bf = _in_heads(
            _in_rope_kernel, "in_rope", h, (w_rope,),
            (cos, sin, _paired_gain(a_q_norm[l] * scale), _paired_gain(a_k_norm[l])),
            [pos_spec, pos_spec, pair_gain_spec, pair_gain_spec], [w_branch_a[l], w_branch_b[l]])
        qk_b, w_down_bf = _in_heads(
            _in_norm_kernel, "in_norm", h, (w_norm,),
            (b_q_norm[l][None] * scale, b_k_norm[l][None]), [gain_spec, gain_spec], [w_down[l]])
        vals, w_out_bf = _in_heads(_in_plain_kernel, "in_plain", h, (w_vb, w_va), (), [], [w_out[l]])
        out_a = _attn_a(qk_a, vals, a_sink[l], batch)
        out_b = _attn_b(qk_b.reshape(n // GRID_W, GRID_W, NORM_DIM),
                        vals.reshape(n // GRID_W, GRID_W, PLAIN_DIM), _attn_b_strips(b_rpb[l]), batch)
        x1, h2 = _merge(x2, out_a, out_b.reshape(n, B_DIM), gates, wa_bf, wb_bf, w_out_bf,
                        norm_ffn[l][None])
        act = _ffn_up(h2, w_up_bf, conv_w[l], conv_b[l][None])
        x2 = _ffn_down(act, w_down_bf, x1)
    return x2.reshape(batch, seq, d_model)
```

```python
import math

import numpy as np
import jax
import jax.numpy as jnp
from jax import lax
from jax.experimental import pallas as pl
from jax.experimental.pallas import tpu as pltpu

D_MODEL = 2048
SEQ = 4096
HEAD_DIM = 128
A_Q_HEADS = 8
A_KV_HEADS = 2
A_GROUP = A_Q_HEADS // A_KV_HEADS
WINDOW = 128
B_HEADS = 8
GRID_W = 64
WIN_H = 8
WIN_W = 16
D_FF = 5632
ROPE_THETA = 10000.0
EPS = 1e-6
NEG = -1e30

A_Q_DIM = A_Q_HEADS * HEAD_DIM
A_KV_DIM = A_KV_HEADS * HEAD_DIM
B_DIM = B_HEADS * HEAD_DIM

VMEM_LIMIT_BYTES = 56 * 1024 * 1024
MXU_COLS = 256
SUB = MXU_COLS
HEADS_PER_SUB = SUB // HEAD_DIM

BF16 = jnp.bfloat16
F32 = jnp.float32


def _params(*semantics):
    return pltpu.CompilerParams(dimension_semantics=semantics, vmem_limit_bytes=VMEM_LIMIT_BYTES)


def _resident(shape):
    return pl.BlockSpec(shape, lambda *_: (0,) * len(shape), pipeline_mode=pl.Buffered(1))


def _rmsnorm(x, gain):
    ms = jnp.mean(x * x, axis=-1, keepdims=True)
    return x * lax.rsqrt(ms + EPS) * gain


class _CastJob:
    def __init__(self, src, steps, outputs=None, step_of=lambda i: i):
        rows, cols = src.shape
        self.src = src
        self.outputs = outputs or [[(0, cols)]]
        self.slab = rows // steps
        assert self.slab * steps == rows
        self.span = max(hi for pieces in self.outputs for _, hi in pieces)
        self.widths = [sum(hi - lo for lo, hi in pieces) for pieces in self.outputs]
        self.index_map = lambda *idx: (step_of(*idx), 0)

    def in_spec(self):
        return pl.BlockSpec((self.slab, self.span), self.index_map)

    def out_specs(self):
        return [pl.BlockSpec((self.slab, width), self.index_map) for width in self.widths]

    def out_shapes(self):
        return [jax.ShapeDtypeStruct((self.src.shape[0], width), BF16) for width in self.widths]

    def run(self, src_ref, out_refs):
        for pieces, out_ref in zip(self.outputs, out_refs):
            parts = [src_ref[:, lo:hi] for lo, hi in pieces]
            value = parts[0] if len(parts) == 1 else jnp.concatenate(parts, axis=1)
            out_ref[...] = value.astype(BF16)


def _with_casts(body, jobs, n_in, n_out):
    n_cast_out = sum(len(j.outputs) for j in jobs)

    def wrapped(*refs):
        ins = refs[:n_in]
        srcs = refs[n_in:n_in + len(jobs)]
        outs = refs[n_in + len(jobs):n_in + len(jobs) + n_out]
        cast_outs = list(refs[n_in + len(jobs) + n_out:n_in + len(jobs) + n_out + n_cast_out])
        scratch = refs[n_in + len(jobs) + n_out + n_cast_out:]
        for job, src_ref in zip(jobs, srcs):
            job.run(src_ref, [cast_outs.pop(0) for _ in job.outputs])
        body(*ins, *outs, *scratch)

    return wrapped


def _sigmoid(x):
    return 0.5 * jnp.tanh(0.5 * x) + 0.5


IN_TM = 1024
GATE_TM = 512
GATE_DIM = 2 * D_MODEL
ROPE_DIM = A_Q_DIM + A_KV_DIM
NORM_DIM = 2 * B_DIM
PLAIN_DIM = B_DIM + A_KV_DIM


def _in_gate_kernel(x_ref, g_ref, w_ref, o_ref, h_ref):
    h_ref[...] = _rmsnorm(x_ref[...], g_ref[...]).astype(BF16)
    lhs = h_ref[...]
    for k in range(GATE_DIM // SUB):
        cols = slice(k * SUB, (k + 1) * SUB)
        acc = jnp.dot(lhs, w_ref[:, cols], preferred_element_type=F32)
        o_ref[:, cols] = _sigmoid(acc).astype(BF16)


def _in_gate(x2, norm_g, w_gate, casts):
    n = x2.shape[0]
    steps = n // GATE_TM
    jobs = [_CastJob(src, steps, ranges) for src, ranges in casts]
    row_block = lambda width: pl.BlockSpec((GATE_TM, width), lambda i: (i, 0))
    return pl.pallas_call(
        _with_casts(_in_gate_kernel, jobs, n_in=3, n_out=2),
        out_shape=[jax.ShapeDtypeStruct((n, GATE_DIM), BF16),
                   jax.ShapeDtypeStruct((n, D_MODEL), BF16)]
                  + [s for j in jobs for s in j.out_shapes()],
        grid=(steps,),
        in_specs=[row_block(D_MODEL), _resident((1, D_MODEL)), _resident((D_MODEL, GATE_DIM))]
                 + [j.in_spec() for j in jobs],
        out_specs=[row_block(GATE_DIM), row_block(D_MODEL)]
                  + [s for j in jobs for s in j.out_specs()],
        compiler_params=_params("arbitrary"),
        name="in_gate",
    )(x2, norm_g, w_gate, *[j.src for j in jobs])


def _store_heads(o_ref, k, acc, fn):
    for t in range(HEADS_PER_SUB):
        col = k * SUB + t * HEAD_DIM
        o_ref[:, col:col + HEAD_DIM] = fn(acc[:, t * HEAD_DIM:(t + 1) * HEAD_DIM]).astype(BF16)


def _pair_columns(head_a, head_b):
    half = HEAD_DIM // 2
    a, b = head_a * HEAD_DIM, head_b * HEAD_DIM
    return [(a, a + half), (b, b + half), (a + half, a + HEAD_DIM), (b + half, b + HEAD_DIM)]


ROPE_PAIRS = [(p, p + A_GROUP) for p in range(A_GROUP)] + [(A_Q_HEADS, A_Q_HEADS + 1)]
assert A_KV_HEADS == 2 and len(ROPE_PAIRS) * SUB == ROPE_DIM
ROPE_COLUMNS = [piece for pair in ROPE_PAIRS for piece in _pair_columns(*pair)]
HALF_LANES = HEAD_DIM // 2


def _in_rope_kernel(h_ref, w_ref, cos_ref, sin_ref, gq_ref, gk_ref, o_ref):
    lhs = h_ref[...]
    first_head = lax.broadcasted_iota(jnp.int32, (IN_TM, HEAD_DIM), 1) < HALF_LANES
    for k in range(ROPE_DIM // SUB):
        acc = jnp.dot(lhs, w_ref[:, k * SUB:(k + 1) * SUB], preferred_element_type=F32)
        gain_ref = gq_ref if k * SUB < A_Q_DIM else gk_ref
        lo, hi = acc[:, :HEAD_DIM], acc[:, HEAD_DIM:]
        sq = lo * lo + hi * hi
        ss_a = jnp.sum(jnp.where(first_head, sq, 0.0), axis=-1, keepdims=True)
        ss_b = jnp.sum(jnp.where(first_head, 0.0, sq), axis=-1, keepdims=True)
        inv = jnp.where(first_head, lax.rsqrt(ss_a * (1.0 / HEAD_DIM) + EPS),
                        lax.rsqrt(ss_b * (1.0 / HEAD_DIM) + EPS))
        lo = lo * inv * gain_ref[0:1, :]
        hi = hi * inv * gain_ref[1:2, :]
        cos, sin = cos_ref[...], sin_ref[...]
        o_ref[:, k * SUB:k * SUB + HEAD_DIM] = (lo * cos - hi * sin).astype(BF16)
        o_ref[:, k * SUB + HEAD_DIM:(k + 1) * SUB] = (hi * cos + lo * sin).astype(BF16)


def _in_norm_kernel(h_ref, w_ref, gq_ref, gk_ref, o_ref):
    lhs = h_ref[...]
    for k in range(NORM_DIM // SUB):
        acc = jnp.dot(lhs, w_ref[:, k * SUB:(k + 1) * SUB], preferred_element_type=F32)
        gain_ref = gq_ref if k * SUB < B_DIM else gk_ref
        _store_heads(o_ref, k, acc, lambda a: _rmsnorm(a, gain_ref[...]))


def _in_plain_kernel(h_ref, wvb_ref, wva_ref, o_ref):
    lhs = h_ref[...]
    for k in range(PLAIN_DIM // SUB):
        cols = slice(k * SUB, (k + 1) * SUB)
        w = wvb_ref[:, cols] if k * SUB < B_DIM else wva_ref[:, k * SUB - B_DIM:(k + 1) * SUB - B_DIM]
        o_ref[:, cols] = jnp.dot(lhs, w, preferred_element_type=F32).astype(BF16)


def _in_heads_kernel(h_ref, w_rope_ref, w_norm_ref, w_vb_ref, w_va_ref, cos_ref, sin_ref,
                     gqa_ref, gka_ref, gqb_ref, gkb_ref, rope_ref, norm_ref, plain_ref):
    _in_rope_kernel(h_ref, w_rope_ref, cos_ref, sin_ref, gqa_ref, gka_ref, rope_ref)
    _in_norm_kernel(h_ref, w_norm_ref, gqb_ref, gkb_ref, norm_ref)
    _in_plain_kernel(h_ref, w_vb_ref, w_va_ref, plain_ref)


def _in_heads(h, weights, cos, sin, gains, casts):
    n = h.shape[0]
    steps = n // IN_TM
    seq_tiles = SEQ // IN_TM
    jobs = [_CastJob(src, steps) for src in casts]
    row_block = lambda width: pl.BlockSpec((IN_TM, width), lambda i: (i, 0))
    pos_spec = pl.BlockSpec((IN_TM, HEAD_DIM), lambda i: (i % seq_tiles, 0))
    widths = (ROPE_DIM, NORM_DIM, PLAIN_DIM)
    return pl.pallas_call(
        _with_casts(_in_heads_kernel, jobs, n_in=11, n_out=3),
        out_shape=[jax.ShapeDtypeStruct((n, width), BF16) for width in widths]
                  + [s for j in jobs for s in j.out_shapes()],
        grid=(steps,),
        in_specs=([row_block(D_MODEL)] + [_resident(w.shape) for w in weights]
                  + [pos_spec, pos_spec] + [_resident(g.shape) for g in gains]
                  + [j.in_spec() for j in jobs]),
        out_specs=[row_block(width) for width in widths] + [s for j in jobs for s in j.out_specs()],
        compiler_params=_params("arbitrary"),
        name="in_heads",
    )(h, *weights, cos, sin, *gains, *[j.src for j in jobs])


A_TQ = 512
A_BLK = WINDOW
A_NB = A_TQ // A_BLK
A_CHUNKS = SEQ // A_TQ


def _attn_a_kernel(sink_ref, q_ref, kp_ref, km_ref, kn_ref, vp_ref, vm_ref, vn_ref, o_ref):
    c = pl.program_id(1)
    k_all = jnp.concatenate([kp_ref[...], km_ref[...], kn_ref[...]], axis=0)
    v_all = jnp.concatenate([vp_ref[...], vm_ref[...], vn_ref[...]], axis=0)
    qq = lax.broadcasted_iota(jnp.int32, (A_BLK, 3 * A_BLK), 0)
    kk = lax.broadcasted_iota(jnp.int32, (A_BLK, 3 * A_BLK), 1)
    d = kk - qq
    band = (d >= 0) & (d <= 2 * WINDOW)
    ones = jnp.ones((3 * A_BLK, HEAD_DIM), BF16)
    lane = lax.broadcasted_iota(jnp.int32, (A_BLK, SUB), 1) % HEAD_DIM
    own_lanes = (lane < HALF_LANES, lane >= HALF_LANES)
    for n in range(A_NB):
        valid = band
        if n == 0:
            valid = valid & ((kk >= A_BLK) | (c > 0))
        if n == A_NB - 1:
            valid = valid & ((kk < 2 * A_BLK) | (c < A_CHUNKS - 1))
        k = k_all[n * A_BLK:(n + 3) * A_BLK, :]
        q_pairs = [q_ref[n * A_BLK:(n + 1) * A_BLK, p * SUB:(p + 1) * SUB] for p in range(A_GROUP)]
        qs = jnp.concatenate([jnp.where(own_lanes[h], q, jnp.zeros_like(q))
                              for h in range(A_KV_HEADS) for q in q_pairs], axis=0)
        s_all = lax.dot_general(qs, k, (((1,), (1,)), ((), ())),
                                preferred_element_type=F32)
        for h in range(A_KV_HEADS):
            v = v_all[n * A_BLK:(n + 3) * A_BLK, h * HEAD_DIM:(h + 1) * HEAD_DIM]
            heads = [h * A_GROUP + g for g in range(A_GROUP)]
            s = s_all[h * A_GROUP * A_BLK:(h + 1) * A_GROUP * A_BLK]
            ps, sink_terms = [], []
            for g, t in enumerate(heads):
                sg = jnp.where(valid, s[g * A_BLK:(g + 1) * A_BLK], NEG)
                m = jnp.maximum(jnp.max(sg, axis=-1, keepdims=True), sink_ref[t])
                ps.append(jnp.exp(sg - m).astype(BF16))
                sink_terms.append(jnp.exp(sink_ref[t] - m))
            o = jnp.dot(jnp.concatenate(ps, axis=0), jnp.concatenate([v, ones], axis=1),
                        preferred_element_type=F32)
            for g, t in enumerate(heads):
                og = o[g * A_BLK:(g + 1) * A_BLK]
                og = og[:, :HEAD_DIM] / (og[:, HEAD_DIM:] + sink_terms[g])
                o_ref[n * A_BLK:(n + 1) * A_BLK, t * HEAD_DIM:(t + 1) * HEAD_DIM] = og.astype(BF16)


def _attn_a(qk, vals, sink, batch, casts):
    n = qk.shape[0]
    blk_per_seq = SEQ // A_BLK
    k_col = A_Q_DIM // A_KV_DIM
    v_col = B_DIM // A_KV_DIM
    jobs = [_CastJob(src, batch * A_CHUNKS, step_of=lambda b, c, *_: b * A_CHUNKS + c)
            for src in casts]

    def prev_map(col):
        return lambda b, c, *_: (b * blk_per_seq + jnp.maximum(c * A_NB - 1, 0), col)

    def main_map(col):
        return lambda b, c, *_: (b * A_CHUNKS + c, col)

    def next_map(col):
        return lambda b, c, *_: (b * blk_per_seq + jnp.minimum(c * A_NB + A_NB, blk_per_seq - 1), col)

    halo = lambda m: pl.BlockSpec((A_BLK, A_KV_DIM), m)
    main = lambda m: pl.BlockSpec((A_TQ, A_KV_DIM), m)
    return pl.pallas_call(
        _with_casts(_attn_a_kernel, jobs, n_in=8, n_out=1),
        out_shape=[jax.ShapeDtypeStruct((n, A_Q_DIM), BF16)] + [s for j in jobs for s in j.out_shapes()],
        grid_spec=pltpu.PrefetchScalarGridSpec(
            num_scalar_prefetch=1,
            grid=(batch, A_CHUNKS),
            in_specs=[
                pl.BlockSpec((A_TQ, A_Q_DIM), main_map(0)),
                halo(prev_map(k_col)), main(main_map(k_col)), halo(next_map(k_col)),
                halo(prev_map(v_col)), main(main_map(v_col)), halo(next_map(v_col)),
            ] + [j.in_spec() for j in jobs],
            out_specs=[pl.BlockSpec((A_TQ, A_Q_DIM), main_map(0))]
                      + [s for j in jobs for s in j.out_specs()],
        ),
        compiler_params=_params("arbitrary", "arbitrary"),
        name="attn_a",
    )(sink, qk, qk, qk, qk, vals, vals, vals, *[j.src for j in jobs])


GRID_ROWS = SEQ // GRID_W
B_QR = 4
B_KR = B_QR + WIN_H
B_NRB = GRID_ROWS // B_QR
B_NQ = B_QR * GRID_W
B_NK = B_KR * GRID_W
B_KBLOCKS = B_KR // B_QR
B_KS_MAX = B_NRB - B_KBLOCKS
B_PATTERN_ROWBLOCKS = (0, 1, B_NRB - 1)


def _attn_b_row_windows():
    starts, row_ok = [], []
    for rb in B_PATTERN_ROWBLOCKS:
        ks = int(np.clip(rb - 1, 0, B_KS_MAX))
        qr = rb * B_QR + np.arange(B_QR)
        kr = ks * B_QR + np.arange(B_KR)
        rs = np.clip(qr - WIN_H // 2, 0, GRID_ROWS - WIN_H)
        row_ok.append((kr[None, :] >= rs[:, None]) & (kr[None, :] < rs[:, None] + WIN_H))
        starts.append(kr[0] - qr + WIN_H - 1)
    return np.stack(starts), np.stack(row_ok)


B_ROW_START, B_ROW_OK = _attn_b_row_windows()
B_STRIP_LO = max(0, -int(B_ROW_START.min()))
B_NSTRIP = int(B_ROW_START.max()) + B_KR - 1 + B_STRIP_LO
B_LANES = 2 * GRID_W


def _attn_b_build_bias(strip_ref, bias_ref):
    qc = lax.broadcasted_iota(jnp.int32, (GRID_W, B_LANES), 0)
    lane = lax.broadcasted_iota(jnp.int32, (GRID_W, B_LANES), 1)
    kc = lane % GRID_W
    cs = jnp.clip(qc - WIN_W // 2, 0, GRID_W - WIN_W)
    col_ok = (kc >= cs) & (kc < cs + WIN_W)
    first_row = lane < GRID_W
    for h in range(B_HEADS):
        tiles = {}

        def tile(i):
            if i not in tiles:
                rows = jnp.broadcast_to(strip_ref[h, i:i + 1, :], (GRID_W, B_LANES))
                toeplitz = pltpu.roll(rows, 0, axis=1, stride=1, stride_axis=0)
                tiles[i] = jnp.where(col_ok, toeplitz, NEG)
            return tiles[i]

        for p in range(len(B_PATTERN_ROWBLOCKS)):
            for qr in range(B_QR):
                for c in range(B_KR // 2):
                    ok0, ok1 = B_ROW_OK[p, qr, 2 * c], B_ROW_OK[p, qr, 2 * c + 1]
                    if ok0 or ok1:
                        t = tile(int(B_ROW_START[p, qr]) + 2 * c + B_STRIP_LO)
                        if not ok1:
                            t = jnp.where(first_row, t, NEG)
                        if not ok0:
                            t = jnp.where(first_row, NEG, t)
                    else:
                        t = jnp.full((GRID_W, B_LANES), NEG, F32)
                    bias_ref[p, h, qr * GRID_W:(qr + 1) * GRID_W, c * B_LANES:(c + 1) * B_LANES] = t


def _attn_b_kernel(q_ref, k0_ref, k1_ref, k2_ref, v0_ref, v1_ref, v2_ref, strip_ref, o_ref,
                   bias_ref):
    r = pl.program_id(1)

    @pl.when((pl.program_id(0) == 0) & (r == 0))
    def _():
        _attn_b_build_bias(strip_ref, bias_ref)

    pattern = jnp.where(r == 0, 0, jnp.where(r == B_NRB - 1, 2, 1))
    k_refs = (k0_ref, k1_ref, k2_ref)
    v_refs = (v0_ref, v1_ref, v2_ref)
    ones = jnp.ones((B_NK, HEAD_DIM), BF16)
    for h in range(B_HEADS):
        hd = slice(h * HEAD_DIM, (h + 1) * HEAD_DIM)
        q = q_ref[:, :, hd].reshape(B_NQ, HEAD_DIM)
        k = jnp.concatenate([ref[:, :, hd] for ref in k_refs], axis=0).reshape(B_NK, HEAD_DIM)
        v = jnp.concatenate([ref[:, :, hd] for ref in v_refs], axis=0).reshape(B_NK, HEAD_DIM)
        s = lax.dot_general(q, k, (((1,), (1,)), ((), ())),
                            preferred_element_type=F32) + bias_ref[pattern, h]
        m = jnp.max(s, axis=-1, keepdims=True)
        p = jnp.exp(s - m).astype(BF16)
        o = jnp.dot(p, jnp.concatenate([v, ones], axis=1), preferred_element_type=F32)
        o = o[:, :HEAD_DIM] / o[:, HEAD_DIM:]
        o_ref[:, :, hd] = o.reshape(B_QR, GRID_W, HEAD_DIM).astype(BF16)


def _attn_b_strips(rpb):
    rows = B_NSTRIP + 1
    table = jnp.pad(rpb.astype(F32), ((0, 0), (B_STRIP_LO, rows - B_STRIP_LO - rpb.shape[1]), (0, 0)))
    first, second = table[:, :-1], table[:, 1:]
    gap = jnp.zeros(first.shape[:2] + (GRID_W - 2 * WIN_W + 1,), F32)
    strips = jnp.concatenate(
        [first[..., WIN_W - 1:], gap, second, gap, first[..., :WIN_W - 1]], axis=-1)
    assert strips.shape == (B_HEADS, B_NSTRIP, B_LANES)
    return strips


def _attn_b(qk3, vals3, strips, batch):
    rows = qk3.shape[0]

    def kv_map(col, t):
        return lambda b, r: (b * B_NRB + jnp.clip(r - 1, 0, B_KS_MAX) + t, 0, col)

    blk = lambda m: pl.BlockSpec((B_QR, GRID_W, B_DIM), m)
    return pl.pallas_call(
        _attn_b_kernel,
        out_shape=jax.ShapeDtypeStruct((rows, GRID_W, B_DIM), BF16),
        grid=(batch, B_NRB),
        in_specs=[
            blk(lambda b, r: (b * B_NRB + r, 0, 0)),
            blk(kv_map(1, 0)), blk(kv_map(1, 1)), blk(kv_map(1, 2)),
            blk(kv_map(0, 0)), blk(kv_map(0, 1)), blk(kv_map(0, 2)),
            _resident((B_HEADS, B_NSTRIP, B_LANES)),
        ],
        out_specs=blk(lambda b, r: (b * B_NRB + r, 0, 0)),
        scratch_shapes=[pltpu.VMEM((len(B_PATTERN_ROWBLOCKS), B_HEADS, B_NQ, B_NK), F32)],
        compiler_params=_params("arbitrary", "arbitrary"),
        name="attn_b",
    )(qk3, qk3, qk3, qk3, vals3, vals3, vals3, strips)


MG_TM = 512


def _merge_kernel(x_ref, oa_ref, ob_ref, ga_ref, gb_ref, wa_ref, wb_ref, wo_ref, g_ref,
                  x1_ref, h2_ref):
    a = jnp.dot(oa_ref[...], wa_ref[...], preferred_element_type=F32)
    b = jnp.dot(ob_ref[...], wb_ref[...], preferred_element_type=F32)
    merged = ga_ref[...].astype(F32) * a + gb_ref[...].astype(F32) * b
    y = jnp.dot(merged.astype(BF16), wo_ref[...], preferred_element_type=F32)
    x1 = x_ref[...] + y
    x1_ref[...] = x1
    h2_ref[...] = _rmsnorm(x1, g_ref[...]).astype(BF16)


def _merge(x2, out_a, out_b, gates, wa, wb, wo, norm_g):
    n = x2.shape[0]
    row_block = lambda width, col=0: pl.BlockSpec((MG_TM, width), lambda i: (i, col))
    return pl.pallas_call(
        _merge_kernel,
        out_shape=(jax.ShapeDtypeStruct((n, D_MODEL), F32),
                   jax.ShapeDtypeStruct((n, D_MODEL), BF16)),
        grid=(n // MG_TM,),
        in_specs=[
            row_block(D_MODEL), row_block(A_Q_DIM), row_block(B_DIM),
            row_block(D_MODEL, 0), row_block(D_MODEL, 1),
            _resident((A_Q_DIM, D_MODEL)), _resident((B_DIM, D_MODEL)),
            _resident((D_MODEL, D_MODEL)), _resident((1, D_MODEL)),
        ],
        out_specs=(row_block(D_MODEL), row_block(D_MODEL)),
        compiler_params=_params("arbitrary"),
        name="merge",
    )(x2, out_a, out_b, gates, gates, wa, wb, wo, norm_g)


UP_TM = 1024
UP_NJ = 4
UP_TN = D_FF // UP_NJ
UP_LANES = SUB // 2
UP_NSUB = UP_TN // UP_LANES
HALO = 16
UP_ROWS = UP_TM + 2 * HALO


def _ffn_up_kernel(hp_ref, hm_ref, hn_ref, wg_ref, wv_ref, cwg_ref, cwv_ref, cbg_ref, cbv_ref,
                   o_ref, lhs_ref, u_ref):
    i = pl.program_id(0)
    tiles_per_seq = SEQ // UP_TM

    @pl.when(pl.program_id(1) == 0)
    def _():
        first = (i % tiles_per_seq) == 0
        last = (i % tiles_per_seq) == tiles_per_seq - 1
        lhs_ref[0:HALO, :] = jnp.where(first, jnp.zeros_like(hp_ref[...]), hp_ref[...])
        lhs_ref[HALO:HALO + UP_TM, :] = hm_ref[...]
        lhs_ref[HALO + UP_TM:, :] = jnp.where(last, jnp.zeros_like(hn_ref[...]), hn_ref[...])

    def conv(u, lanes, cw_ref, cb_ref, cols):
        below = u[HALO - 1:HALO - 1 + UP_TM, lanes]
        mid = u[HALO:HALO + UP_TM, lanes]
        above = u[HALO + 1:HALO + 1 + UP_TM, lanes]
        y = mid * cw_ref[1:2, cols] + cb_ref[:, cols]
        y = y + below * cw_ref[0:1, cols]
        return y + above * cw_ref[2:3, cols]

    lhs = lhs_ref[...]
    for k in range(UP_NSUB):
        cols = slice(k * UP_LANES, (k + 1) * UP_LANES)
        u = u_ref.at[k % 2]
        w = jnp.concatenate([wg_ref[:, cols], wv_ref[:, cols]], axis=1)
        u[...] = jnp.dot(lhs, w, preferred_element_type=F32)
        half_gate = 0.5 * conv(u, slice(0, UP_LANES), cwg_ref, cbg_ref, cols)
        val = conv(u, slice(UP_LANES, SUB), cwv_ref, cbv_ref, cols)
        silu = half_gate * jnp.tanh(half_gate) + half_gate
        o_ref[:, cols] = (silu * val).astype(BF16)


def _ffn_up(h2, w_up, conv_w, conv_b):
    n = h2.shape[0]
    halo_blocks = UP_TM // HALO
    last_halo = n // HALO - 1
    return pl.pallas_call(
        _ffn_up_kernel,
        out_shape=jax.ShapeDtypeStruct((n, D_FF), BF16),
        grid=(n // UP_TM, UP_NJ),
        in_specs=[
            pl.BlockSpec((HALO, D_MODEL), lambda i, j: (jnp.maximum(i * halo_blocks - 1, 0), 0)),
            pl.BlockSpec((UP_TM, D_MODEL), lambda i, j: (i, 0)),
            pl.BlockSpec((HALO, D_MODEL),
                         lambda i, j: (jnp.minimum((i + 1) * halo_blocks, last_halo), 0)),
            pl.BlockSpec((D_MODEL, UP_TN), lambda i, j: (0, j)),
            pl.BlockSpec((D_MODEL, UP_TN), lambda i, j: (0, UP_NJ + j)),
            pl.BlockSpec((3, UP_TN), lambda i, j: (0, j)),
            pl.BlockSpec((3, UP_TN), lambda i, j: (0, UP_NJ + j)),
            pl.BlockSpec((1, UP_TN), lambda i, j: (0, j)),
            pl.BlockSpec((1, UP_TN), lambda i, j: (0, UP_NJ + j)),
        ],
        out_specs=pl.BlockSpec((UP_TM, UP_TN), lambda i, j: (i, j)),
        scratch_shapes=[pltpu.VMEM((UP_ROWS, D_MODEL), BF16),
                        pltpu.VMEM((2, UP_ROWS, SUB), F32)],
        compiler_params=_params("arbitrary", "arbitrary"),
        name="ffn_up",
    )(h2, h2, h2, w_up, w_up, conv_w, conv_w, conv_b, conv_b)


DN_TM = 512


def _ffn_down_kernel(a_ref, w_ref, x_ref, o_ref):
    lhs = a_ref[...]
    for k in range(D_MODEL // SUB):
        cols = slice(k * SUB, (k + 1) * SUB)
        o_ref[:, cols] = x_ref[:, cols] + jnp.dot(lhs, w_ref[:, cols], preferred_element_type=F32)


def _ffn_down(act, w_down, x1):
    n = act.shape[0]
    row_block = lambda width: pl.BlockSpec((DN_TM, width), lambda i: (i, 0))
    return pl.pallas_call(
        _ffn_down_kernel,
        out_shape=jax.ShapeDtypeStruct((n, D_MODEL), F32),
        grid=(n // DN_TM,),
        in_specs=[row_block(D_FF), _resident((D_FF, D_MODEL)), row_block(D_MODEL)],
        out_specs=row_block(D_MODEL),
        compiler_params=_params("arbitrary"),
        name="ffn_down",
    )(act, w_down, x1)


def _rope_tables():
    half = HEAD_DIM // 2
    pos = jnp.arange(SEQ, dtype=F32)
    inv_freq = ROPE_THETA ** (-jnp.arange(half, dtype=F32) * (2.0 / HEAD_DIM))
    ang = pos[:, None] * inv_freq[None, :]
    cos, sin = jnp.cos(ang), jnp.sin(ang)
    return jnp.concatenate([cos, cos], axis=-1), jnp.concatenate([sin, sin], axis=-1)


def _paired_gain(gain):
    lo, hi = gain[:HALF_LANES], gain[HALF_LANES:]
    return jnp.stack([jnp.concatenate([lo, lo]), jnp.concatenate([hi, hi])])


IN_COL_VA = ROPE_DIM
IN_COL_QB = IN_COL_VA + A_KV_DIM
IN_COL_VB = IN_COL_QB + NORM_DIM
IN_COL_GATE = IN_COL_VB + B_DIM
IN_HEAD_OUTPUTS = [ROPE_COLUMNS, [(IN_COL_QB, IN_COL_VB)], [(IN_COL_VB, IN_COL_GATE)],
                   [(IN_COL_VA, IN_COL_QB)]]


def kernel(x, norm_mix, w_in, a_q_norm, a_k_norm, a_sink, b_q_norm, b_k_norm, b_rpb,
           w_branch_a, w_branch_b, w_out, norm_ffn, w_up, conv_w, conv_b, w_down):
    batch, seq, d_model = x.shape
    assert (seq, d_model) == (SEQ, D_MODEL)
    n = batch * seq
    cos, sin = _rope_tables()
    scale = 1.0 / math.sqrt(HEAD_DIM)
    x2 = x.reshape(n, d_model)
    for l in range(norm_mix.shape[0]):
        w_gate = w_in[l][:, IN_COL_GATE:].astype(BF16)
        gates, h, w_rope, w_norm, w_vb, w_va, w_up_bf = _in_gate(
            x2, norm_mix[l][None], w_gate, [(w_in[l], IN_HEAD_OUTPUTS), (w_up[l], None)])
        gains = (_paired_gain(a_q_norm[l] * scale), _paired_gain(a_k_norm[l]),
                 b_q_norm[l][None] * scale, b_k_norm[l][None])
        qk_a, qk_b, vals = _in_heads(h, (w_rope, w_norm, w_vb, w_va), cos, sin, gains, [])
        out_a, wa_bf, wb_bf, w_out_bf, w_down_bf = _attn_a(
            qk_a, vals, a_sink[l], batch, [w_branch_a[l], w_branch_b[l], w_out[l], w_down[l]])
        out_b = _attn_b(qk_b.reshape(n // GRID_W, GRID_W, NORM_DIM),
                        vals.reshape(n // GRID_W, GRID_W, PLAIN_DIM), _attn_b_strips(b_rpb[l]), batch)
        x1, h2 = _merge(x2, out_a, out_b.reshape(n, B_DIM), gates, wa_bf, wb_bf, w_out_bf,
                        norm_ffn[l][None])
        act = _ffn_up(h2, w_up_bf, conv_w[l], conv_b[l][None])
        x2 = _ffn_down(act, w_down_bf, x1)
    return x2.reshape(batch, seq, d_model)
```

```python
import math

import numpy as np
import jax
import jax.numpy as jnp
from jax import lax
from jax.experimental import pallas as pl
from jax.experimental.pallas import tpu as pltpu

D_MODEL = 2048
SEQ = 4096
HEAD_DIM = 128
A_Q_HEADS = 8
A_KV_HEADS = 2
A_GROUP = A_Q_HEADS // A_KV_HEADS
WINDOW = 128
B_HEADS = 8
GRID_W = 64
WIN_H = 8
WIN_W = 16
D_FF = 5632
ROPE_THETA = 10000.0
EPS = 1e-6
NEG = -1e30

A_Q_DIM = A_Q_HEADS * HEAD_DIM
A_KV_DIM = A_KV_HEADS * HEAD_DIM
B_DIM = B_HEADS * HEAD_DIM

VMEM_LIMIT_BYTES = 56 * 1024 * 1024
MXU_COLS = 256
SUB = MXU_COLS
HEADS_PER_SUB = SUB // HEAD_DIM

BF16 = jnp.bfloat16
F32 = jnp.float32


def _params(*semantics):
    return pltpu.CompilerParams(dimension_semantics=semantics, vmem_limit_bytes=VMEM_LIMIT_BYTES)


def _resident(shape):
    return pl.BlockSpec(shape, lambda *_: (0,) * len(shape), pipeline_mode=pl.Buffered(1))


def _rmsnorm(x, gain):
    ms = jnp.mean(x * x, axis=-1, keepdims=True)
    return x * lax.rsqrt(ms + EPS) * gain


class _CastJob:
    def __init__(self, src, steps, outputs=None, step_of=lambda i: i):
        rows, cols = src.shape
        self.src = src
        self.outputs = outputs or [[(0, cols)]]
        self.slab = rows // steps
        assert self.slab * steps == rows
        self.span = max(hi for pieces in self.outputs for _, hi in pieces)
        self.widths = [sum(hi - lo for lo, hi in pieces) for pieces in self.outputs]
        self.index_map = lambda *idx: (step_of(*idx), 0)

    def in_spec(self):
        return pl.BlockSpec((self.slab, self.span), self.index_map)

    def out_specs(self):
        return [pl.BlockSpec((self.slab, width), self.index_map) for width in self.widths]

    def out_shapes(self):
        return [jax.ShapeDtypeStruct((self.src.shape[0], width), BF16) for width in self.widths]

    def run(self, src_ref, out_refs):
        for pieces, out_ref in zip(self.outputs, out_refs):
            parts = [src_ref[:, lo:hi] for lo, hi in pieces]
            value = parts[0] if len(parts) == 1 else jnp.concatenate(parts, axis=1)
            out_ref[...] = value.astype(BF16)


def _with_casts(body, jobs, n_in, n_out):
    n_cast_out = sum(len(j.outputs) for j in jobs)

    def wrapped(*refs):
        ins = refs[:n_in]
        srcs = refs[n_in:n_in + len(jobs)]
        outs = refs[n_in + len(jobs):n_in + len(jobs) + n_out]
        cast_outs = list(refs[n_in + len(jobs) + n_out:n_in + len(jobs) + n_out + n_cast_out])
        scratch = refs[n_in + len(jobs) + n_out + n_cast_out:]
        for job, src_ref in zip(jobs, srcs):
            job.run(src_ref, [cast_outs.pop(0) for _ in job.outputs])
        body(*ins, *outs, *scratch)

    return wrapped


def _sigmoid(x):
    return 0.5 * jnp.tanh(0.5 * x) + 0.5


IN_TM = 1024
GATE_TM = 512
GATE_DIM = 2 * D_MODEL
ROPE_DIM = A_Q_DIM + A_KV_DIM
NORM_DIM = 2 * B_DIM
PLAIN_DIM = B_DIM + A_KV_DIM


def _in_gate_kernel(x_ref, g_ref, w_ref, o_ref, h_ref):
    h_ref[...] = _rmsnorm(x_ref[...], g_ref[...]).astype(BF16)
    lhs = h_ref[...]
    for k in range(GATE_DIM // SUB):
        cols = slice(k * SUB, (k + 1) * SUB)
        acc = jnp.dot(lhs, w_ref[:, cols], preferred_element_type=F32)
        o_ref[:, cols] = _sigmoid(acc).astype(BF16)


def _in_gate(x2, norm_g, w_gate, casts):
    n = x2.shape[0]
    steps = n // GATE_TM
    jobs = [_CastJob(src, steps, ranges) for src, ranges in casts]
    row_block = lambda width: pl.BlockSpec((GATE_TM, width), lambda i: (i, 0))
    return pl.pallas_call(
        _with_casts(_in_gate_kernel, jobs, n_in=3, n_out=2),
        out_shape=[jax.ShapeDtypeStruct((n, GATE_DIM), BF16),
                   jax.ShapeDtypeStruct((n, D_MODEL), BF16)]
                  + [s for j in jobs for s in j.out_shapes()],
        grid=(steps,),
        in_specs=[row_block(D_MODEL), _resident((1, D_MODEL)), _resident((D_MODEL, GATE_DIM))]
                 + [j.in_spec() for j in jobs],
        out_specs=[row_block(GATE_DIM), row_block(D_MODEL)]
                  + [s for j in jobs for s in j.out_specs()],
        compiler_params=_params("arbitrary"),
        name="in_gate",
    )(x2, norm_g, w_gate, *[j.src for j in jobs])


def _store_heads(o_ref, k, acc, fn):
    for t in range(HEADS_PER_SUB):
        col = k * SUB + t * HEAD_DIM
        o_ref[:, col:col + HEAD_DIM] = fn(acc[:, t * HEAD_DIM:(t + 1) * HEAD_DIM]).astype(BF16)


def _pair_columns(head_a, head_b):
    half = HEAD_DIM // 2
    a, b = head_a * HEAD_DIM, head_b * HEAD_DIM
    return [(a, a + half), (b, b + half), (a + half, a + HEAD_DIM), (b + half, b + HEAD_DIM)]


ROPE_PAIRS = [(p, p + A_GROUP) for p in range(A_GROUP)] + [(A_Q_HEADS, A_Q_HEADS + 1)]
assert A_KV_HEADS == 2 and len(ROPE_PAIRS) * SUB == ROPE_DIM
ROPE_COLUMNS = [piece for pair in ROPE_PAIRS for piece in _pair_columns(*pair)]
HALF_LANES = HEAD_DIM // 2


def _in_rope_kernel(h_ref, w_ref, cos_ref, sin_ref, gq_ref, gk_ref, o_ref):
    lhs = h_ref[...]
    first_head = lax.broadcasted_iota(jnp.int32, (IN_TM, HEAD_DIM), 1) < HALF_LANES
    for k in range(ROPE_DIM // SUB):
        acc = jnp.dot(lhs, w_ref[:, k * SUB:(k + 1) * SUB], preferred_element_type=F32)
        gain_ref = gq_ref if k * SUB < A_Q_DIM else gk_ref
        lo, hi = acc[:, :HEAD_DIM], acc[:, HEAD_DIM:]
        sq = lo * lo + hi * hi
        ss_a = jnp.sum(jnp.where(first_head, sq, 0.0), axis=-1, keepdims=True)
        ss_b = jnp.sum(jnp.where(first_head, 0.0, sq), axis=-1, keepdims=True)
        inv = jnp.where(first_head, lax.rsqrt(ss_a * (1.0 / HEAD_DIM) + EPS),
                        lax.rsqrt(ss_b * (1.0 / HEAD_DIM) + EPS))
        lo = lo * inv * gain_ref[0:1, :]
        hi = hi * inv * gain_ref[1:2, :]
        cos, sin = cos_ref[...], sin_ref[...]
        o_ref[:, k * SUB:k * SUB + HEAD_DIM] = (lo * cos - hi * sin).astype(BF16)
        o_ref[:, k * SUB + HEAD_DIM:(k + 1) * SUB] = (hi * cos + lo * sin).astype(BF16)


def _in_norm_kernel(h_ref, w_ref, gq_ref, gk_ref, o_ref):
    lhs = h_ref[...]
    for k in range(NORM_DIM // SUB):
        acc = jnp.dot(lhs, w_ref[:, k * SUB:(k + 1) * SUB], preferred_element_type=F32)
        gain_ref = gq_ref if k * SUB < B_DIM else gk_ref
        _store_heads(o_ref, k, acc, lambda a: _rmsnorm(a, gain_ref[...]))


def _in_plain_kernel(h_ref, wvb_ref, wva_ref, o_ref):
    lhs = h_ref[...]
    for k in range(PLAIN_DIM // SUB):
        cols = slice(k * SUB, (k + 1) * SUB)
        w = wvb_ref[:, cols] if k * SUB < B_DIM else wva_ref[:, k * SUB - B_DIM:(k + 1) * SUB - B_DIM]
        o_ref[:, cols] = jnp.dot(lhs, w, preferred_element_type=F32).astype(BF16)


def _in_heads_kernel(h_ref, w_rope_ref, w_norm_ref, w_vb_ref, w_va_ref, cos_ref, sin_ref,
                     gqa_ref, gka_ref, gqb_ref, gkb_ref, rope_ref, norm_ref, plain_ref):
    _in_rope_kernel(h_ref, w_rope_ref, cos_ref, sin_ref, gqa_ref, gka_ref, rope_ref)
    _in_norm_kernel(h_ref, w_norm_ref, gqb_ref, gkb_ref, norm_ref)
    _in_plain_kernel(h_ref, w_vb_ref, w_va_ref, plain_ref)


def _in_heads(h, weights, cos, sin, gains, casts):
    n = h.shape[0]
    steps = n // IN_TM
    seq_tiles = SEQ // IN_TM
    jobs = [_CastJob(src, steps) for src in casts]
    row_block = lambda width: pl.BlockSpec((IN_TM, width), lambda i: (i, 0))
    pos_spec = pl.BlockSpec((IN_TM, HEAD_DIM), lambda i: (i % seq_tiles, 0))
    widths = (ROPE_DIM, NORM_DIM, PLAIN_DIM)
    return pl.pallas_call(
        _with_casts(_in_heads_kernel, jobs, n_in=11, n_out=3),
        out_shape=[jax.ShapeDtypeStruct((n, width), BF16) for width in widths]
                  + [s for j in jobs for s in j.out_shapes()],
        grid=(steps,),
        in_specs=([row_block(D_MODEL)] + [_resident(w.shape) for w in weights]
                  + [pos_spec, pos_spec] + [_resident(g.shape) for g in gains]
                  + [j.in_spec() for j in jobs]),
        out_specs=[row_block(width) for width in widths] + [s for j in jobs for s in j.out_specs()],
        compiler_params=_params("arbitrary"),
        name="in_heads",
    )(h, *weights, cos, sin, *gains, *[j.src for j in jobs])


A_TQ = 512
A_BLK = WINDOW
A_NB = A_TQ // A_BLK
A_CHUNKS = SEQ // A_TQ


def _attn_a_kernel(sink_ref, q_ref, kp_ref, km_ref, kn_ref, vp_ref, vm_ref, vn_ref, o_ref):
    c = pl.program_id(1)
    k_all = jnp.concatenate([kp_ref[...], km_ref[...], kn_ref[...]], axis=0)
    v_all = jnp.concatenate([vp_ref[...], vm_ref[...], vn_ref[...]], axis=0)
    qq = lax.broadcasted_iota(jnp.int32, (A_BLK, 3 * A_BLK), 0)
    kk = lax.broadcasted_iota(jnp.int32, (A_BLK, 3 * A_BLK), 1)
    d = kk - qq
    band = (d >= 0) & (d <= 2 * WINDOW)
    ones = jnp.ones((3 * A_BLK, HEAD_DIM), BF16)
    lane = lax.broadcasted_iota(jnp.int32, (A_BLK, SUB), 1) % HEAD_DIM
    own_lanes = (lane < HALF_LANES, lane >= HALF_LANES)
    for n in range(A_NB):
        valid = band
        if n == 0:
            valid = valid & ((kk >= A_BLK) | (c > 0))
        if n == A_NB - 1:
            valid = valid & ((kk < 2 * A_BLK) | (c < A_CHUNKS - 1))
        k = k_all[n * A_BLK:(n + 3) * A_BLK, :]
        q_pairs = [q_ref[n * A_BLK:(n + 1) * A_BLK, p * SUB:(p + 1) * SUB] for p in range(A_GROUP)]
        qs = jnp.concatenate([jnp.where(own_lanes[h], q, jnp.zeros_like(q))
                              for h in range(A_KV_HEADS) for q in q_pairs], axis=0)
        s_all = lax.dot_general(qs, k, (((1,), (1,)), ((), ())),
                                preferred_element_type=F32)
        for h in range(A_KV_HEADS):
            v = v_all[n * A_BLK:(n + 3) * A_BLK, h * HEAD_DIM:(h + 1) * HEAD_DIM]
            heads = [h * A_GROUP + g for g in range(A_GROUP)]
            s = s_all[h * A_GROUP * A_BLK:(h + 1) * A_GROUP * A_BLK]
            ps, sink_terms = [], []
            for g, t in enumerate(heads):
                sg = jnp.where(valid, s[g * A_BLK:(g + 1) * A_BLK], NEG)
                m = jnp.maximum(jnp.max(sg, axis=-1, keepdims=True), sink_ref[t])
                ps.append(jnp.exp(sg - m).astype(BF16))
                sink_terms.append(jnp.exp(sink_ref[t] - m))
            o = jnp.dot(jnp.concatenate(ps, axis=0), jnp.concatenate([v, ones], axis=1),
                        preferred_element_type=F32)
            for g, t in enumerate(heads):
                og = o[g * A_BLK:(g + 1) * A_BLK]
                og = og[:, :HEAD_DIM] / (og[:, HEAD_DIM:] + sink_terms[g])
                o_ref[n * A_BLK:(n + 1) * A_BLK, t * HEAD_DIM:(t + 1) * HEAD_DIM] = og.astype(BF16)


def _attn_a(qk, vals, sink, batch, casts):
    n = qk.shape[0]
    blk_per_seq = SEQ // A_BLK
    k_col = A_Q_DIM // A_KV_DIM
    v_col = B_DIM // A_KV_DIM
    jobs = [_CastJob(src, batch * A_CHUNKS, step_of=lambda b, c, *_: b * A_CHUNKS + c)
            for src in casts]

    def prev_map(col):
        return lambda b, c, *_: (b * blk_per_seq + jnp.maximum(c * A_NB - 1, 0), col)

    def main_map(col):
        return lambda b, c, *_: (b * A_CHUNKS + c, col)

    def next_map(col):
        return lambda b, c, *_: (b * blk_per_seq + jnp.minimum(c * A_NB + A_NB, blk_per_seq - 1), col)

    halo = lambda m: pl.BlockSpec((A_BLK, A_KV_DIM), m)
    main = lambda m: pl.BlockSpec((A_TQ, A_KV_DIM), m)
    return pl.pallas_call(
        _with_casts(_attn_a_kernel, jobs, n_in=8, n_out=1),
        out_shape=[jax.ShapeDtypeStruct((n, A_Q_DIM), BF16)] + [s for j in jobs for s in j.out_shapes()],
        grid_spec=pltpu.PrefetchScalarGridSpec(
            num_scalar_prefetch=1,
            grid=(batch, A_CHUNKS),
            in_specs=[
                pl.BlockSpec((A_TQ, A_Q_DIM), main_map(0)),
                halo(prev_map(k_col)), main(main_map(k_col)), halo(next_map(k_col)),
                halo(prev_map(v_col)), main(main_map(v_col)), halo(next_map(v_col)),
            ] + [j.in_spec() for j in jobs],
            out_specs=[pl.BlockSpec((A_TQ, A_Q_DIM), main_map(0))]
                      + [s for j in jobs for s in j.out_specs()],
        ),
        compiler_params=_params("arbitrary", "arbitrary"),
        name="attn_a",
    )(sink, qk, qk, qk, qk, vals, vals, vals, *[j.src for j in jobs])


GRID_ROWS = SEQ // GRID_W
B_QR = 4
B_KR = B_QR + WIN_H
B_NRB = GRID_ROWS // B_QR
B_NQ = B_QR * GRID_W
B_NK = B_KR * GRID_W
B_KBLOCKS = B_KR // B_QR
B_KS_MAX = B_NRB - B_KBLOCKS
B_PATTERN_ROWBLOCKS = (0, 1, B_NRB - 1)


def _attn_b_row_windows():
    starts, row_ok = [], []
    for rb in B_PATTERN_ROWBLOCKS:
        ks = int(np.clip(rb - 1, 0, B_KS_MAX))
        qr = rb * B_QR + np.arange(B_QR)
        kr = ks * B_QR + np.arange(B_KR)
        rs = np.clip(qr - WIN_H // 2, 0, GRID_ROWS - WIN_H)
        row_ok.append((kr[None, :] >= rs[:, None]) & (kr[None, :] < rs[:, None] + WIN_H))
        starts.append(kr[0] - qr + WIN_H - 1)
    return np.stack(starts), np.stack(row_ok)


B_ROW_START, B_ROW_OK = _attn_b_row_windows()
B_STRIP_LO = max(0, -int(B_ROW_START.min()))
B_NSTRIP = int(B_ROW_START.max()) + B_KR - 1 + B_STRIP_LO
B_LANES = 2 * GRID_W


def _attn_b_build_bias(strip_ref, bias_ref):
    qc = lax.broadcasted_iota(jnp.int32, (GRID_W, B_LANES), 0)
    lane = lax.broadcasted_iota(jnp.int32, (GRID_W, B_LANES), 1)
    kc = lane % GRID_W
    cs = jnp.clip(qc - WIN_W // 2, 0, GRID_W - WIN_W)
    col_ok = (kc >= cs) & (kc < cs + WIN_W)
    first_row = lane < GRID_W
    for h in range(B_HEADS):
        tiles = {}

        def tile(i):
            if i not in tiles:
                rows = jnp.broadcast_to(strip_ref[h, i:i + 1, :], (GRID_W, B_LANES))
                toeplitz = pltpu.roll(rows, 0, axis=1, stride=1, stride_axis=0)
                tiles[i] = jnp.where(col_ok, toeplitz, NEG)
            return tiles[i]

        for p in range(len(B_PATTERN_ROWBLOCKS)):
            for qr in range(B_QR):
                for c in range(B_KR // 2):
                    ok0, ok1 = B_ROW_OK[p, qr, 2 * c], B_ROW_OK[p, qr, 2 * c + 1]
                    if ok0 or ok1:
                        t = tile(int(B_ROW_START[p, qr]) + 2 * c + B_STRIP_LO)
                        if not ok1:
                            t = jnp.where(first_row, t, NEG)
                        if not ok0:
                            t = jnp.where(first_row, NEG, t)
                    else:
                        t = jnp.full((GRID_W, B_LANES), NEG, F32)
                    bias_ref[p, h, qr * GRID_W:(qr + 1) * GRID_W, c * B_LANES:(c + 1) * B_LANES] = t


def _attn_b_kernel(q_ref, k0_ref, k1_ref, k2_ref, v0_ref, v1_ref, v2_ref, strip_ref, o_ref,
                   bias_ref):
    r = pl.program_id(1)

    @pl.when((pl.program_id(0) == 0) & (r == 0))
    def _():
        _attn_b_build_bias(strip_ref, bias_ref)

    pattern = jnp.where(r == 0, 0, jnp.where(r == B_NRB - 1, 2, 1))
    k_refs = (k0_ref, k1_ref, k2_ref)
    v_refs = (v0_ref, v1_ref, v2_ref)
    ones = jnp.ones((B_NK, HEAD_DIM), BF16)
    for h in range(B_HEADS):
        hd = slice(h * HEAD_DIM, (h + 1) * HEAD_DIM)
        q = q_ref[:, :, hd].reshape(B_NQ, HEAD_DIM)
        k = jnp.concatenate([ref[:, :, hd] for ref in k_refs], axis=0).reshape(B_NK, HEAD_DIM)
        v = jnp.concatenate([ref[:, :, hd] for ref in v_refs], axis=0).reshape(B_NK, HEAD_DIM)
        s = lax.dot_general(q, k, (((1,), (1,)), ((), ())),
                            preferred_element_type=F32) + bias_ref[pattern, h]
        m = jnp.max(s, axis=-1, keepdims=True)
        p = jnp.exp(s - m).astype(BF16)
        o = jnp.dot(p, jnp.concatenate([v, ones], axis=1), preferred_element_type=F32)
        o = o[:, :HEAD_DIM] / o[:, HEAD_DIM:]
        o_ref[:, :, hd] = o.reshape(B_QR, GRID_W, HEAD_DIM).astype(BF16)


def _attn_b_strips(rpb):
    rows = B_NSTRIP + 1
    table = jnp.pad(rpb.astype(F32), ((0, 0), (B_STRIP_LO, rows - B_STRIP_LO - rpb.shape[1]), (0, 0)))
    first, second = table[:, :-1], table[:, 1:]
    gap = jnp.zeros(first.shape[:2] + (GRID_W - 2 * WIN_W + 1,), F32)
    strips = jnp.concatenate(
        [first[..., WIN_W - 1:], gap, second, gap, first[..., :WIN_W - 1]], axis=-1)
    assert strips.shape == (B_HEADS, B_NSTRIP, B_LANES)
    return strips


def _attn_b(qk3, vals3, strips, batch):
    rows = qk3.shape[0]

    def kv_map(col, t):
        return lambda b, r: (b * B_NRB + jnp.clip(r - 1, 0, B_KS_MAX) + t, 0, col)

    blk = lambda m: pl.BlockSpec((B_QR, GRID_W, B_DIM), m)
    return pl.pallas_call(
        _attn_b_kernel,
        out_shape=jax.ShapeDtypeStruct((rows, GRID_W, B_DIM), BF16),
        grid=(batch, B_NRB),
        in_specs=[
            blk(lambda b, r: (b * B_NRB + r, 0, 0)),
            blk(kv_map(1, 0)), blk(kv_map(1, 1)), blk(kv_map(1, 2)),
            blk(kv_map(0, 0)), blk(kv_map(0, 1)), blk(kv_map(0, 2)),
            _resident((B_HEADS, B_NSTRIP, B_LANES)),
        ],
        out_specs=blk(lambda b, r: (b * B_NRB + r, 0, 0)),
        scratch_shapes=[pltpu.VMEM((len(B_PATTERN_ROWBLOCKS), B_HEADS, B_NQ, B_NK), F32)],
        compiler_params=_params("arbitrary", "arbitrary"),
        name="attn_b",
    )(qk3, qk3, qk3, qk3, vals3, vals3, vals3, strips)


MG_TM = 512


def _merge_kernel(x_ref, oa_ref, ob_ref, ga_ref, gb_ref, wa_ref, wb_ref, wo_ref, g_ref,
                  x1_ref, h2_ref):
    a = jnp.dot(oa_ref[...], wa_ref[...], preferred_element_type=F32)
    b = jnp.dot(ob_ref[...], wb_ref[...], preferred_element_type=F32)
    merged = ga_ref[...].astype(F32) * a + gb_ref[...].astype(F32) * b
    y = jnp.dot(merged.astype(BF16), wo_ref[...], preferred_element_type=F32)
    x1 = x_ref[...] + y
    x1_ref[...] = x1
    h2_ref[...] = _rmsnorm(x1, g_ref[...]).astype(BF16)


def _merge(x2, out_a, out_b, gates, wa, wb, wo, norm_g):
    n = x2.shape[0]
    row_block = lambda width, col=0: pl.BlockSpec((MG_TM, width), lambda i: (i, col))
    return pl.pallas_call(
        _merge_kernel,
        out_shape=(jax.ShapeDtypeStruct((n, D_MODEL), F32),
                   jax.ShapeDtypeStruct((n, D_MODEL), BF16)),
        grid=(n // MG_TM,),
        in_specs=[
            row_block(D_MODEL), row_block(A_Q_DIM), row_block(B_DIM),
            row_block(D_MODEL, 0), row_block(D_MODEL, 1),
            _resident((A_Q_DIM, D_MODEL)), _resident((B_DIM, D_MODEL)),
            _resident((D_MODEL, D_MODEL)), _resident((1, D_MODEL)),
        ],
        out_specs=(row_block(D_MODEL), row_block(D_MODEL)),
        compiler_params=_params("arbitrary"),
        name="merge",
    )(x2, out_a, out_b, gates, gates, wa, wb, wo, norm_g)


UP_TM = 1024
UP_NJ = 4
UP_TN = D_FF // UP_NJ
UP_LANES = SUB // 2
UP_NSUB = UP_TN // UP_LANES
HALO = 16
UP_ROWS = UP_TM + 2 * HALO
UP_SPLIT_TAIL = 2


def _ffn_up_kernel(hp_ref, hm_ref, hn_ref, wg_ref, wv_ref, cwg_ref, cwv_ref, cbg_ref, cbv_ref,
                   o_ref, lhs_ref, u_ref):
    i = pl.program_id(0)
    tiles_per_seq = SEQ // UP_TM

    @pl.when(pl.program_id(1) == 0)
    def _():
        first = (i % tiles_per_seq) == 0
        last = (i % tiles_per_seq) == tiles_per_seq - 1
        lhs_ref[0:HALO, :] = jnp.where(first, jnp.zeros_like(hp_ref[...]), hp_ref[...])
        lhs_ref[HALO:HALO + UP_TM, :] = hm_ref[...]
        lhs_ref[HALO + UP_TM:, :] = jnp.where(last, jnp.zeros_like(hn_ref[...]), hn_ref[...])

    def conv(slot, rows, lanes, cw_ref, cb_ref, cols, scale):
        below = u_ref[slot, HALO - 1:HALO - 1 + rows, lanes]
        mid = u_ref[slot, HALO:HALO + rows, lanes]
        above = u_ref[slot, HALO + 1:HALO + 1 + rows, lanes]
        c0, c1, c2 = (scale * cw_ref[t:t + 1, cols] for t in range(3))
        y = mid * c1 + scale * cb_ref[:, cols]
        y = y + below * c0
        return y + above * c2

    def unit(k, slot, row0, rows):
        cols = slice(k * UP_LANES, (k + 1) * UP_LANES)
        w = jnp.concatenate([wg_ref[:, cols], wv_ref[:, cols]], axis=1)
        u_ref[slot, 0:rows + 2 * HALO, :] = jnp.dot(
            lhs_ref[row0:row0 + rows + 2 * HALO, :], w, preferred_element_type=F32)
        half_gate = conv(slot, rows, slice(0, UP_LANES), cwg_ref, cbg_ref, cols, 0.5)
        val = conv(slot, rows, slice(UP_LANES, SUB), cwv_ref, cbv_ref, cols, 1.0)
        silu = half_gate * jnp.tanh(half_gate) + half_gate
        o_ref[row0:row0 + rows, cols] = (silu * val).astype(BF16)

    units = []
    for k in range(UP_NSUB):
        if k < UP_NSUB - UP_SPLIT_TAIL:
            units.append((k, 0, UP_TM))
        else:
            units += [(k, 0, UP_TM // 2), (k, UP_TM // 2, UP_TM // 2)]
    for slot, (k, row0, rows) in enumerate(units):
        unit(k, slot % 2, row0, rows)


def _ffn_up(h2, w_up, conv_w, conv_b):
    n = h2.shape[0]
    halo_blocks = UP_TM // HALO
    last_halo = n // HALO - 1
    return pl.pallas_call(
        _ffn_up_kernel,
        out_shape=jax.ShapeDtypeStruct((n, D_FF), BF16),
        grid=(n // UP_TM, UP_NJ),
        in_specs=[
            pl.BlockSpec((HALO, D_MODEL), lambda i, j: (jnp.maximum(i * halo_blocks - 1, 0), 0)),
            pl.BlockSpec((UP_TM, D_MODEL), lambda i, j: (i, 0)),
            pl.BlockSpec((HALO, D_MODEL),
                         lambda i, j: (jnp.minimum((i + 1) * halo_blocks, last_halo), 0)),
            pl.BlockSpec((D_MODEL, UP_TN), lambda i, j: (0, j)),
            pl.BlockSpec((D_MODEL, UP_TN), lambda i, j: (0, UP_NJ + j)),
            pl.BlockSpec((3, UP_TN), lambda i, j: (0, j)),
            pl.BlockSpec((3, UP_TN), lambda i, j: (0, UP_NJ + j)),
            pl.BlockSpec((1, UP_TN), lambda i, j: (0, j)),
            pl.BlockSpec((1, UP_TN), lambda i, j: (0, UP_NJ + j)),
        ],
        out_specs=pl.BlockSpec((UP_TM, UP_TN), lambda i, j: (i, j)),
        scratch_shapes=[pltpu.VMEM((UP_ROWS, D_MODEL), BF16),
                        pltpu.VMEM((2, UP_ROWS, SUB), F32)],
        compiler_params=_params("arbitrary", "arbitrary"),
        name="ffn_up",
    )(h2, h2, h2, w_up, w_up, conv_w, conv_w, conv_b, conv_b)


DN_TM = 512


def _ffn_down_kernel(a_ref, w_ref, x_ref, o_ref):
    lhs = a_ref[...]
    for k in range(D_MODEL // SUB):
        cols = slice(k * SUB, (k + 1) * SUB)
        o_ref[:, cols] = x_ref[:, cols] + jnp.dot(lhs, w_ref[:, cols], preferred_element_type=F32)


def _ffn_down(act, w_down, x1):
    n = act.shape[0]
    row_block = lambda width: pl.BlockSpec((DN_TM, width), lambda i: (i, 0))
    return pl.pallas_call(
        _ffn_down_kernel,
        out_shape=jax.ShapeDtypeStruct((n, D_MODEL), F32),
        grid=(n // DN_TM,),
        in_specs=[row_block(D_FF), _resident((D_FF, D_MODEL)), row_block(D_MODEL)],
        out_specs=row_block(D_MODEL),
        compiler_params=_params("arbitrary"),
        name="ffn_down",
    )(act, w_down, x1)


def _rope_tables():
    half = HEAD_DIM // 2
    pos = jnp.arange(SEQ, dtype=F32)
    inv_freq = ROPE_THETA ** (-jnp.arange(half, dtype=F32) * (2.0 / HEAD_DIM))
    ang = pos[:, None] * inv_freq[None, :]
    cos, sin = jnp.cos(ang), jnp.sin(ang)
    return jnp.concatenate([cos, cos], axis=-1), jnp.concatenate([sin, sin], axis=-1)


def _paired_gain(gain):
    lo, hi = gain[:HALF_LANES], gain[HALF_LANES:]
    return jnp.stack([jnp.concatenate([lo, lo]), jnp.concatenate([hi, hi])])


IN_COL_VA = ROPE_DIM
IN_COL_QB = IN_COL_VA + A_KV_DIM
IN_COL_VB = IN_COL_QB + NORM_DIM
IN_COL_GATE = IN_COL_VB + B_DIM
IN_HEAD_OUTPUTS = [ROPE_COLUMNS, [(IN_COL_QB, IN_COL_VB)], [(IN_COL_VB, IN_COL_GATE)],
                   [(IN_COL_VA, IN_COL_QB)]]


def kernel(x, norm_mix, w_in, a_q_norm, a_k_norm, a_sink, b_q_norm, b_k_norm, b_rpb,
           w_branch_a, w_branch_b, w_out, norm_ffn, w_up, conv_w, conv_b, w_down):
    batch, seq, d_model = x.shape
    assert (seq, d_model) == (SEQ, D_MODEL)
    n = batch * seq
    cos, sin = _rope_tables()
    scale = 1.0 / math.sqrt(HEAD_DIM)
    x2 = x.reshape(n, d_model)
    for l in range(norm_mix.shape[0]):
        w_gate = w_in[l][:, IN_COL_GATE:].astype(BF16)
        gates, h, w_rope, w_norm, w_vb, w_va, w_up_bf = _in_gate(
            x2, norm_mix[l][None], w_gate, [(w_in[l], IN_HEAD_OUTPUTS), (w_up[l], None)])
        gains = (_paired_gain(a_q_norm[l] * scale), _paired_gain(a_k_norm[l]),
                 b_q_norm[l][None] * scale, b_k_norm[l][None])
        qk_a, qk_b, vals = _in_heads(h, (w_rope, w_norm, w_vb, w_va), cos, sin, gains, [])
        out_a, wa_bf, wb_bf, w_out_bf, w_down_bf = _attn_a(
            qk_a, vals, a_sink[l], batch, [w_branch_a[l], w_branch_b[l], w_out[l], w_down[l]])
        out_b = _attn_b(qk_b.reshape(n // GRID_W, GRID_W, NORM_DIM),
                        vals.reshape(n // GRID_W, GRID_W, PLAIN_DIM), _attn_b_strips(b_rpb[l]), batch)
        x1, h2 = _merge(x2, out_a, out_b.reshape(n, B_DIM), gates, wa_bf, wb_bf, w_out_bf,
                        norm_ffn[l][None])
        act = _ffn_up(h2, w_up_bf, conv_w[l], conv_b[l][None])
        x2 = _ffn_down(act, w_down_bf, x1)
    return x2.reshape(batch, seq, d_model)
```

```python
import math

import numpy as np
import jax
import jax.numpy as jnp
from jax import lax
from jax.experimental import pallas as pl
from jax.experimental.pallas import tpu as pltpu

D_MODEL = 2048
SEQ = 4096
HEAD_DIM = 128
A_Q_HEADS = 8
A_KV_HEADS = 2
A_GROUP = A_Q_HEADS // A_KV_HEADS
WINDOW = 128
B_HEADS = 8
GRID_W = 64
WIN_H = 8
WIN_W = 16
D_FF = 5632
ROPE_THETA = 10000.0
EPS = 1e-6
NEG = -1e30
LOG2E = math.log2(math.e)

A_Q_DIM = A_Q_HEADS * HEAD_DIM
A_KV_DIM = A_KV_HEADS * HEAD_DIM
B_DIM = B_HEADS * HEAD_DIM

VMEM_LIMIT_BYTES = 56 * 1024 * 1024
MXU_COLS = 256
SUB = MXU_COLS
HEADS_PER_SUB = SUB // HEAD_DIM

BF16 = jnp.bfloat16
F32 = jnp.float32


def _params(*semantics):
    return pltpu.CompilerParams(dimension_semantics=semantics, vmem_limit_bytes=VMEM_LIMIT_BYTES)


def _resident(shape):
    return pl.BlockSpec(shape, lambda *_: (0,) * len(shape), pipeline_mode=pl.Buffered(1))


def _rmsnorm(x, gain):
    ms = jnp.mean(x * x, axis=-1, keepdims=True)
    return x * lax.rsqrt(ms + EPS) * gain


class _CastJob:
    def __init__(self, src, steps, outputs=None, step_of=lambda i: i):
        rows, cols = src.shape
        self.src = src
        self.outputs = outputs or [[(0, cols)]]
        self.slab = rows // steps
        assert self.slab * steps == rows
        self.span = max(hi for pieces in self.outputs for _, hi in pieces)
        self.widths = [sum(hi - lo for lo, hi in pieces) for pieces in self.outputs]
        self.index_map = lambda *idx: (step_of(*idx), 0)

    def in_spec(self):
        return pl.BlockSpec((self.slab, self.span), self.index_map)

    def out_specs(self):
        return [pl.BlockSpec((self.slab, width), self.index_map) for width in self.widths]

    def out_shapes(self):
        return [jax.ShapeDtypeStruct((self.src.shape[0], width), BF16) for width in self.widths]

    def run(self, src_ref, out_refs):
        for pieces, out_ref in zip(self.outputs, out_refs):
            parts = [src_ref[:, lo:hi] for lo, hi in pieces]
            value = parts[0] if len(parts) == 1 else jnp.concatenate(parts, axis=1)
            out_ref[...] = value.astype(BF16)


def _with_casts(body, jobs, n_in, n_out):
    n_cast_out = sum(len(j.outputs) for j in jobs)

    def wrapped(*refs):
        ins = refs[:n_in]
        srcs = refs[n_in:n_in + len(jobs)]
        outs = refs[n_in + len(jobs):n_in + len(jobs) + n_out]
        cast_outs = list(refs[n_in + len(jobs) + n_out:n_in + len(jobs) + n_out + n_cast_out])
        scratch = refs[n_in + len(jobs) + n_out + n_cast_out:]
        for job, src_ref in zip(jobs, srcs):
            job.run(src_ref, [cast_outs.pop(0) for _ in job.outputs])
        body(*ins, *outs, *scratch)

    return wrapped


def _sigmoid(x):
    return 0.5 * jnp.tanh(0.5 * x) + 0.5


IN_TM = 1024
GATE_TM = 512
GATE_DIM = 2 * D_MODEL
ROPE_DIM = A_Q_DIM + A_KV_DIM
NORM_DIM = 2 * B_DIM
PLAIN_DIM = B_DIM + A_KV_DIM


def _in_gate_kernel(x_ref, g_ref, w_ref, o_ref, h_ref):
    h_ref[...] = _rmsnorm(x_ref[...], g_ref[...]).astype(BF16)
    lhs = h_ref[...]
    for k in range(GATE_DIM // SUB):
        cols = slice(k * SUB, (k + 1) * SUB)
        acc = jnp.dot(lhs, w_ref[:, cols], preferred_element_type=F32)
        o_ref[:, cols] = _sigmoid(acc).astype(BF16)


def _in_gate(x2, norm_g, w_gate, casts):
    n = x2.shape[0]
    steps = n // GATE_TM
    jobs = [_CastJob(src, steps, ranges) for src, ranges in casts]
    row_block = lambda width: pl.BlockSpec((GATE_TM, width), lambda i: (i, 0))
    return pl.pallas_call(
        _with_casts(_in_gate_kernel, jobs, n_in=3, n_out=2),
        out_shape=[jax.ShapeDtypeStruct((n, GATE_DIM), BF16),
                   jax.ShapeDtypeStruct((n, D_MODEL), BF16)]
                  + [s for j in jobs for s in j.out_shapes()],
        grid=(steps,),
        in_specs=[row_block(D_MODEL), _resident((1, D_MODEL)), _resident((D_MODEL, GATE_DIM))]
                 + [j.in_spec() for j in jobs],
        out_specs=[row_block(GATE_DIM), row_block(D_MODEL)]
                  + [s for j in jobs for s in j.out_specs()],
        compiler_params=_params("arbitrary"),
        name="in_gate",
    )(x2, norm_g, w_gate, *[j.src for j in jobs])


def _store_heads(o_ref, k, acc, fn):
    for t in range(HEADS_PER_SUB):
        col = k * SUB + t * HEAD_DIM
        o_ref[:, col:col + HEAD_DIM] = fn(acc[:, t * HEAD_DIM:(t + 1) * HEAD_DIM]).astype(BF16)


def _pair_columns(head_a, head_b):
    half = HEAD_DIM // 2
    a, b = head_a * HEAD_DIM, head_b * HEAD_DIM
    return [(a, a + half), (b, b + half), (a + half, a + HEAD_DIM), (b + half, b + HEAD_DIM)]


ROPE_PAIRS = [(p, p + A_GROUP) for p in range(A_GROUP)] + [(A_Q_HEADS, A_Q_HEADS + 1)]
assert A_KV_HEADS == 2 and len(ROPE_PAIRS) * SUB == ROPE_DIM
ROPE_COLUMNS = [piece for pair in ROPE_PAIRS for piece in _pair_columns(*pair)]
HALF_LANES = HEAD_DIM // 2


def _in_rope_kernel(h_ref, w_ref, cos_ref, sin_ref, gq_ref, gk_ref, o_ref):
    lhs = h_ref[...]
    first_head = lax.broadcasted_iota(jnp.int32, (IN_TM, HEAD_DIM), 1) < HALF_LANES
    for k in range(ROPE_DIM // SUB):
        acc = jnp.dot(lhs, w_ref[:, k * SUB:(k + 1) * SUB], preferred_element_type=F32)
        gain_ref = gq_ref if k * SUB < A_Q_DIM else gk_ref
        lo, hi = acc[:, :HEAD_DIM], acc[:, HEAD_DIM:]
        sq = lo * lo + hi * hi
        ss_a = jnp.sum(jnp.where(first_head, sq, 0.0), axis=-1, keepdims=True)
        ss_b = jnp.sum(jnp.where(first_head, 0.0, sq), axis=-1, keepdims=True)
        inv = jnp.where(first_head, lax.rsqrt(ss_a * (1.0 / HEAD_DIM) + EPS),
                        lax.rsqrt(ss_b * (1.0 / HEAD_DIM) + EPS))
        lo = lo * inv * gain_ref[0:1, :]
        hi = hi * inv * gain_ref[1:2, :]
        cos, sin = cos_ref[...], sin_ref[...]
        o_ref[:, k * SUB:k * SUB + HEAD_DIM] = (lo * cos - hi * sin).astype(BF16)
        o_ref[:, k * SUB + HEAD_DIM:(k + 1) * SUB] = (hi * cos + lo * sin).astype(BF16)


def _in_norm_kernel(h_ref, w_ref, gq_ref, gk_ref, o_ref):
    lhs = h_ref[...]
    for k in range(NORM_DIM // SUB):
        acc = jnp.dot(lhs, w_ref[:, k * SUB:(k + 1) * SUB], preferred_element_type=F32)
        gain_ref = gq_ref if k * SUB < B_DIM else gk_ref
        _store_heads(o_ref, k, acc, lambda a: _rmsnorm(a, gain_ref[...]))


def _in_plain_kernel(h_ref, wvb_ref, wva_ref, o_ref):
    lhs = h_ref[...]
    for k in range(PLAIN_DIM // SUB):
        cols = slice(k * SUB, (k + 1) * SUB)
        w = wvb_ref[:, cols] if k * SUB < B_DIM else wva_ref[:, k * SUB - B_DIM:(k + 1) * SUB - B_DIM]
        o_ref[:, cols] = jnp.dot(lhs, w, preferred_element_type=F32).astype(BF16)


def _in_heads_kernel(h_ref, w_rope_ref, w_norm_ref, w_vb_ref, w_va_ref, cos_ref, sin_ref,
                     gqa_ref, gka_ref, gqb_ref, gkb_ref, rope_ref, norm_ref, plain_ref):
    _in_rope_kernel(h_ref, w_rope_ref, cos_ref, sin_ref, gqa_ref, gka_ref, rope_ref)
    _in_norm_kernel(h_ref, w_norm_ref, gqb_ref, gkb_ref, norm_ref)
    _in_plain_kernel(h_ref, w_vb_ref, w_va_ref, plain_ref)


def _in_heads(h, weights, cos, sin, gains, casts):
    n = h.shape[0]
    steps = n // IN_TM
    seq_tiles = SEQ // IN_TM
    jobs = [_CastJob(src, steps) for src in casts]
    row_block = lambda width: pl.BlockSpec((IN_TM, width), lambda i: (i, 0))
    pos_spec = pl.BlockSpec((IN_TM, HEAD_DIM), lambda i: (i % seq_tiles, 0))
    widths = (ROPE_DIM, NORM_DIM, PLAIN_DIM)
    return pl.pallas_call(
        _with_casts(_in_heads_kernel, jobs, n_in=11, n_out=3),
        out_shape=[jax.ShapeDtypeStruct((n, width), BF16) for width in widths]
                  + [s for j in jobs for s in j.out_shapes()],
        grid=(steps,),
        in_specs=([row_block(D_MODEL)] + [_resident(w.shape) for w in weights]
                  + [pos_spec, pos_spec] + [_resident(g.shape) for g in gains]
                  + [j.in_spec() for j in jobs]),
        out_specs=[row_block(width) for width in widths] + [s for j in jobs for s in j.out_specs()],
        compiler_params=_params("arbitrary"),
        name="in_heads",
    )(h, *weights, cos, sin, *gains, *[j.src for j in jobs])


A_TQ = 512
A_BLK = WINDOW
A_NB = A_TQ // A_BLK
A_CHUNKS = SEQ // A_TQ


def _attn_a_kernel(sink_ref, q_ref, kp_ref, km_ref, kn_ref, vp_ref, vm_ref, vn_ref, o_ref):
    c = pl.program_id(1)
    k_all = jnp.concatenate([kp_ref[...], km_ref[...], kn_ref[...]], axis=0)
    v_all = jnp.concatenate([vp_ref[...], vm_ref[...], vn_ref[...]], axis=0)
    qq = lax.broadcasted_iota(jnp.int32, (A_BLK, 3 * A_BLK), 0)
    kk = lax.broadcasted_iota(jnp.int32, (A_BLK, 3 * A_BLK), 1)
    d = kk - qq
    band = (d >= 0) & (d <= 2 * WINDOW)
    ones = jnp.ones((3 * A_BLK, HEAD_DIM), BF16)
    lane = lax.broadcasted_iota(jnp.int32, (A_BLK, SUB), 1) % HEAD_DIM
    own_lanes = (lane < HALF_LANES, lane >= HALF_LANES)
    for n in range(A_NB):
        valid = band
        if n == 0:
            valid = valid & ((kk >= A_BLK) | (c > 0))
        if n == A_NB - 1:
            valid = valid & ((kk < 2 * A_BLK) | (c < A_CHUNKS - 1))
        k = k_all[n * A_BLK:(n + 3) * A_BLK, :]
        q_pairs = [q_ref[n * A_BLK:(n + 1) * A_BLK, p * SUB:(p + 1) * SUB] for p in range(A_GROUP)]
        qs = jnp.concatenate([jnp.where(own_lanes[h], q, jnp.zeros_like(q))
                              for h in range(A_KV_HEADS) for q in q_pairs], axis=0)
        s_all = lax.dot_general(qs, k, (((1,), (1,)), ((), ())),
                                preferred_element_type=F32)
        for h in range(A_KV_HEADS):
            v = v_all[n * A_BLK:(n + 3) * A_BLK, h * HEAD_DIM:(h + 1) * HEAD_DIM]
            heads = [h * A_GROUP + g for g in range(A_GROUP)]
            s = s_all[h * A_GROUP * A_BLK:(h + 1) * A_GROUP * A_BLK]
            ps, sink_terms = [], []
            for g, t in enumerate(heads):
                sg = jnp.where(valid, s[g * A_BLK:(g + 1) * A_BLK], NEG)
                m = jnp.maximum(jnp.max(sg, axis=-1, keepdims=True), sink_ref[t])
                ps.append(jnp.exp2(sg - m).astype(BF16))
                sink_terms.append(jnp.exp2(sink_ref[t] - m))
            o = jnp.dot(jnp.concatenate(ps, axis=0), jnp.concatenate([v, ones], axis=1),
                        preferred_element_type=F32)
            for g, t in enumerate(heads):
                og = o[g * A_BLK:(g + 1) * A_BLK]
                og = og[:, :HEAD_DIM] / (og[:, HEAD_DIM:] + sink_terms[g])
                o_ref[n * A_BLK:(n + 1) * A_BLK, t * HEAD_DIM:(t + 1) * HEAD_DIM] = og.astype(BF16)


def _attn_a(qk, vals, sink, batch, casts):
    n = qk.shape[0]
    blk_per_seq = SEQ // A_BLK
    k_col = A_Q_DIM // A_KV_DIM
    v_col = B_DIM // A_KV_DIM
    jobs = [_CastJob(src, batch * A_CHUNKS, step_of=lambda b, c, *_: b * A_CHUNKS + c)
            for src in casts]

    def prev_map(col):
        return lambda b, c, *_: (b * blk_per_seq + jnp.maximum(c * A_NB - 1, 0), col)

    def main_map(col):
        return lambda b, c, *_: (b * A_CHUNKS + c, col)

    def next_map(col):
        return lambda b, c, *_: (b * blk_per_seq + jnp.minimum(c * A_NB + A_NB, blk_per_seq - 1), col)

    halo = lambda m: pl.BlockSpec((A_BLK, A_KV_DIM), m)
    main = lambda m: pl.BlockSpec((A_TQ, A_KV_DIM), m)
    return pl.pallas_call(
        _with_casts(_attn_a_kernel, jobs, n_in=8, n_out=1),
        out_shape=[jax.ShapeDtypeStruct((n, A_Q_DIM), BF16)] + [s for j in jobs for s in j.out_shapes()],
        grid_spec=pltpu.PrefetchScalarGridSpec(
            num_scalar_prefetch=1,
            grid=(batch, A_CHUNKS),
            in_specs=[
                pl.BlockSpec((A_TQ, A_Q_DIM), main_map(0)),
                halo(prev_map(k_col)), main(main_map(k_col)), halo(next_map(k_col)),
                halo(prev_map(v_col)), main(main_map(v_col)), halo(next_map(v_col)),
            ] + [j.in_spec() for j in jobs],
            out_specs=[pl.BlockSpec((A_TQ, A_Q_DIM), main_map(0))]
                      + [s for j in jobs for s in j.out_specs()],
        ),
        compiler_params=_params("arbitrary", "arbitrary"),
        name="attn_a",
    )(sink, qk, qk, qk, qk, vals, vals, vals, *[j.src for j in jobs])


GRID_ROWS = SEQ // GRID_W
B_QR = 4
B_KR = B_QR + WIN_H
B_NRB = GRID_ROWS // B_QR
B_NQ = B_QR * GRID_W
B_NK = B_KR * GRID_W
B_KBLOCKS = B_KR // B_QR
B_KS_MAX = B_NRB - B_KBLOCKS
B_STEP_BLOCKS = 2
B_STEP_KBLOCKS = B_KBLOCKS + B_STEP_BLOCKS - 1
B_PATTERN_ROWBLOCKS = (0, 1, B_NRB - 1)


def _attn_b_row_windows():
    starts, row_ok = [], []
    for rb in B_PATTERN_ROWBLOCKS:
        ks = int(np.clip(rb - 1, 0, B_KS_MAX))
        qr = rb * B_QR + np.arange(B_QR)
        kr = ks * B_QR + np.arange(B_KR)
        rs = np.clip(qr - WIN_H // 2, 0, GRID_ROWS - WIN_H)
        row_ok.append((kr[None, :] >= rs[:, None]) & (kr[None, :] < rs[:, None] + WIN_H))
        starts.append(kr[0] - qr + WIN_H - 1)
    return np.stack(starts), np.stack(row_ok)


B_ROW_START, B_ROW_OK = _attn_b_row_windows()
B_STRIP_LO = max(0, -int(B_ROW_START.min()))
B_NSTRIP = int(B_ROW_START.max()) + B_KR - 1 + B_STRIP_LO
B_LANES = 2 * GRID_W


def _attn_b_build_bias(strip_ref, bias_ref):
    qc = lax.broadcasted_iota(jnp.int32, (GRID_W, B_LANES), 0)
    lane = lax.broadcasted_iota(jnp.int32, (GRID_W, B_LANES), 1)
    kc = lane % GRID_W
    cs = jnp.clip(qc - WIN_W // 2, 0, GRID_W - WIN_W)
    col_ok = (kc >= cs) & (kc < cs + WIN_W)
    first_row = lane < GRID_W
    for h in range(B_HEADS):
        tiles = {}

        def tile(i):
            if i not in tiles:
                rows = jnp.broadcast_to(strip_ref[h, i:i + 1, :], (GRID_W, B_LANES))
                toeplitz = pltpu.roll(rows, 0, axis=1, stride=1, stride_axis=0)
                tiles[i] = jnp.where(col_ok, toeplitz, NEG)
            return tiles[i]

        for p in range(len(B_PATTERN_ROWBLOCKS)):
            for qr in range(B_QR):
                for c in range(B_KR // 2):
                    ok0, ok1 = B_ROW_OK[p, qr, 2 * c], B_ROW_OK[p, qr, 2 * c + 1]
                    if ok0 or ok1:
                        t = tile(int(B_ROW_START[p, qr]) + 2 * c + B_STRIP_LO)
                        if not ok1:
                            t = jnp.where(first_row, t, NEG)
                        if not ok0:
                            t = jnp.where(first_row, NEG, t)
                    else:
                        t = jnp.full((GRID_W, B_LANES), NEG, F32)
                    bias_ref[p, h, qr * GRID_W:(qr + 1) * GRID_W, c * B_LANES:(c + 1) * B_LANES] = t


def _attn_b_key_base(step):
    return jnp.clip(step * B_STEP_BLOCKS - 1, 0, B_NRB - B_STEP_KBLOCKS)


def _attn_b_kernel(q_ref, *refs):
    k_refs = refs[:B_STEP_KBLOCKS]
    v_refs = refs[B_STEP_KBLOCKS:2 * B_STEP_KBLOCKS]
    strip_ref, o_ref, bias_ref, k_buf, v_buf = refs[2 * B_STEP_KBLOCKS:]
    step = pl.program_id(1)

    @pl.when((pl.program_id(0) == 0) & (step == 0))
    def _():
        _attn_b_build_bias(strip_ref, bias_ref)

    for t in range(B_STEP_KBLOCKS):
        k_buf[t * B_QR:(t + 1) * B_QR] = k_refs[t][...]
        v_buf[t * B_QR:(t + 1) * B_QR] = v_refs[t][...]

    ones = jnp.ones((B_NK, HEAD_DIM), BF16)
    for sb in range(B_STEP_BLOCKS):
        r = step * B_STEP_BLOCKS + sb
        pattern = jnp.where(r == 0, 0, jnp.where(r == B_NRB - 1, 2, 1))
        first_row = (jnp.clip(r - 1, 0, B_KS_MAX) - _attn_b_key_base(step)) * B_QR
        rows = slice(sb * B_QR, (sb + 1) * B_QR)
        for h in range(B_HEADS):
            hd = slice(h * HEAD_DIM, (h + 1) * HEAD_DIM)
            q = q_ref[rows, :, hd].reshape(B_NQ, HEAD_DIM)
            k = k_buf[pl.ds(first_row, B_KR), :, hd].reshape(B_NK, HEAD_DIM)
            v = v_buf[pl.ds(first_row, B_KR), :, hd].reshape(B_NK, HEAD_DIM)
            s = lax.dot_general(q, k, (((1,), (1,)), ((), ())),
                                preferred_element_type=F32) + bias_ref[pattern, h]
            m = jnp.max(s, axis=-1, keepdims=True)
            p = jnp.exp2(s - m).astype(BF16)
            o = jnp.dot(p, jnp.concatenate([v, ones], axis=1), preferred_element_type=F32)
            o = o[:, :HEAD_DIM] / o[:, HEAD_DIM:]
            o_ref[rows, :, hd] = o.reshape(B_QR, GRID_W, HEAD_DIM).astype(BF16)


def _attn_b_strips(rpb):
    rows = B_NSTRIP + 1
    table = jnp.pad(rpb.astype(F32) * LOG2E,
                    ((0, 0), (B_STRIP_LO, rows - B_STRIP_LO - rpb.shape[1]), (0, 0)))
    first, second = table[:, :-1], table[:, 1:]
    gap = jnp.zeros(first.shape[:2] + (GRID_W - 2 * WIN_W + 1,), F32)
    strips = jnp.concatenate(
        [first[..., WIN_W - 1:], gap, second, gap, first[..., :WIN_W - 1]], axis=-1)
    assert strips.shape == (B_HEADS, B_NSTRIP, B_LANES)
    return strips


def _attn_b(qk3, vals3, strips, batch):
    rows = qk3.shape[0]

    steps = B_NRB // B_STEP_BLOCKS

    def kv_spec(col, t):
        return pl.BlockSpec((B_QR, GRID_W, B_DIM),
                            lambda b, s: (b * B_NRB + _attn_b_key_base(s) + t, 0, col))

    q_spec = pl.BlockSpec((B_STEP_BLOCKS * B_QR, GRID_W, B_DIM), lambda b, s: (b * steps + s, 0, 0))
    window = (B_STEP_KBLOCKS * B_QR, GRID_W, B_DIM)
    return pl.pallas_call(
        _attn_b_kernel,
        out_shape=jax.ShapeDtypeStruct((rows, GRID_W, B_DIM), BF16),
        grid=(batch, steps),
        in_specs=([q_spec] + [kv_spec(1, t) for t in range(B_STEP_KBLOCKS)]
                  + [kv_spec(0, t) for t in range(B_STEP_KBLOCKS)]
                  + [_resident((B_HEADS, B_NSTRIP, B_LANES))]),
        out_specs=q_spec,
        scratch_shapes=[pltpu.VMEM((len(B_PATTERN_ROWBLOCKS), B_HEADS, B_NQ, B_NK), F32),
                        pltpu.VMEM(window, BF16), pltpu.VMEM(window, BF16)],
        compiler_params=_params("arbitrary", "arbitrary"),
        name="attn_b",
    )(qk3, *[qk3] * B_STEP_KBLOCKS, *[vals3] * B_STEP_KBLOCKS, strips)


MG_TM = 512


def _merge_kernel(x_ref, oa_ref, ob_ref, ga_ref, gb_ref, wa_ref, wb_ref, wo_ref, g_ref,
                  x1_ref, h2_ref):
    a = jnp.dot(oa_ref[...], wa_ref[...], preferred_element_type=F32)
    b = jnp.dot(ob_ref[...], wb_ref[...], preferred_element_type=F32)
    merged = ga_ref[...].astype(F32) * a + gb_ref[...].astype(F32) * b
    y = jnp.dot(merged.astype(BF16), wo_ref[...], preferred_element_type=F32)
    x1 = x_ref[...] + y
    x1_ref[...] = x1
    h2_ref[...] = _rmsnorm(x1, g_ref[...]).astype(BF16)


def _merge(x2, out_a, out_b, gates, wa, wb, wo, norm_g):
    n = x2.shape[0]
    row_block = lambda width, col=0: pl.BlockSpec((MG_TM, width), lambda i: (i, col))
    return pl.pallas_call(
        _merge_kernel,
        out_shape=(jax.ShapeDtypeStruct((n, D_MODEL), F32),
                   jax.ShapeDtypeStruct((n, D_MODEL), BF16)),
        grid=(n // MG_TM,),
        in_specs=[
            row_block(D_MODEL), row_block(A_Q_DIM), row_block(B_DIM),
            row_block(D_MODEL, 0), row_block(D_MODEL, 1),
            _resident((A_Q_DIM, D_MODEL)), _resident((B_DIM, D_MODEL)),
            _resident((D_MODEL, D_MODEL)), _resident((1, D_MODEL)),
        ],
        out_specs=(row_block(D_MODEL), row_block(D_MODEL)),
        compiler_params=_params("arbitrary"),
        name="merge",
    )(x2, out_a, out_b, gates, gates, wa, wb, wo, norm_g)


UP_TM = 1024
UP_NJ = 4
UP_TN = D_FF // UP_NJ
UP_LANES = SUB // 2
UP_NSUB = UP_TN // UP_LANES
HALO = 16
UP_ROWS = UP_TM + 2 * HALO
UP_SPLIT_TAIL = 2


def _ffn_up_kernel(hp_ref, hm_ref, hn_ref, wg_ref, wv_ref, cwg_ref, cwv_ref, cbg_ref, cbv_ref,
                   o_ref, lhs_ref, u_ref):
    i = pl.program_id(0)
    tiles_per_seq = SEQ // UP_TM

    @pl.when(pl.program_id(1) == 0)
    def _():
        first = (i % tiles_per_seq) == 0
        last = (i % tiles_per_seq) == tiles_per_seq - 1
        lhs_ref[0:HALO, :] = jnp.where(first, jnp.zeros_like(hp_ref[...]), hp_ref[...])
        lhs_ref[HALO:HALO + UP_TM, :] = hm_ref[...]
        lhs_ref[HALO + UP_TM:, :] = jnp.where(last, jnp.zeros_like(hn_ref[...]), hn_ref[...])

    def conv(slot, rows, lanes, cw_ref, cb_ref, cols, scale):
        below = u_ref[slot, HALO - 1:HALO - 1 + rows, lanes]
        mid = u_ref[slot, HALO:HALO + rows, lanes]
        above = u_ref[slot, HALO + 1:HALO + 1 + rows, lanes]
        c0, c1, c2 = (scale * cw_ref[t:t + 1, cols] for t in range(3))
        y = mid * c1 + scale * cb_ref[:, cols]
        y = y + below * c0
        return y + above * c2

    def unit(k, slot, row0, rows):
        cols = slice(k * UP_LANES, (k + 1) * UP_LANES)
        w = jnp.concatenate([wg_ref[:, cols], wv_ref[:, cols]], axis=1)
        u_ref[slot, 0:rows + 2 * HALO, :] = jnp.dot(
            lhs_ref[row0:row0 + rows + 2 * HALO, :], w, preferred_element_type=F32)
        half_gate = conv(slot, rows, slice(0, UP_LANES), cwg_ref, cbg_ref, cols, 0.5)
        val = conv(slot, rows, slice(UP_LANES, SUB), cwv_ref, cbv_ref, cols, 1.0)
        silu = half_gate * jnp.tanh(half_gate) + half_gate
        o_ref[row0:row0 + rows, cols] = (silu * val).astype(BF16)

    units = []
    for k in range(UP_NSUB):
        if k < UP_NSUB - UP_SPLIT_TAIL:
            units.append((k, 0, UP_TM))
        else:
            units += [(k, 0, UP_TM // 2), (k, UP_TM // 2, UP_TM // 2)]
    for slot, (k, row0, rows) in enumerate(units):
        unit(k, slot % 2, row0, rows)


def _ffn_up(h2, w_up, conv_w, conv_b):
    n = h2.shape[0]
    halo_blocks = UP_TM // HALO
    last_halo = n // HALO - 1
    return pl.pallas_call(
        _ffn_up_kernel,
        out_shape=jax.ShapeDtypeStruct((n, D_FF), BF16),
        grid=(n // UP_TM, UP_NJ),
        in_specs=[
            pl.BlockSpec((HALO, D_MODEL), lambda i, j: (jnp.maximum(i * halo_blocks - 1, 0), 0)),
            pl.BlockSpec((UP_TM, D_MODEL), lambda i, j: (i, 0)),
            pl.BlockSpec((HALO, D_MODEL),
                         lambda i, j: (jnp.minimum((i + 1) * halo_blocks, last_halo), 0)),
            pl.BlockSpec((D_MODEL, UP_TN), lambda i, j: (0, j)),
            pl.BlockSpec((D_MODEL, UP_TN), lambda i, j: (0, UP_NJ + j)),
            pl.BlockSpec((3, UP_TN), lambda i, j: (0, j)),
            pl.BlockSpec((3, UP_TN), lambda i, j: (0, UP_NJ + j)),
            pl.BlockSpec((1, UP_TN), lambda i, j: (0, j)),
            pl.BlockSpec((1, UP_TN), lambda i, j: (0, UP_NJ + j)),
        ],
        out_specs=pl.BlockSpec((UP_TM, UP_TN), lambda i, j: (i, j)),
        scratch_shapes=[pltpu.VMEM((UP_ROWS, D_MODEL), BF16),
                        pltpu.VMEM((2, UP_ROWS, SUB), F32)],
        compiler_params=_params("arbitrary", "arbitrary"),
        name="ffn_up",
    )(h2, h2, h2, w_up, w_up, conv_w, conv_w, conv_b, conv_b)


DN_TM = 512


def _ffn_down_kernel(a_ref, w_ref, x_ref, o_ref):
    lhs = a_ref[...]
    for k in range(D_MODEL // SUB):
        cols = slice(k * SUB, (k + 1) * SUB)
        o_ref[:, cols] = x_ref[:, cols] + jnp.dot(lhs, w_ref[:, cols], preferred_element_type=F32)


def _ffn_down(act, w_down, x1):
    n = act.shape[0]
    row_block = lambda width: pl.BlockSpec((DN_TM, width), lambda i: (i, 0))
    return pl.pallas_call(
        _ffn_down_kernel,
        out_shape=jax.ShapeDtypeStruct((n, D_MODEL), F32),
        grid=(n // DN_TM,),
        in_specs=[row_block(D_FF), _resident((D_FF, D_MODEL)), row_block(D_MODEL)],
        out_specs=row_block(D_MODEL),
        compiler_params=_params("arbitrary"),
        name="ffn_down",
    )(act, w_down, x1)


def _rope_tables():
    half = HEAD_DIM // 2
    pos = jnp.arange(SEQ, dtype=F32)
    inv_freq = ROPE_THETA ** (-jnp.arange(half, dtype=F32) * (2.0 / HEAD_DIM))
    ang = pos[:, None] * inv_freq[None, :]
    cos, sin = jnp.cos(ang), jnp.sin(ang)
    return jnp.concatenate([cos, cos], axis=-1), jnp.concatenate([sin, sin], axis=-1)


def _paired_gain(gain):
    lo, hi = gain[:HALF_LANES], gain[HALF_LANES:]
    return jnp.stack([jnp.concatenate([lo, lo]), jnp.concatenate([hi, hi])])


IN_COL_VA = ROPE_DIM
IN_COL_QB = IN_COL_VA + A_KV_DIM
IN_COL_VB = IN_COL_QB + NORM_DIM
IN_COL_GATE = IN_COL_VB + B_DIM
IN_HEAD_OUTPUTS = [ROPE_COLUMNS, [(IN_COL_QB, IN_COL_VB)], [(IN_COL_VB, IN_COL_GATE)],
                   [(IN_COL_VA, IN_COL_QB)]]


def kernel(x, norm_mix, w_in, a_q_norm, a_k_norm, a_sink, b_q_norm, b_k_norm, b_rpb,
           w_branch_a, w_branch_b, w_out, norm_ffn, w_up, conv_w, conv_b, w_down):
    batch, seq, d_model = x.shape
    assert (seq, d_model) == (SEQ, D_MODEL)
    n = batch * seq
    cos, sin = _rope_tables()
    scale = LOG2E / math.sqrt(HEAD_DIM)
    x2 = x.reshape(n, d_model)
    for l in range(norm_mix.shape[0]):
        w_gate = w_in[l][:, IN_COL_GATE:].astype(BF16)
        gates, h, w_rope, w_norm, w_vb, w_va, w_up_bf = _in_gate(
            x2, norm_mix[l][None], w_gate, [(w_in[l], IN_HEAD_OUTPUTS), (w_up[l], None)])
        gains = (_paired_gain(a_q_norm[l] * scale), _paired_gain(a_k_norm[l]),
                 b_q_norm[l][None] * scale, b_k_norm[l][None])
        qk_a, qk_b, vals = _in_heads(h, (w_rope, w_norm, w_vb, w_va), cos, sin, gains, [])
        out_a, wa_bf, wb_bf, w_out_bf, w_down_bf = _attn_a(
            qk_a, vals, a_sink[l] * LOG2E, batch,
            [w_branch_a[l], w_branch_b[l], w_out[l], w_down[l]])
        out_b = _attn_b(qk_b.reshape(n // GRID_W, GRID_W, NORM_DIM),
                        vals.reshape(n // GRID_W, GRID_W, PLAIN_DIM), _attn_b_strips(b_rpb[l]), batch)
        x1, h2 = _merge(x2, out_a, out_b.reshape(n, B_DIM), gates, wa_bf, wb_bf, w_out_bf,
                        norm_ffn[l][None])
        act = _ffn_up(h2, w_up_bf, conv_w[l], conv_b[l][None])
        x2 = _ffn_down(act, w_down_bf, x1)
    return x2.reshape(batch, seq, d_model)
```

```python
import math

import numpy as np
import jax
import jax.numpy as jnp
from jax import lax
from jax.experimental import pallas as pl
from jax.experimental.pallas import tpu as pltpu

D_MODEL = 2048
SEQ = 4096
HEAD_DIM = 128
A_Q_HEADS = 8
A_KV_HEADS = 2
A_GROUP = A_Q_HEADS // A_KV_HEADS
WINDOW = 128
B_HEADS = 8
GRID_W = 64
WIN_H = 8
WIN_W = 16
D_FF = 5632
ROPE_THETA = 10000.0
EPS = 1e-6
NEG = -1e30
LOG2E = math.log2(math.e)

A_Q_DIM = A_Q_HEADS * HEAD_DIM
A_KV_DIM = A_KV_HEADS * HEAD_DIM
B_DIM = B_HEADS * HEAD_DIM

VMEM_LIMIT_BYTES = 56 * 1024 * 1024
MXU_COLS = 256
SUB = MXU_COLS
HEADS_PER_SUB = SUB // HEAD_DIM

BF16 = jnp.bfloat16
F32 = jnp.float32


def _params(*semantics):
    return pltpu.CompilerParams(dimension_semantics=semantics, vmem_limit_bytes=VMEM_LIMIT_BYTES)


def _resident(shape):
    return pl.BlockSpec(shape, lambda *_: (0,) * len(shape), pipeline_mode=pl.Buffered(1))


def _rmsnorm(x, gain):
    ms = jnp.mean(x * x, axis=-1, keepdims=True)
    return x * lax.rsqrt(ms + EPS) * gain


class _CastJob:
    def __init__(self, src, steps, outputs=None, step_of=lambda i: i):
        rows, cols = src.shape
        self.src = src
        self.outputs = outputs or [[(0, cols)]]
        self.slab = rows // steps
        assert self.slab * steps == rows
        self.span = max(hi for pieces in self.outputs for _, hi in pieces)
        self.widths = [sum(hi - lo for lo, hi in pieces) for pieces in self.outputs]
        self.index_map = lambda *idx: (step_of(*idx), 0)

    def in_spec(self):
        return pl.BlockSpec((self.slab, self.span), self.index_map)

    def out_specs(self):
        return [pl.BlockSpec((self.slab, width), self.index_map) for width in self.widths]

    def out_shapes(self):
        return [jax.ShapeDtypeStruct((self.src.shape[0], width), BF16) for width in self.widths]

    def run(self, src_ref, out_refs):
        for pieces, out_ref in zip(self.outputs, out_refs):
            parts = [src_ref[:, lo:hi] for lo, hi in pieces]
            value = parts[0] if len(parts) == 1 else jnp.concatenate(parts, axis=1)
            out_ref[...] = value.astype(BF16)


def _with_casts(body, jobs, n_in, n_out):
    n_cast_out = sum(len(j.outputs) for j in jobs)

    def wrapped(*refs):
        ins = refs[:n_in]
        srcs = refs[n_in:n_in + len(jobs)]
        outs = refs[n_in + len(jobs):n_in + len(jobs) + n_out]
        cast_outs = list(refs[n_in + len(jobs) + n_out:n_in + len(jobs) + n_out + n_cast_out])
        scratch = refs[n_in + len(jobs) + n_out + n_cast_out:]
        for job, src_ref in zip(jobs, srcs):
            job.run(src_ref, [cast_outs.pop(0) for _ in job.outputs])
        body(*ins, *outs, *scratch)

    return wrapped


def _sigmoid(x):
    return 0.5 * jnp.tanh(0.5 * x) + 0.5


IN_TM = 1024
GATE_TM = 512
GATE_DIM = 2 * D_MODEL
ROPE_DIM = A_Q_DIM + A_KV_DIM
NORM_DIM = 2 * B_DIM
PLAIN_DIM = B_DIM + A_KV_DIM


def _in_gate_kernel(x_ref, g_ref, w_ref, o_ref, h_ref):
    h_ref[...] = _rmsnorm(x_ref[...], g_ref[...]).astype(BF16)
    lhs = h_ref[...]
    for k in range(GATE_DIM // SUB):
        cols = slice(k * SUB, (k + 1) * SUB)
        acc = jnp.dot(lhs, w_ref[:, cols], preferred_element_type=F32)
        o_ref[:, cols] = _sigmoid(acc).astype(BF16)


def _in_gate(x2, norm_g, w_gate, casts):
    n = x2.shape[0]
    steps = n // GATE_TM
    jobs = [_CastJob(src, steps, ranges) for src, ranges in casts]
    row_block = lambda width: pl.BlockSpec((GATE_TM, width), lambda i: (i, 0))
    return pl.pallas_call(
        _with_casts(_in_gate_kernel, jobs, n_in=3, n_out=2),
        out_shape=[jax.ShapeDtypeStruct((n, GATE_DIM), BF16),
                   jax.ShapeDtypeStruct((n, D_MODEL), BF16)]
                  + [s for j in jobs for s in j.out_shapes()],
        grid=(steps,),
        in_specs=[row_block(D_MODEL), _resident((1, D_MODEL)), _resident((D_MODEL, GATE_DIM))]
                 + [j.in_spec() for j in jobs],
        out_specs=[row_block(GATE_DIM), row_block(D_MODEL)]
                  + [s for j in jobs for s in j.out_specs()],
        compiler_params=_params("arbitrary"),
        name="in_gate",
    )(x2, norm_g, w_gate, *[j.src for j in jobs])


def _store_heads(o_ref, k, acc, fn):
    for t in range(HEADS_PER_SUB):
        col = k * SUB + t * HEAD_DIM
        o_ref[:, col:col + HEAD_DIM] = fn(acc[:, t * HEAD_DIM:(t + 1) * HEAD_DIM]).astype(BF16)


def _pair_columns(head_a, head_b):
    half = HEAD_DIM // 2
    a, b = head_a * HEAD_DIM, head_b * HEAD_DIM
    return [(a, a + half), (b, b + half), (a + half, a + HEAD_DIM), (b + half, b + HEAD_DIM)]


ROPE_PAIRS = [(p, p + A_GROUP) for p in range(A_GROUP)] + [(A_Q_HEADS, A_Q_HEADS + 1)]
assert A_KV_HEADS == 2 and len(ROPE_PAIRS) * SUB == ROPE_DIM
ROPE_COLUMNS = [piece for pair in ROPE_PAIRS for piece in _pair_columns(*pair)]
HALF_LANES = HEAD_DIM // 2


def _in_rope_kernel(h_ref, w_ref, cos_ref, sin_ref, gq_ref, gk_ref, o_ref):
    lhs = h_ref[...]
    first_head = lax.broadcasted_iota(jnp.int32, (IN_TM, HEAD_DIM), 1) < HALF_LANES
    for k in range(ROPE_DIM // SUB):
        acc = jnp.dot(lhs, w_ref[:, k * SUB:(k + 1) * SUB], preferred_element_type=F32)
        gain_ref = gq_ref if k * SUB < A_Q_DIM else gk_ref
        lo, hi = acc[:, :HEAD_DIM], acc[:, HEAD_DIM:]
        sq = lo * lo + hi * hi
        ss_a = jnp.sum(jnp.where(first_head, sq, 0.0), axis=-1, keepdims=True)
        ss_b = jnp.sum(jnp.where(first_head, 0.0, sq), axis=-1, keepdims=True)
        inv = jnp.where(first_head, lax.rsqrt(ss_a * (1.0 / HEAD_DIM) + EPS),
                        lax.rsqrt(ss_b * (1.0 / HEAD_DIM) + EPS))
        lo = lo * inv * gain_ref[0:1, :]
        hi = hi * inv * gain_ref[1:2, :]
        cos, sin = cos_ref[...], sin_ref[...]
        o_ref[:, k * SUB:k * SUB + HEAD_DIM] = (lo * cos - hi * sin).astype(BF16)
        o_ref[:, k * SUB + HEAD_DIM:(k + 1) * SUB] = (hi * cos + lo * sin).astype(BF16)


def _in_norm_kernel(h_ref, w_ref, gq_ref, gk_ref, o_ref):
    lhs = h_ref[...]
    for k in range(NORM_DIM // SUB):
        acc = jnp.dot(lhs, w_ref[:, k * SUB:(k + 1) * SUB], preferred_element_type=F32)
        gain_ref = gq_ref if k * SUB < B_DIM else gk_ref
        _store_heads(o_ref, k, acc, lambda a: _rmsnorm(a, gain_ref[...]))


def _in_plain_kernel(h_ref, wvb_ref, wva_ref, o_ref):
    lhs = h_ref[...]
    for k in range(PLAIN_DIM // SUB):
        cols = slice(k * SUB, (k + 1) * SUB)
        w = wvb_ref[:, cols] if k * SUB < B_DIM else wva_ref[:, k * SUB - B_DIM:(k + 1) * SUB - B_DIM]
        o_ref[:, cols] = jnp.dot(lhs, w, preferred_element_type=F32).astype(BF16)


def _in_heads_kernel(h_ref, w_rope_ref, w_norm_ref, w_vb_ref, w_va_ref, cos_ref, sin_ref,
                     gqa_ref, gka_ref, gqb_ref, gkb_ref, rope_ref, norm_ref, plain_ref):
    _in_rope_kernel(h_ref, w_rope_ref, cos_ref, sin_ref, gqa_ref, gka_ref, rope_ref)
    _in_norm_kernel(h_ref, w_norm_ref, gqb_ref, gkb_ref, norm_ref)
    _in_plain_kernel(h_ref, w_vb_ref, w_va_ref, plain_ref)


def _in_heads(h, weights, cos, sin, gains, casts):
    n = h.shape[0]
    steps = n // IN_TM
    seq_tiles = SEQ // IN_TM
    jobs = [_CastJob(src, steps) for src in casts]
    row_block = lambda width: pl.BlockSpec((IN_TM, width), lambda i: (i, 0))
    pos_spec = pl.BlockSpec((IN_TM, HEAD_DIM), lambda i: (i % seq_tiles, 0))
    widths = (ROPE_DIM, NORM_DIM, PLAIN_DIM)
    return pl.pallas_call(
        _with_casts(_in_heads_kernel, jobs, n_in=11, n_out=3),
        out_shape=[jax.ShapeDtypeStruct((n, width), BF16) for width in widths]
                  + [s for j in jobs for s in j.out_shapes()],
        grid=(steps,),
        in_specs=([row_block(D_MODEL)] + [_resident(w.shape) for w in weights]
                  + [pos_spec, pos_spec] + [_resident(g.shape) for g in gains]
                  + [j.in_spec() for j in jobs]),
        out_specs=[row_block(width) for width in widths] + [s for j in jobs for s in j.out_specs()],
        compiler_params=_params("arbitrary"),
        name="in_heads",
    )(h, *weights, cos, sin, *gains, *[j.src for j in jobs])


A_TQ = 1024
A_BLK = WINDOW
A_NB = A_TQ // A_BLK
A_CHUNKS = SEQ // A_TQ


def _attn_a_kernel(sink_ref, q_ref, kp_ref, km_ref, kn_ref, vp_ref, vm_ref, vn_ref, o_ref):
    c = pl.program_id(1)
    k_all = jnp.concatenate([kp_ref[...], km_ref[...], kn_ref[...]], axis=0)
    v_all = jnp.concatenate([vp_ref[...], vm_ref[...], vn_ref[...]], axis=0)
    qq = lax.broadcasted_iota(jnp.int32, (A_BLK, 3 * A_BLK), 0)
    kk = lax.broadcasted_iota(jnp.int32, (A_BLK, 3 * A_BLK), 1)
    d = kk - qq
    band = (d >= 0) & (d <= 2 * WINDOW)
    ones = jnp.ones((3 * A_BLK, HEAD_DIM), BF16)
    lane = lax.broadcasted_iota(jnp.int32, (A_BLK, SUB), 1) % HEAD_DIM
    own_lanes = (lane < HALF_LANES, lane >= HALF_LANES)
    for n in range(A_NB):
        valid = band
        if n == 0:
            valid = valid & ((kk >= A_BLK) | (c > 0))
        if n == A_NB - 1:
            valid = valid & ((kk < 2 * A_BLK) | (c < A_CHUNKS - 1))
        k = k_all[n * A_BLK:(n + 3) * A_BLK, :]
        q_pairs = [q_ref[n * A_BLK:(n + 1) * A_BLK, p * SUB:(p + 1) * SUB] for p in range(A_GROUP)]
        qs = jnp.concatenate([jnp.where(own_lanes[h], q, jnp.zeros_like(q))
                              for h in range(A_KV_HEADS) for q in q_pairs], axis=0)
        s_all = lax.dot_general(qs, k, (((1,), (1,)), ((), ())),
                                preferred_element_type=F32)
        for h in range(A_KV_HEADS):
            v = v_all[n * A_BLK:(n + 3) * A_BLK, h * HEAD_DIM:(h + 1) * HEAD_DIM]
            heads = [h * A_GROUP + g for g in range(A_GROUP)]
            s = s_all[h * A_GROUP * A_BLK:(h + 1) * A_GROUP * A_BLK]
            ps, sink_terms = [], []
            for g, t in enumerate(heads):
                sg = jnp.where(valid, s[g * A_BLK:(g + 1) * A_BLK], NEG)
                m = jnp.maximum(jnp.max(sg, axis=-1, keepdims=True), sink_ref[t])
                ps.append(jnp.exp2(sg - m).astype(BF16))
                sink_terms.append(jnp.exp2(sink_ref[t] - m))
            o = jnp.dot(jnp.concatenate(ps, axis=0), jnp.concatenate([v, ones], axis=1),
                        preferred_element_type=F32)
            for g, t in enumerate(heads):
                og = o[g * A_BLK:(g + 1) * A_BLK]
                og = og[:, :HEAD_DIM] / (og[:, HEAD_DIM:] + sink_terms[g])
                o_ref[n * A_BLK:(n + 1) * A_BLK, t * HEAD_DIM:(t + 1) * HEAD_DIM] = og.astype(BF16)


def _attn_a(qk, vals, sink, batch, casts):
    n = qk.shape[0]
    blk_per_seq = SEQ // A_BLK
    k_col = A_Q_DIM // A_KV_DIM
    v_col = B_DIM // A_KV_DIM
    jobs = [_CastJob(src, batch * A_CHUNKS, step_of=lambda b, c, *_: b * A_CHUNKS + c)
            for src in casts]

    def prev_map(col):
        return lambda b, c, *_: (b * blk_per_seq + jnp.maximum(c * A_NB - 1, 0), col)

    def main_map(col):
        return lambda b, c, *_: (b * A_CHUNKS + c, col)

    def next_map(col):
        return lambda b, c, *_: (b * blk_per_seq + jnp.minimum(c * A_NB + A_NB, blk_per_seq - 1), col)

    halo = lambda m: pl.BlockSpec((A_BLK, A_KV_DIM), m)
    main = lambda m: pl.BlockSpec((A_TQ, A_KV_DIM), m)
    return pl.pallas_call(
        _with_casts(_attn_a_kernel, jobs, n_in=8, n_out=1),
        out_shape=[jax.ShapeDtypeStruct((n, A_Q_DIM), BF16)] + [s for j in jobs for s in j.out_shapes()],
        grid_spec=pltpu.PrefetchScalarGridSpec(
            num_scalar_prefetch=1,
            grid=(batch, A_CHUNKS),
            in_specs=[
                pl.BlockSpec((A_TQ, A_Q_DIM), main_map(0)),
                halo(prev_map(k_col)), main(main_map(k_col)), halo(next_map(k_col)),
                halo(prev_map(v_col)), main(main_map(v_col)), halo(next_map(v_col)),
            ] + [j.in_spec() for j in jobs],
            out_specs=[pl.BlockSpec((A_TQ, A_Q_DIM), main_map(0))]
                      + [s for j in jobs for s in j.out_specs()],
        ),
        compiler_params=_params("arbitrary", "arbitrary"),
        name="attn_a",
    )(sink, qk, qk, qk, qk, vals, vals, vals, *[j.src for j in jobs])


GRID_ROWS = SEQ // GRID_W
B_QR = 4
B_KR = B_QR + WIN_H
B_NRB = GRID_ROWS // B_QR
B_NQ = B_QR * GRID_W
B_NK = B_KR * GRID_W
B_KBLOCKS = B_KR // B_QR
B_KS_MAX = B_NRB - B_KBLOCKS
B_STEP_BLOCKS = 4
B_STEP_KBLOCKS = B_KBLOCKS + B_STEP_BLOCKS - 1
B_PATTERN_ROWBLOCKS = (0, 1, B_NRB - 1)


def _attn_b_row_windows():
    starts, row_ok = [], []
    for rb in B_PATTERN_ROWBLOCKS:
        ks = int(np.clip(rb - 1, 0, B_KS_MAX))
        qr = rb * B_QR + np.arange(B_QR)
        kr = ks * B_QR + np.arange(B_KR)
        rs = np.clip(qr - WIN_H // 2, 0, GRID_ROWS - WIN_H)
        row_ok.append((kr[None, :] >= rs[:, None]) & (kr[None, :] < rs[:, None] + WIN_H))
        starts.append(kr[0] - qr + WIN_H - 1)
    return np.stack(starts), np.stack(row_ok)


B_ROW_START, B_ROW_OK = _attn_b_row_windows()
B_STRIP_LO = max(0, -int(B_ROW_START.min()))
B_NSTRIP = int(B_ROW_START.max()) + B_KR - 1 + B_STRIP_LO
B_LANES = 2 * GRID_W


def _attn_b_build_bias(strip_ref, bias_ref):
    qc = lax.broadcasted_iota(jnp.int32, (GRID_W, B_LANES), 0)
    lane = lax.broadcasted_iota(jnp.int32, (GRID_W, B_LANES), 1)
    kc = lane % GRID_W
    cs = jnp.clip(qc - WIN_W // 2, 0, GRID_W - WIN_W)
    col_ok = (kc >= cs) & (kc < cs + WIN_W)
    first_row = lane < GRID_W
    for h in range(B_HEADS):
        tiles = {}

        def tile(i):
            if i not in tiles:
                rows = jnp.broadcast_to(strip_ref[h, i:i + 1, :], (GRID_W, B_LANES))
                toeplitz = pltpu.roll(rows, 0, axis=1, stride=1, stride_axis=0)
                tiles[i] = jnp.where(col_ok, toeplitz, NEG)
            return tiles[i]

        for p in range(len(B_PATTERN_ROWBLOCKS)):
            for qr in range(B_QR):
                for c in range(B_KR // 2):
                    ok0, ok1 = B_ROW_OK[p, qr, 2 * c], B_ROW_OK[p, qr, 2 * c + 1]
                    if ok0 or ok1:
                        t = tile(int(B_ROW_START[p, qr]) + 2 * c + B_STRIP_LO)
                        if not ok1:
                            t = jnp.where(first_row, t, NEG)
                        if not ok0:
                            t = jnp.where(first_row, NEG, t)
                    else:
                        t = jnp.full((GRID_W, B_LANES), NEG, F32)
                    bias_ref[p, h, qr * GRID_W:(qr + 1) * GRID_W, c * B_LANES:(c + 1) * B_LANES] = t


def _attn_b_key_base(step):
    return jnp.clip(step * B_STEP_BLOCKS - 1, 0, B_NRB - B_STEP_KBLOCKS)


def _attn_b_kernel(q_ref, *refs):
    k_refs = refs[:B_STEP_KBLOCKS]
    v_refs = refs[B_STEP_KBLOCKS:2 * B_STEP_KBLOCKS]
    strip_ref, o_ref, bias_ref, k_buf, v_buf = refs[2 * B_STEP_KBLOCKS:]
    step = pl.program_id(1)

    @pl.when((pl.program_id(0) == 0) & (step == 0))
    def _():
        _attn_b_build_bias(strip_ref, bias_ref)

    for t in range(B_STEP_KBLOCKS):
        k_buf[t * B_QR:(t + 1) * B_QR] = k_refs[t][...]
        v_buf[t * B_QR:(t + 1) * B_QR] = v_refs[t][...]

    ones = jnp.ones((B_NK, HEAD_DIM), BF16)
    for sb in range(B_STEP_BLOCKS):
        r = step * B_STEP_BLOCKS + sb
        pattern = jnp.where(r == 0, 0, jnp.where(r == B_NRB - 1, 2, 1))
        first_row = (jnp.clip(r - 1, 0, B_KS_MAX) - _attn_b_key_base(step)) * B_QR
        rows = slice(sb * B_QR, (sb + 1) * B_QR)
        for h in range(B_HEADS):
            hd = slice(h * HEAD_DIM, (h + 1) * HEAD_DIM)
            q = q_ref[rows, :, hd].reshape(B_NQ, HEAD_DIM)
            k = k_buf[pl.ds(first_row, B_KR), :, hd].reshape(B_NK, HEAD_DIM)
            v = v_buf[pl.ds(first_row, B_KR), :, hd].reshape(B_NK, HEAD_DIM)
            s = lax.dot_general(q, k, (((1,), (1,)), ((), ())),
                                preferred_element_type=F32) + bias_ref[pattern, h]
            m = jnp.max(s, axis=-1, keepdims=True)
            p = jnp.exp2(s - m).astype(BF16)
            o = jnp.dot(p, jnp.concatenate([v, ones], axis=1), preferred_element_type=F32)
            o = o[:, :HEAD_DIM] / o[:, HEAD_DIM:]
            o_ref[rows, :, hd] = o.reshape(B_QR, GRID_W, HEAD_DIM).astype(BF16)


def _attn_b_strips(rpb):
    rows = B_NSTRIP + 1
    table = jnp.pad(rpb.astype(F32) * LOG2E,
                    ((0, 0), (B_STRIP_LO, rows - B_STRIP_LO - rpb.shape[1]), (0, 0)))
    first, second = table[:, :-1], table[:, 1:]
    gap = jnp.zeros(first.shape[:2] + (GRID_W - 2 * WIN_W + 1,), F32)
    strips = jnp.concatenate(
        [first[..., WIN_W - 1:], gap, second, gap, first[..., :WIN_W - 1]], axis=-1)
    assert strips.shape == (B_HEADS, B_NSTRIP, B_LANES)
    return strips


def _attn_b(qk3, vals3, strips, batch):
    rows = qk3.shape[0]

    steps = B_NRB // B_STEP_BLOCKS

    def kv_spec(col, t):
        return pl.BlockSpec((B_QR, GRID_W, B_DIM),
                            lambda b, s: (b * B_NRB + _attn_b_key_base(s) + t, 0, col))

    q_spec = pl.BlockSpec((B_STEP_BLOCKS * B_QR, GRID_W, B_DIM), lambda b, s: (b * steps + s, 0, 0))
    window = (B_STEP_KBLOCKS * B_QR, GRID_W, B_DIM)
    return pl.pallas_call(
        _attn_b_kernel,
        out_shape=jax.ShapeDtypeStruct((rows, GRID_W, B_DIM), BF16),
        grid=(batch, steps),
        in_specs=([q_spec] + [kv_spec(1, t) for t in range(B_STEP_KBLOCKS)]
                  + [kv_spec(0, t) for t in range(B_STEP_KBLOCKS)]
                  + [_resident((B_HEADS, B_NSTRIP, B_LANES))]),
        out_specs=q_spec,
        scratch_shapes=[pltpu.VMEM((len(B_PATTERN_ROWBLOCKS), B_HEADS, B_NQ, B_NK), F32),
                        pltpu.VMEM(window, BF16), pltpu.VMEM(window, BF16)],
        compiler_params=_params("arbitrary", "arbitrary"),
        name="attn_b",
    )(qk3, *[qk3] * B_STEP_KBLOCKS, *[vals3] * B_STEP_KBLOCKS, strips)


MG_TM = 512


def _merge_kernel(x_ref, oa_ref, ob_ref, ga_ref, gb_ref, wa_ref, wb_ref, wo_ref, g_ref,
                  x1_ref, h2_ref):
    a = jnp.dot(oa_ref[...], wa_ref[...], preferred_element_type=F32)
    b = jnp.dot(ob_ref[...], wb_ref[...], preferred_element_type=F32)
    merged = ga_ref[...].astype(F32) * a + gb_ref[...].astype(F32) * b
    y = jnp.dot(merged.astype(BF16), wo_ref[...], preferred_element_type=F32)
    x1 = x_ref[...] + y
    x1_ref[...] = x1
    h2_ref[...] = _rmsnorm(x1, g_ref[...]).astype(BF16)


def _merge(x2, out_a, out_b, gates, wa, wb, wo, norm_g):
    n = x2.shape[0]
    row_block = lambda width, col=0: pl.BlockSpec((MG_TM, width), lambda i: (i, col))
    return pl.pallas_call(
        _merge_kernel,
        out_shape=(jax.ShapeDtypeStruct((n, D_MODEL), F32),
                   jax.ShapeDtypeStruct((n, D_MODEL), BF16)),
        grid=(n // MG_TM,),
        in_specs=[
            row_block(D_MODEL), row_block(A_Q_DIM), row_block(B_DIM),
            row_block(D_MODEL, 0), row_block(D_MODEL, 1),
            _resident((A_Q_DIM, D_MODEL)), _resident((B_DIM, D_MODEL)),
            _resident((D_MODEL, D_MODEL)), _resident((1, D_MODEL)),
        ],
        out_specs=(row_block(D_MODEL), row_block(D_MODEL)),
        compiler_params=_params("arbitrary"),
        name="merge",
    )(x2, out_a, out_b, gates, gates, wa, wb, wo, norm_g)


UP_TM = 1024
UP_NJ = 4
UP_TN = D_FF // UP_NJ
UP_LANES = SUB // 2
UP_NSUB = UP_TN // UP_LANES
HALO = 16
UP_ROWS = UP_TM + 2 * HALO
UP_SPLIT_TAIL = 2


def _ffn_up_kernel(hp_ref, hm_ref, hn_ref, wg_ref, wv_ref, cwg_ref, cwv_ref, cbg_ref, cbv_ref,
                   o_ref, lhs_ref, u_ref):
    i = pl.program_id(0)
    tiles_per_seq = SEQ // UP_TM

    @pl.when(pl.program_id(1) == 0)
    def _():
        first = (i % tiles_per_seq) == 0
        last = (i % tiles_per_seq) == tiles_per_seq - 1
        lhs_ref[0:HALO, :] = jnp.where(first, jnp.zeros_like(hp_ref[...]), hp_ref[...])
        lhs_ref[HALO:HALO + UP_TM, :] = hm_ref[...]
        lhs_ref[HALO + UP_TM:, :] = jnp.where(last, jnp.zeros_like(hn_ref[...]), hn_ref[...])

    def conv(slot, rows, lanes, cw_ref, cb_ref, cols, scale):
        below = u_ref[slot, HALO - 1:HALO - 1 + rows, lanes]
        mid = u_ref[slot, HALO:HALO + rows, lanes]
        above = u_ref[slot, HALO + 1:HALO + 1 + rows, lanes]
        c0, c1, c2 = (scale * cw_ref[t:t + 1, cols] for t in range(3))
        y = mid * c1 + scale * cb_ref[:, cols]
        y = y + below * c0
        return y + above * c2

    def unit(k, slot, row0, rows):
        cols = slice(k * UP_LANES, (k + 1) * UP_LANES)
        w = jnp.concatenate([wg_ref[:, cols], wv_ref[:, cols]], axis=1)
        u_ref[slot, 0:rows + 2 * HALO, :] = jnp.dot(
            lhs_ref[row0:row0 + rows + 2 * HALO, :], w, preferred_element_type=F32)
        half_gate = conv(slot, rows, slice(0, UP_LANES), cwg_ref, cbg_ref, cols, 0.5)
        val = conv(slot, rows, slice(UP_LANES, SUB), cwv_ref, cbv_ref, cols, 1.0)
        silu = half_gate * jnp.tanh(half_gate) + half_gate
        o_ref[row0:row0 + rows, cols] = (silu * val).astype(BF16)

    units = []
    for k in range(UP_NSUB):
        if k < UP_NSUB - UP_SPLIT_TAIL:
            units.append((k, 0, UP_TM))
        else:
            units += [(k, 0, UP_TM // 2), (k, UP_TM // 2, UP_TM // 2)]
    for slot, (k, row0, rows) in enumerate(units):
        unit(k, slot % 2, row0, rows)


def _ffn_up(h2, w_up, conv_w, conv_b):
    n = h2.shape[0]
    halo_blocks = UP_TM // HALO
    last_halo = n // HALO - 1
    return pl.pallas_call(
        _ffn_up_kernel,
        out_shape=jax.ShapeDtypeStruct((n, D_FF), BF16),
        grid=(n // UP_TM, UP_NJ),
        in_specs=[
            pl.BlockSpec((HALO, D_MODEL), lambda i, j: (jnp.maximum(i * halo_blocks - 1, 0), 0)),
            pl.BlockSpec((UP_TM, D_MODEL), lambda i, j: (i, 0)),
            pl.BlockSpec((HALO, D_MODEL),
                         lambda i, j: (jnp.minimum((i + 1) * halo_blocks, last_halo), 0)),
            pl.BlockSpec((D_MODEL, UP_TN), lambda i, j: (0, j)),
            pl.BlockSpec((D_MODEL, UP_TN), lambda i, j: (0, UP_NJ + j)),
            pl.BlockSpec((3, UP_TN), lambda i, j: (0, j)),
            pl.BlockSpec((3, UP_TN), lambda i, j: (0, UP_NJ + j)),
            pl.BlockSpec((1, UP_TN), lambda i, j: (0, j)),
            pl.BlockSpec((1, UP_TN), lambda i, j: (0, UP_NJ + j)),
        ],
        out_specs=pl.BlockSpec((UP_TM, UP_TN), lambda i, j: (i, j)),
        scratch_shapes=[pltpu.VMEM((UP_ROWS, D_MODEL), BF16),
                        pltpu.VMEM((2, UP_ROWS, SUB), F32)],
        compiler_params=_params("arbitrary", "arbitrary"),
        name="ffn_up",
    )(h2, h2, h2, w_up, w_up, conv_w, conv_w, conv_b, conv_b)


DN_TM = 512


def _ffn_down_kernel(a_ref, w_ref, x_ref, o_ref):
    lhs = a_ref[...]
    for k in range(D_MODEL // SUB):
        cols = slice(k * SUB, (k + 1) * SUB)
        o_ref[:, cols] = x_ref[:, cols] + jnp.dot(lhs, w_ref[:, cols], preferred_element_type=F32)


def _ffn_down(act, w_down, x1):
    n = act.shape[0]
    row_block = lambda width: pl.BlockSpec((DN_TM, width), lambda i: (i, 0))
    return pl.pallas_call(
        _ffn_down_kernel,
        out_shape=jax.ShapeDtypeStruct((n, D_MODEL), F32),
        grid=(n // DN_TM,),
        in_specs=[row_block(D_FF), _resident((D_FF, D_MODEL)), row_block(D_MODEL)],
        out_specs=row_block(D_MODEL),
        compiler_params=_params("arbitrary"),
        name="ffn_down",
    )(act, w_down, x1)


def _rope_tables():
    with jax.ensure_compile_time_eval():
        half = HEAD_DIM // 2
        pos = jnp.arange(SEQ, dtype=F32)
        inv_freq = ROPE_THETA ** (-jnp.arange(half, dtype=F32) * (2.0 / HEAD_DIM))
        ang = pos[:, None] * inv_freq[None, :]
        cos, sin = jnp.cos(ang), jnp.sin(ang)
        return jnp.concatenate([cos, cos], axis=-1), jnp.concatenate([sin, sin], axis=-1)


def _paired_gain(gain):
    lo, hi = gain[:HALF_LANES], gain[HALF_LANES:]
    return jnp.stack([jnp.concatenate([lo, lo]), jnp.concatenate([hi, hi])])


IN_COL_VA = ROPE_DIM
IN_COL_QB = IN_COL_VA + A_KV_DIM
IN_COL_VB = IN_COL_QB + NORM_DIM
IN_COL_GATE = IN_COL_VB + B_DIM
IN_HEAD_OUTPUTS = [ROPE_COLUMNS, [(IN_COL_QB, IN_COL_VB)], [(IN_COL_VB, IN_COL_GATE)],
                   [(IN_COL_VA, IN_COL_QB)]]


def kernel(x, norm_mix, w_in, a_q_norm, a_k_norm, a_sink, b_q_norm, b_k_norm, b_rpb,
           w_branch_a, w_branch_b, w_out, norm_ffn, w_up, conv_w, conv_b, w_down):
    batch, seq, d_model = x.shape
    assert (seq, d_model) == (SEQ, D_MODEL)
    n = batch * seq
    cos, sin = _rope_tables()
    scale = LOG2E / math.sqrt(HEAD_DIM)
    x2 = x.reshape(n, d_model)
    for l in range(norm_mix.shape[0]):
        w_gate = w_in[l][:, IN_COL_GATE:].astype(BF16)
        gates, h, w_rope, w_norm, w_vb, w_va, w_up_bf = _in_gate(
            x2, norm_mix[l][None], w_gate, [(w_in[l], IN_HEAD_OUTPUTS), (w_up[l], None)])
        gains = (_paired_gain(a_q_norm[l] * scale), _paired_gain(a_k_norm[l]),
                 b_q_norm[l][None] * scale, b_k_norm[l][None])
        qk_a, qk_b, vals = _in_heads(h, (w_rope, w_norm, w_vb, w_va), cos, sin, gains, [])
        out_a, wa_bf, wb_bf, w_out_bf, w_down_bf = _attn_a(
            qk_a, vals, a_sink[l] * LOG2E, batch,
            [w_branch_a[l], w_branch_b[l], w_out[l], w_down[l]])
        out_b = _attn_b(qk_b.reshape(n // GRID_W, GRID_W, NORM_DIM),
                        vals.reshape(n // GRID_W, GRID_W, PLAIN_DIM), _attn_b_strips(b_rpb[l]), batch)
        x1, h2 = _merge(x2, out_a, out_b.reshape(n, B_DIM), gates, wa_bf, wb_bf, w_out_bf,
                        norm_ffn[l][None])
        act = _ffn_up(h2, w_up_bf, conv_w[l], conv_b[l][None])
        x2 = _ffn_down(act, w_down_bf, x1)
    return x2.reshape(batch, seq, d_model)
```

```python
import math

import numpy as np
import jax
import jax.numpy as jnp
from jax import lax
from jax.experimental import pallas as pl
from jax.experimental.pallas import tpu as pltpu

D_MODEL = 2048
SEQ = 4096
HEAD_DIM = 128
A_Q_HEADS = 8
A_KV_HEADS = 2
A_GROUP = A_Q_HEADS // A_KV_HEADS
WINDOW = 128
B_HEADS = 8
GRID_W = 64
WIN_H = 8
WIN_W = 16
D_FF = 5632
ROPE_THETA = 10000.0
EPS = 1e-6
NEG = -1e30
LOG2E = math.log2(math.e)

A_Q_DIM = A_Q_HEADS * HEAD_DIM
A_KV_DIM = A_KV_HEADS * HEAD_DIM
B_DIM = B_HEADS * HEAD_DIM

VMEM_LIMIT_BYTES = 56 * 1024 * 1024
MXU_COLS = 256
SUB = MXU_COLS
HEADS_PER_SUB = SUB // HEAD_DIM

BF16 = jnp.bfloat16
F32 = jnp.float32


def _params(*semantics):
    return pltpu.CompilerParams(dimension_semantics=semantics, vmem_limit_bytes=VMEM_LIMIT_BYTES)


def _resident(shape):
    return pl.BlockSpec(shape, lambda *_: (0,) * len(shape), pipeline_mode=pl.Buffered(1))


def _rmsnorm(x, gain):
    ms = jnp.mean(x * x, axis=-1, keepdims=True)
    return x * lax.rsqrt(ms + EPS) * gain


class _CastJob:
    def __init__(self, src, steps, outputs=None, step_of=lambda i: i):
        rows, cols = src.shape
        self.src = src
        self.outputs = outputs or [[(0, cols)]]
        self.slab = rows // steps
        assert self.slab * steps == rows
        self.span = max(hi for pieces in self.outputs for _, hi in pieces)
        self.widths = [sum(hi - lo for lo, hi in pieces) for pieces in self.outputs]
        self.index_map = lambda *idx: (step_of(*idx), 0)

    def in_spec(self):
        return pl.BlockSpec((self.slab, self.span), self.index_map)

    def out_specs(self):
        return [pl.BlockSpec((self.slab, width), self.index_map) for width in self.widths]

    def out_shapes(self):
        return [jax.ShapeDtypeStruct((self.src.shape[0], width), BF16) for width in self.widths]

    def run(self, src_ref, out_refs):
        for pieces, out_ref in zip(self.outputs, out_refs):
            parts = [src_ref[:, lo:hi] for lo, hi in pieces]
            value = parts[0] if len(parts) == 1 else jnp.concatenate(parts, axis=1)
            out_ref[...] = value.astype(BF16)


def _with_casts(body, jobs, n_in, n_out):
    n_cast_out = sum(len(j.outputs) for j in jobs)

    def wrapped(*refs):
        ins = refs[:n_in]
        srcs = refs[n_in:n_in + len(jobs)]
        outs = refs[n_in + len(jobs):n_in + len(jobs) + n_out]
        cast_outs = list(refs[n_in + len(jobs) + n_out:n_in + len(jobs) + n_out + n_cast_out])
        scratch = refs[n_in + len(jobs) + n_out + n_cast_out:]
        for job, src_ref in zip(jobs, srcs):
            job.run(src_ref, [cast_outs.pop(0) for _ in job.outputs])
        body(*ins, *outs, *scratch)

    return wrapped


def _sigmoid(x):
    return 0.5 * jnp.tanh(0.5 * x) + 0.5


IN_TM = 1024
GATE_TM = 512
GATE_DIM = 2 * D_MODEL
ROPE_DIM = A_Q_DIM + A_KV_DIM
NORM_DIM = 2 * B_DIM
PLAIN_DIM = B_DIM + A_KV_DIM


def _in_gate_kernel(x_ref, g_ref, w_ref, o_ref, h_ref):
    h_ref[...] = _rmsnorm(x_ref[...], g_ref[...]).astype(BF16)
    lhs = h_ref[...]
    for k in range(GATE_DIM // SUB):
        cols = slice(k * SUB, (k + 1) * SUB)
        acc = jnp.dot(lhs, w_ref[:, cols], preferred_element_type=F32)
        o_ref[:, cols] = _sigmoid(acc).astype(BF16)


def _in_gate(x2, norm_g, w_gate, casts):
    n = x2.shape[0]
    steps = n // GATE_TM
    jobs = [_CastJob(src, steps, ranges) for src, ranges in casts]
    row_block = lambda width: pl.BlockSpec((GATE_TM, width), lambda i: (i, 0))
    return pl.pallas_call(
        _with_casts(_in_gate_kernel, jobs, n_in=3, n_out=2),
        out_shape=[jax.ShapeDtypeStruct((n, GATE_DIM), BF16),
                   jax.ShapeDtypeStruct((n, D_MODEL), BF16)]
                  + [s for j in jobs for s in j.out_shapes()],
        grid=(steps,),
        in_specs=[row_block(D_MODEL), _resident((1, D_MODEL)), _resident((D_MODEL, GATE_DIM))]
                 + [j.in_spec() for j in jobs],
        out_specs=[row_block(GATE_DIM), row_block(D_MODEL)]
                  + [s for j in jobs for s in j.out_specs()],
        compiler_params=_params("arbitrary"),
        name="in_gate",
    )(x2, norm_g, w_gate, *[j.src for j in jobs])


def _store_heads(o_ref, k, acc, fn):
    for t in range(HEADS_PER_SUB):
        col = k * SUB + t * HEAD_DIM
        o_ref[:, col:col + HEAD_DIM] = fn(acc[:, t * HEAD_DIM:(t + 1) * HEAD_DIM]).astype(BF16)


def _pair_columns(head_a, head_b):
    half = HEAD_DIM // 2
    a, b = head_a * HEAD_DIM, head_b * HEAD_DIM
    return [(a, a + half), (b, b + half), (a + half, a + HEAD_DIM), (b + half, b + HEAD_DIM)]


ROPE_PAIRS = [(p, p + A_GROUP) for p in range(A_GROUP)] + [(A_Q_HEADS, A_Q_HEADS + 1)]
assert A_KV_HEADS == 2 and len(ROPE_PAIRS) * SUB == ROPE_DIM
ROPE_COLUMNS = [piece for pair in ROPE_PAIRS for piece in _pair_columns(*pair)]
HALF_LANES = HEAD_DIM // 2


def _in_rope_kernel(h_ref, w_ref, cos_ref, sin_ref, gq_ref, gk_ref, o_ref):
    lhs = h_ref[...]
    first_head = lax.broadcasted_iota(jnp.int32, (IN_TM, HEAD_DIM), 1) < HALF_LANES
    for k in range(ROPE_DIM // SUB):
        acc = jnp.dot(lhs, w_ref[:, k * SUB:(k + 1) * SUB], preferred_element_type=F32)
        gain_ref = gq_ref if k * SUB < A_Q_DIM else gk_ref
        lo, hi = acc[:, :HEAD_DIM], acc[:, HEAD_DIM:]
        sq = lo * lo + hi * hi
        ss_a = jnp.sum(jnp.where(first_head, sq, 0.0), axis=-1, keepdims=True)
        ss_b = jnp.sum(jnp.where(first_head, 0.0, sq), axis=-1, keepdims=True)
        inv = jnp.where(first_head, lax.rsqrt(ss_a * (1.0 / HEAD_DIM) + EPS),
                        lax.rsqrt(ss_b * (1.0 / HEAD_DIM) + EPS))
        lo = lo * inv * gain_ref[0:1, :]
        hi = hi * inv * gain_ref[1:2, :]
        cos, sin = cos_ref[...], sin_ref[...]
        o_ref[:, k * SUB:k * SUB + HEAD_DIM] = (lo * cos - hi * sin).astype(BF16)
        o_ref[:, k * SUB + HEAD_DIM:(k + 1) * SUB] = (hi * cos + lo * sin).astype(BF16)


def _in_norm_kernel(h_ref, w_ref, gq_ref, gk_ref, o_ref):
    lhs = h_ref[...]
    for k in range(NORM_DIM // SUB):
        acc = jnp.dot(lhs, w_ref[:, k * SUB:(k + 1) * SUB], preferred_element_type=F32)
        gain_ref = gq_ref if k * SUB < B_DIM else gk_ref
        _store_heads(o_ref, k, acc, lambda a: _rmsnorm(a, gain_ref[...]))


def _in_plain_kernel(h_ref, wvb_ref, wva_ref, o_ref):
    lhs = h_ref[...]
    for k in range(PLAIN_DIM // SUB):
        cols = slice(k * SUB, (k + 1) * SUB)
        w = wvb_ref[:, cols] if k * SUB < B_DIM else wva_ref[:, k * SUB - B_DIM:(k + 1) * SUB - B_DIM]
        o_ref[:, cols] = jnp.dot(lhs, w, preferred_element_type=F32).astype(BF16)


def _in_heads_kernel(h_ref, w_rope_ref, w_norm_ref, w_vb_ref, w_va_ref, cos_ref, sin_ref,
                     gqa_ref, gka_ref, gqb_ref, gkb_ref, rope_ref, norm_ref, plain_ref):
    _in_rope_kernel(h_ref, w_rope_ref, cos_ref, sin_ref, gqa_ref, gka_ref, rope_ref)
    _in_norm_kernel(h_ref, w_norm_ref, gqb_ref, gkb_ref, norm_ref)
    _in_plain_kernel(h_ref, w_vb_ref, w_va_ref, plain_ref)


def _in_heads(h, weights, cos, sin, gains, casts):
    n = h.shape[0]
    steps = n // IN_TM
    seq_tiles = SEQ // IN_TM
    jobs = [_CastJob(src, steps) for src in casts]
    row_block = lambda width: pl.BlockSpec((IN_TM, width), lambda i: (i, 0))
    pos_spec = pl.BlockSpec((IN_TM, HEAD_DIM), lambda i: (i % seq_tiles, 0))
    widths = (ROPE_DIM, NORM_DIM, PLAIN_DIM)
    return pl.pallas_call(
        _with_casts(_in_heads_kernel, jobs, n_in=11, n_out=3),
        out_shape=[jax.ShapeDtypeStruct((n, width), BF16) for width in widths]
                  + [s for j in jobs for s in j.out_shapes()],
        grid=(steps,),
        in_specs=([row_block(D_MODEL)] + [_resident(w.shape) for w in weights]
                  + [pos_spec, pos_spec] + [_resident(g.shape) for g in gains]
                  + [j.in_spec() for j in jobs]),
        out_specs=[row_block(width) for width in widths] + [s for j in jobs for s in j.out_specs()],
        compiler_params=_params("arbitrary"),
        name="in_heads",
    )(h, *weights, cos, sin, *gains, *[j.src for j in jobs])


A_TQ = 1024
A_BLK = WINDOW
A_NB = A_TQ // A_BLK
A_CHUNKS = SEQ // A_TQ


def _attn_a_kernel(sink_ref, q_ref, kp_ref, km_ref, kn_ref, vp_ref, vm_ref, vn_ref, o_ref):
    c = pl.program_id(1)
    k_all = jnp.concatenate([kp_ref[...], km_ref[...], kn_ref[...]], axis=0)
    v_all = jnp.concatenate([vp_ref[...], vm_ref[...], vn_ref[...]], axis=0)
    qq = lax.broadcasted_iota(jnp.int32, (A_BLK, 3 * A_BLK), 0)
    kk = lax.broadcasted_iota(jnp.int32, (A_BLK, 3 * A_BLK), 1)
    d = kk - qq
    band = (d >= 0) & (d <= 2 * WINDOW)
    ones = jnp.ones((3 * A_BLK, HEAD_DIM), BF16)
    lane = lax.broadcasted_iota(jnp.int32, (A_BLK, SUB), 1) % HEAD_DIM
    own_lanes = (lane < HALF_LANES, lane >= HALF_LANES)
    for n in range(A_NB):
        valid = band
        if n == 0:
            valid = valid & ((kk >= A_BLK) | (c > 0))
        if n == A_NB - 1:
            valid = valid & ((kk < 2 * A_BLK) | (c < A_CHUNKS - 1))
        k = k_all[n * A_BLK:(n + 3) * A_BLK, :]
        q_pairs = [q_ref[n * A_BLK:(n + 1) * A_BLK, p * SUB:(p + 1) * SUB] for p in range(A_GROUP)]
        qs = jnp.concatenate([jnp.where(own_lanes[h], q, jnp.zeros_like(q))
                              for h in range(A_KV_HEADS) for q in q_pairs], axis=0)
        s_all = lax.dot_general(qs, k, (((1,), (1,)), ((), ())),
                                preferred_element_type=F32)
        for h in range(A_KV_HEADS):
            v = v_all[n * A_BLK:(n + 3) * A_BLK, h * HEAD_DIM:(h + 1) * HEAD_DIM]
            heads = [h * A_GROUP + g for g in range(A_GROUP)]
            s = s_all[h * A_GROUP * A_BLK:(h + 1) * A_GROUP * A_BLK]
            ps, sink_terms = [], []
            for g, t in enumerate(heads):
                sg = jnp.where(valid, s[g * A_BLK:(g + 1) * A_BLK], NEG)
                m = jnp.maximum(jnp.max(sg, axis=-1, keepdims=True), sink_ref[t])
                ps.append(jnp.exp2(sg - m).astype(BF16))
                sink_terms.append(jnp.exp2(sink_ref[t] - m))
            o = jnp.dot(jnp.concatenate(ps, axis=0), jnp.concatenate([v, ones], axis=1),
                        preferred_element_type=F32)
            for g, t in enumerate(heads):
                og = o[g * A_BLK:(g + 1) * A_BLK]
                og = og[:, :HEAD_DIM] / (og[:, HEAD_DIM:] + sink_terms[g])
                o_ref[n * A_BLK:(n + 1) * A_BLK, t * HEAD_DIM:(t + 1) * HEAD_DIM] = og.astype(BF16)


def _attn_a(qk, vals, sink, batch, casts):
    n = qk.shape[0]
    blk_per_seq = SEQ // A_BLK
    k_col = A_Q_DIM // A_KV_DIM
    v_col = B_DIM // A_KV_DIM
    jobs = [_CastJob(src, batch * A_CHUNKS, step_of=lambda b, c, *_: b * A_CHUNKS + c)
            for src in casts]

    def prev_map(col):
        return lambda b, c, *_: (b * blk_per_seq + jnp.maximum(c * A_NB - 1, 0), col)

    def main_map(col):
        return lambda b, c, *_: (b * A_CHUNKS + c, col)

    def next_map(col):
        return lambda b, c, *_: (b * blk_per_seq + jnp.minimum(c * A_NB + A_NB, blk_per_seq - 1), col)

    halo = lambda m: pl.BlockSpec((A_BLK, A_KV_DIM), m)
    main = lambda m: pl.BlockSpec((A_TQ, A_KV_DIM), m)
    return pl.pallas_call(
        _with_casts(_attn_a_kernel, jobs, n_in=8, n_out=1),
        out_shape=[jax.ShapeDtypeStruct((n, A_Q_DIM), BF16)] + [s for j in jobs for s in j.out_shapes()],
        grid_spec=pltpu.PrefetchScalarGridSpec(
            num_scalar_prefetch=1,
            grid=(batch, A_CHUNKS),
            in_specs=[
                pl.BlockSpec((A_TQ, A_Q_DIM), main_map(0)),
                halo(prev_map(k_col)), main(main_map(k_col)), halo(next_map(k_col)),
                halo(prev_map(v_col)), main(main_map(v_col)), halo(next_map(v_col)),
            ] + [j.in_spec() for j in jobs],
            out_specs=[pl.BlockSpec((A_TQ, A_Q_DIM), main_map(0))]
                      + [s for j in jobs for s in j.out_specs()],
        ),
        compiler_params=_params("arbitrary", "arbitrary"),
        name="attn_a",
    )(sink, qk, qk, qk, qk, vals, vals, vals, *[j.src for j in jobs])


GRID_ROWS = SEQ // GRID_W
B_QR = 4
B_KR = B_QR + WIN_H
B_NRB = GRID_ROWS // B_QR
B_NQ = B_QR * GRID_W
B_NK = B_KR * GRID_W
B_KBLOCKS = B_KR // B_QR
B_KS_MAX = B_NRB - B_KBLOCKS
B_STEP_BLOCKS = 4
B_STEP_KBLOCKS = B_KBLOCKS + B_STEP_BLOCKS - 1
B_PATTERN_ROWBLOCKS = (0, 1, B_NRB - 1)


def _attn_b_row_windows():
    starts, row_ok = [], []
    for rb in B_PATTERN_ROWBLOCKS:
        ks = int(np.clip(rb - 1, 0, B_KS_MAX))
        qr = rb * B_QR + np.arange(B_QR)
        kr = ks * B_QR + np.arange(B_KR)
        rs = np.clip(qr - WIN_H // 2, 0, GRID_ROWS - WIN_H)
        row_ok.append((kr[None, :] >= rs[:, None]) & (kr[None, :] < rs[:, None] + WIN_H))
        starts.append(kr[0] - qr + WIN_H - 1)
    return np.stack(starts), np.stack(row_ok)


B_ROW_START, B_ROW_OK = _attn_b_row_windows()
B_STRIP_LO = max(0, -int(B_ROW_START.min()))
B_NSTRIP = int(B_ROW_START.max()) + B_KR - 1 + B_STRIP_LO
B_LANES = 2 * GRID_W


def _attn_b_build_bias(strip_ref, bias_ref):
    qc = lax.broadcasted_iota(jnp.int32, (GRID_W, B_LANES), 0)
    lane = lax.broadcasted_iota(jnp.int32, (GRID_W, B_LANES), 1)
    kc = lane % GRID_W
    cs = jnp.clip(qc - WIN_W // 2, 0, GRID_W - WIN_W)
    col_ok = (kc >= cs) & (kc < cs + WIN_W)
    first_row = lane < GRID_W
    for h in range(B_HEADS):
        tiles = {}

        def tile(i):
            if i not in tiles:
                rows = jnp.broadcast_to(strip_ref[h, i:i + 1, :], (GRID_W, B_LANES))
                toeplitz = pltpu.roll(rows, 0, axis=1, stride=1, stride_axis=0)
                tiles[i] = jnp.where(col_ok, toeplitz, NEG)
            return tiles[i]

        for p in range(len(B_PATTERN_ROWBLOCKS)):
            for qr in range(B_QR):
                for c in range(B_KR // 2):
                    ok0, ok1 = B_ROW_OK[p, qr, 2 * c], B_ROW_OK[p, qr, 2 * c + 1]
                    if ok0 or ok1:
                        t = tile(int(B_ROW_START[p, qr]) + 2 * c + B_STRIP_LO)
                        if not ok1:
                            t = jnp.where(first_row, t, NEG)
                        if not ok0:
                            t = jnp.where(first_row, NEG, t)
                    else:
                        t = jnp.full((GRID_W, B_LANES), NEG, F32)
                    bias_ref[p, h, qr * GRID_W:(qr + 1) * GRID_W, c * B_LANES:(c + 1) * B_LANES] = t


def _attn_b_key_base(step):
    return jnp.clip(step * B_STEP_BLOCKS - 1, 0, B_NRB - B_STEP_KBLOCKS)


def _attn_b_kernel(q_ref, *refs):
    k_refs = refs[:B_STEP_KBLOCKS]
    v_refs = refs[B_STEP_KBLOCKS:2 * B_STEP_KBLOCKS]
    strip_ref, o_ref, bias_ref, k_buf, v_buf = refs[2 * B_STEP_KBLOCKS:]
    step = pl.program_id(1)

    @pl.when((pl.program_id(0) == 0) & (step == 0))
    def _():
        _attn_b_build_bias(strip_ref, bias_ref)

    for t in range(B_STEP_KBLOCKS):
        k_buf[t * B_QR:(t + 1) * B_QR] = k_refs[t][...]
        v_buf[t * B_QR:(t + 1) * B_QR] = v_refs[t][...]

    ones = jnp.ones((B_NK, HEAD_DIM), BF16)
    for sb in range(B_STEP_BLOCKS):
        r = step * B_STEP_BLOCKS + sb
        pattern = jnp.where(r == 0, 0, jnp.where(r == B_NRB - 1, 2, 1))
        first_row = (jnp.clip(r - 1, 0, B_KS_MAX) - _attn_b_key_base(step)) * B_QR
        rows = slice(sb * B_QR, (sb + 1) * B_QR)
        for h in range(B_HEADS):
            hd = slice(h * HEAD_DIM, (h + 1) * HEAD_DIM)
            q = q_ref[rows, :, hd].reshape(B_NQ, HEAD_DIM)
            k = k_buf[pl.ds(first_row, B_KR), :, hd].reshape(B_NK, HEAD_DIM)
            v = v_buf[pl.ds(first_row, B_KR), :, hd].reshape(B_NK, HEAD_DIM)
            s = lax.dot_general(q, k, (((1,), (1,)), ((), ())),
                                preferred_element_type=F32) + bias_ref[pattern, h]
            m = jnp.max(s, axis=-1, keepdims=True)
            p = jnp.exp2(s - m).astype(BF16)
            o = jnp.dot(p, jnp.concatenate([v, ones], axis=1), preferred_element_type=F32)
            o = o[:, :HEAD_DIM] / o[:, HEAD_DIM:]
            o_ref[rows, :, hd] = o.reshape(B_QR, GRID_W, HEAD_DIM).astype(BF16)


def _attn_b_strips(rpb):
    rows = B_NSTRIP + 1
    table = jnp.pad(rpb.astype(F32) * LOG2E,
                    ((0, 0), (B_STRIP_LO, rows - B_STRIP_LO - rpb.shape[1]), (0, 0)))
    first, second = table[:, :-1], table[:, 1:]
    gap = jnp.zeros(first.shape[:2] + (GRID_W - 2 * WIN_W + 1,), F32)
    strips = jnp.concatenate(
        [first[..., WIN_W - 1:], gap, second, gap, first[..., :WIN_W - 1]], axis=-1)
    assert strips.shape == (B_HEADS, B_NSTRIP, B_LANES)
    return strips


def _attn_b(qk3, vals3, strips, batch):
    rows = qk3.shape[0]

    steps = B_NRB // B_STEP_BLOCKS

    def kv_spec(col, t):
        return pl.BlockSpec((B_QR, GRID_W, B_DIM),
                            lambda b, s: (b * B_NRB + _attn_b_key_base(s) + t, 0, col))

    q_spec = pl.BlockSpec((B_STEP_BLOCKS * B_QR, GRID_W, B_DIM), lambda b, s: (b * steps + s, 0, 0))
    window = (B_STEP_KBLOCKS * B_QR, GRID_W, B_DIM)
    return pl.pallas_call(
        _attn_b_kernel,
        out_shape=jax.ShapeDtypeStruct((rows, GRID_W, B_DIM), BF16),
        grid=(batch, steps),
        in_specs=([q_spec] + [kv_spec(1, t) for t in range(B_STEP_KBLOCKS)]
                  + [kv_spec(0, t) for t in range(B_STEP_KBLOCKS)]
                  + [_resident((B_HEADS, B_NSTRIP, B_LANES))]),
        out_specs=q_spec,
        scratch_shapes=[pltpu.VMEM((len(B_PATTERN_ROWBLOCKS), B_HEADS, B_NQ, B_NK), F32),
                        pltpu.VMEM(window, BF16), pltpu.VMEM(window, BF16)],
        compiler_params=_params("arbitrary", "arbitrary"),
        name="attn_b",
    )(qk3, *[qk3] * B_STEP_KBLOCKS, *[vals3] * B_STEP_KBLOCKS, strips)


MG_TM = 512


def _merge_kernel(x_ref, oa_ref, ob_ref, ga_ref, gb_ref, wa_ref, wb_ref, wo_ref, g_ref,
                  x1_ref, h2_ref):
    a = jnp.dot(oa_ref[...], wa_ref[...], preferred_element_type=F32)
    b = jnp.dot(ob_ref[...], wb_ref[...], preferred_element_type=F32)
    merged = ga_ref[...].astype(F32) * a + gb_ref[...].astype(F32) * b
    y = jnp.dot(merged.astype(BF16), wo_ref[...], preferred_element_type=F32)
    x1 = x_ref[...] + y
    x1_ref[...] = x1
    h2_ref[...] = _rmsnorm(x1, g_ref[...]).astype(BF16)


def _merge(x2, out_a, out_b, gates, wa, wb, wo, norm_g):
    n = x2.shape[0]
    row_block = lambda width, col=0: pl.BlockSpec((MG_TM, width), lambda i: (i, col))
    return pl.pallas_call(
        _merge_kernel,
        out_shape=(jax.ShapeDtypeStruct((n, D_MODEL), F32),
                   jax.ShapeDtypeStruct((n, D_MODEL), BF16)),
        grid=(n // MG_TM,),
        in_specs=[
            row_block(D_MODEL), row_block(A_Q_DIM), row_block(B_DIM),
            row_block(D_MODEL, 0), row_block(D_MODEL, 1),
            _resident((A_Q_DIM, D_MODEL)), _resident((B_DIM, D_MODEL)),
            _resident((D_MODEL, D_MODEL)), _resident((1, D_MODEL)),
        ],
        out_specs=(row_block(D_MODEL), row_block(D_MODEL)),
        compiler_params=_params("arbitrary"),
        name="merge",
    )(x2, out_a, out_b, gates, gates, wa, wb, wo, norm_g)


UP_TM = 1024
UP_NJ = 4
UP_TN = D_FF // UP_NJ
UP_LANES = SUB // 2
UP_NSUB = UP_TN // UP_LANES
HALO = 16
UP_ROWS = UP_TM + 2 * HALO
UP_SPLIT_TAIL = 2


def _ffn_up_kernel(hp_ref, hm_ref, hn_ref, wg_ref, wv_ref, cwg_ref, cwv_ref, cbg_ref, cbv_ref,
                   o_ref, lhs_ref, u_ref):
    i = pl.program_id(0)
    tiles_per_seq = SEQ // UP_TM

    @pl.when(pl.program_id(1) == 0)
    def _():
        first = (i % tiles_per_seq) == 0
        last = (i % tiles_per_seq) == tiles_per_seq - 1
        lhs_ref[0:HALO, :] = jnp.where(first, jnp.zeros_like(hp_ref[...]), hp_ref[...])
        lhs_ref[HALO:HALO + UP_TM, :] = hm_ref[...]
        lhs_ref[HALO + UP_TM:, :] = jnp.where(last, jnp.zeros_like(hn_ref[...]), hn_ref[...])

    def conv(slot, rows, lanes, cw_ref, cb_ref, cols, scale):
        below = u_ref[slot, HALO - 1:HALO - 1 + rows, lanes]
        mid = u_ref[slot, HALO:HALO + rows, lanes]
        above = u_ref[slot, HALO + 1:HALO + 1 + rows, lanes]
        c0, c1, c2 = (scale * cw_ref[t:t + 1, cols] for t in range(3))
        y = mid * c1 + scale * cb_ref[:, cols]
        y = y + below * c0
        return y + above * c2

    def unit(k, slot, row0, rows):
        cols = slice(k * UP_LANES, (k + 1) * UP_LANES)
        w = jnp.concatenate([wg_ref[:, cols], wv_ref[:, cols]], axis=1)
        u_ref[slot, 0:rows + 2 * HALO, :] = jnp.dot(
            lhs_ref[row0:row0 + rows + 2 * HALO, :], w, preferred_element_type=F32)
        half_gate = conv(slot, rows, slice(0, UP_LANES), cwg_ref, cbg_ref, cols, 0.5)
        val = conv(slot, rows, slice(UP_LANES, SUB), cwv_ref, cbv_ref, cols, 1.0)
        silu = half_gate * jnp.tanh(half_gate) + half_gate
        o_ref[row0:row0 + rows, cols] = (silu * val).astype(BF16)

    units = []
    for k in range(UP_NSUB):
        if k < UP_NSUB - UP_SPLIT_TAIL:
            units.append((k, 0, UP_TM))
        else:
            units += [(k, 0, UP_TM // 2), (k, UP_TM // 2, UP_TM // 2)]
    for slot, (k, row0, rows) in enumerate(units):
        unit(k, slot % 2, row0, rows)


def _ffn_up(h2, w_up, conv_w, conv_b):
    n = h2.shape[0]
    halo_blocks = UP_TM // HALO
    last_halo = n // HALO - 1
    return pl.pallas_call(
        _ffn_up_kernel,
        out_shape=jax.ShapeDtypeStruct((n, D_FF), BF16),
        grid=(n // UP_TM, UP_NJ),
        in_specs=[
            pl.BlockSpec((HALO, D_MODEL), lambda i, j: (jnp.maximum(i * halo_blocks - 1, 0), 0)),
            pl.BlockSpec((UP_TM, D_MODEL), lambda i, j: (i, 0)),
            pl.BlockSpec((HALO, D_MODEL),
                         lambda i, j: (jnp.minimum((i + 1) * halo_blocks, last_halo), 0)),
            pl.BlockSpec((D_MODEL, UP_TN), lambda i, j: (0, j)),
            pl.BlockSpec((D_MODEL, UP_TN), lambda i, j: (0, UP_NJ + j)),
            pl.BlockSpec((3, UP_TN), lambda i, j: (0, j)),
            pl.BlockSpec((3, UP_TN), lambda i, j: (0, UP_NJ + j)),
            pl.BlockSpec((1, UP_TN), lambda i, j: (0, j)),
            pl.BlockSpec((1, UP_TN), lambda i, j: (0, UP_NJ + j)),
        ],
        out_specs=pl.BlockSpec((UP_TM, UP_TN), lambda i, j: (i, j)),
        scratch_shapes=[pltpu.VMEM((UP_ROWS, D_MODEL), BF16),
                        pltpu.VMEM((2, UP_ROWS, SUB), F32)],
        compiler_params=_params("arbitrary", "arbitrary"),
        name="ffn_up",
    )(h2, h2, h2, w_up, w_up, conv_w, conv_w, conv_b, conv_b)


DN_TM = 512


def _ffn_down_kernel(a_ref, w_ref, x_ref, o_ref):
    lhs = a_ref[...]
    for k in range(D_MODEL // SUB):
        cols = slice(k * SUB, (k + 1) * SUB)
        o_ref[:, cols] = x_ref[:, cols] + jnp.dot(lhs, w_ref[:, cols], preferred_element_type=F32)


def _ffn_down(act, w_down, x1):
    n = act.shape[0]
    row_block = lambda width: pl.BlockSpec((DN_TM, width), lambda i: (i, 0))
    return pl.pallas_call(
        _ffn_down_kernel,
        out_shape=jax.ShapeDtypeStruct((n, D_MODEL), F32),
        grid=(n // DN_TM,),
        in_specs=[row_block(D_FF), _resident((D_FF, D_MODEL)), row_block(D_MODEL)],
        out_specs=row_block(D_MODEL),
        compiler_params=_params("arbitrary"),
        name="ffn_down",
    )(act, w_down, x1)


def _rope_tables():
    half = HEAD_DIM // 2
    pos = jnp.arange(SEQ, dtype=F32)
    inv_freq = ROPE_THETA ** (-jnp.arange(half, dtype=F32) * (2.0 / HEAD_DIM))
    ang = pos[:, None] * inv_freq[None, :]
    cos, sin = jnp.cos(ang), jnp.sin(ang)
    return jnp.concatenate([cos, cos], axis=-1), jnp.concatenate([sin, sin], axis=-1)


def _paired_gain(gain):
    lo, hi = gain[:HALF_LANES], gain[HALF_LANES:]
    return jnp.stack([jnp.concatenate([lo, lo]), jnp.concatenate([hi, hi])])


IN_COL_VA = ROPE_DIM
IN_COL_QB = IN_COL_VA + A_KV_DIM
IN_COL_VB = IN_COL_QB + NORM_DIM
IN_COL_GATE = IN_COL_VB + B_DIM
IN_HEAD_OUTPUTS = [ROPE_COLUMNS, [(IN_COL_QB, IN_COL_VB)], [(IN_COL_VB, IN_COL_GATE)],
                   [(IN_COL_VA, IN_COL_QB)]]


def kernel(x, norm_mix, w_in, a_q_norm, a_k_norm, a_sink, b_q_norm, b_k_norm, b_rpb,
           w_branch_a, w_branch_b, w_out, norm_ffn, w_up, conv_w, conv_b, w_down):
    batch, seq, d_model = x.shape
    assert (seq, d_model) == (SEQ, D_MODEL)
    n = batch * seq
    cos, sin = _rope_tables()
    scale = LOG2E / math.sqrt(HEAD_DIM)
    x2 = x.reshape(n, d_model)
    for l in range(norm_mix.shape[0]):
        w_gate = w_in[l][:, IN_COL_GATE:].astype(BF16)
        gates, h, w_rope, w_norm, w_vb, w_va, w_up_bf = _in_gate(
            x2, norm_mix[l][None], w_gate, [(w_in[l], IN_HEAD_OUTPUTS), (w_up[l], None)])
        gains = (_paired_gain(a_q_norm[l] * scale), _paired_gain(a_k_norm[l]),
                 b_q_norm[l][None] * scale, b_k_norm[l][None])
        qk_a, qk_b, vals = _in_heads(h, (w_rope, w_norm, w_vb, w_va), cos, sin, gains, [])
        out_a, wa_bf, wb_bf, w_out_bf, w_down_bf = _attn_a(
            qk_a, vals, a_sink[l] * LOG2E, batch,
            [w_branch_a[l], w_branch_b[l], w_out[l], w_down[l]])
        out_b = _attn_b(qk_b.reshape(n // GRID_W, GRID_W, NORM_DIM),
                        vals.reshape(n // GRID_W, GRID_W, PLAIN_DIM), _attn_b_strips(b_rpb[l]), batch)
        x1, h2 = _merge(x2, out_a, out_b.reshape(n, B_DIM), gates, wa_bf, wb_bf, w_out_bf,
                        norm_ffn[l][None])
        act = _ffn_up(h2, w_up_bf, conv_w[l], conv_b[l][None])
        x2 = _ffn_down(act, w_down_bf, x1)
    return x2.reshape(batch, seq, d_model)
```

```python
import math

import numpy as np
import jax
import jax.numpy as jnp
from jax import lax
from jax.experimental import pallas as pl
from jax.experimental.pallas import tpu as pltpu

D_MODEL = 2048
SEQ = 4096
HEAD_DIM = 128
A_Q_HEADS = 8
A_KV_HEADS = 2
A_GROUP = A_Q_HEADS // A_KV_HEADS
WINDOW = 128
B_HEADS = 8
GRID_W = 64
WIN_H = 8
WIN_W = 16
D_FF = 5632
ROPE_THETA = 10000.0
EPS = 1e-6
NEG = -1e30
LOG2E = math.log2(math.e)

A_Q_DIM = A_Q_HEADS * HEAD_DIM
A_KV_DIM = A_KV_HEADS * HEAD_DIM
B_DIM = B_HEADS * HEAD_DIM

VMEM_LIMIT_BYTES = 56 * 1024 * 1024
MXU_COLS = 256
SUB = MXU_COLS
HEADS_PER_SUB = SUB // HEAD_DIM

BF16 = jnp.bfloat16
F32 = jnp.float32


def _params(*semantics):
    return pltpu.CompilerParams(dimension_semantics=semantics, vmem_limit_bytes=VMEM_LIMIT_BYTES)


def _resident(shape):
    return pl.BlockSpec(shape, lambda *_: (0,) * len(shape), pipeline_mode=pl.Buffered(1))


def _rmsnorm(x, gain):
    ms = jnp.mean(x * x, axis=-1, keepdims=True)
    return x * lax.rsqrt(ms + EPS) * gain


class _CastJob:
    def __init__(self, src, steps, outputs=None, step_of=lambda i: i):
        rows, cols = src.shape
        self.src = src
        self.outputs = outputs or [[(0, cols)]]
        self.slab = rows // steps
        assert self.slab * steps == rows
        self.span = max(hi for pieces in self.outputs for _, hi in pieces)
        self.widths = [sum(hi - lo for lo, hi in pieces) for pieces in self.outputs]
        self.index_map = lambda *idx: (step_of(*idx), 0)

    def in_spec(self):
        return pl.BlockSpec((self.slab, self.span), self.index_map)

    def out_specs(self):
        return [pl.BlockSpec((self.slab, width), self.index_map) for width in self.widths]

    def out_shapes(self):
        return [jax.ShapeDtypeStruct((self.src.shape[0], width), BF16) for width in self.widths]

    def run(self, src_ref, out_refs):
        for pieces, out_ref in zip(self.outputs, out_refs):
            parts = [src_ref[:, lo:hi] for lo, hi in pieces]
            value = parts[0] if len(parts) == 1 else jnp.concatenate(parts, axis=1)
            out_ref[...] = value.astype(BF16)


def _with_casts(body, jobs, n_in, n_out):
    n_cast_out = sum(len(j.outputs) for j in jobs)

    def wrapped(*refs):
        ins = refs[:n_in]
        srcs = refs[n_in:n_in + len(jobs)]
        outs = refs[n_in + len(jobs):n_in + len(jobs) + n_out]
        cast_outs = list(refs[n_in + len(jobs) + n_out:n_in + len(jobs) + n_out + n_cast_out])
        scratch = refs[n_in + len(jobs) + n_out + n_cast_out:]
        for job, src_ref in zip(jobs, srcs):
            job.run(src_ref, [cast_outs.pop(0) for _ in job.outputs])
        body(*ins, *outs, *scratch)

    return wrapped


def _sigmoid(x):
    return 0.5 * jnp.tanh(0.5 * x) + 0.5


IN_TM = 1024
GATE_TM = 512
GATE_DIM = 2 * D_MODEL
ROPE_DIM = A_Q_DIM + A_KV_DIM
NORM_DIM = 2 * B_DIM
PLAIN_DIM = B_DIM + A_KV_DIM


def _in_gate_kernel(x_ref, g_ref, w_ref, o_ref, h_ref):
    h_ref[...] = _rmsnorm(x_ref[...], g_ref[...]).astype(BF16)
    lhs = h_ref[...]
    for k in range(GATE_DIM // SUB):
        cols = slice(k * SUB, (k + 1) * SUB)
        acc = jnp.dot(lhs, w_ref[:, cols], preferred_element_type=F32)
        o_ref[:, cols] = _sigmoid(acc).astype(BF16)


def _in_gate(x2, norm_g, w_gate, casts):
    n = x2.shape[0]
    steps = n // GATE_TM
    jobs = [_CastJob(src, steps, ranges) for src, ranges in casts]
    row_block = lambda width: pl.BlockSpec((GATE_TM, width), lambda i: (i, 0))
    return pl.pallas_call(
        _with_casts(_in_gate_kernel, jobs, n_in=3, n_out=2),
        out_shape=[jax.ShapeDtypeStruct((n, GATE_DIM), BF16),
                   jax.ShapeDtypeStruct((n, D_MODEL), BF16)]
                  + [s for j in jobs for s in j.out_shapes()],
        grid=(steps,),
        in_specs=[row_block(D_MODEL), _resident((1, D_MODEL)), _resident((D_MODEL, GATE_DIM))]
                 + [j.in_spec() for j in jobs],
        out_specs=[row_block(GATE_DIM), row_block(D_MODEL)]
                  + [s for j in jobs for s in j.out_specs()],
        compiler_params=_params("arbitrary"),
        name="in_gate",
    )(x2, norm_g, w_gate, *[j.src for j in jobs])


def _store_heads(o_ref, k, acc, fn):
    for t in range(HEADS_PER_SUB):
        col = k * SUB + t * HEAD_DIM
        o_ref[:, col:col + HEAD_DIM] = fn(acc[:, t * HEAD_DIM:(t + 1) * HEAD_DIM]).astype(BF16)


def _pair_columns(head_a, head_b):
    half = HEAD_DIM // 2
    a, b = head_a * HEAD_DIM, head_b * HEAD_DIM
    return [(a, a + half), (b, b + half), (a + half, a + HEAD_DIM), (b + half, b + HEAD_DIM)]


ROPE_PAIRS = [(p, p + A_GROUP) for p in range(A_GROUP)] + [(A_Q_HEADS, A_Q_HEADS + 1)]
assert A_KV_HEADS == 2 and len(ROPE_PAIRS) * SUB == ROPE_DIM
ROPE_COLUMNS = [piece for pair in ROPE_PAIRS for piece in _pair_columns(*pair)]
HALF_LANES = HEAD_DIM // 2


def _in_rope_kernel(h_ref, w_ref, cos_ref, sin_ref, gq_ref, gk_ref, o_ref):
    lhs = h_ref[...]
    first_head = lax.broadcasted_iota(jnp.int32, (IN_TM, HEAD_DIM), 1) < HALF_LANES
    for k in range(ROPE_DIM // SUB):
        acc = jnp.dot(lhs, w_ref[:, k * SUB:(k + 1) * SUB], preferred_element_type=F32)
        gain_ref = gq_ref if k * SUB < A_Q_DIM else gk_ref
        lo, hi = acc[:, :HEAD_DIM], acc[:, HEAD_DIM:]
        sq = lo * lo + hi * hi
        ss_a = jnp.sum(jnp.where(first_head, sq, 0.0), axis=-1, keepdims=True)
        ss_b = jnp.sum(jnp.where(first_head, 0.0, sq), axis=-1, keepdims=True)
        inv = jnp.where(first_head, lax.rsqrt(ss_a * (1.0 / HEAD_DIM) + EPS),
                        lax.rsqrt(ss_b * (1.0 / HEAD_DIM) + EPS))
        lo = lo * inv * gain_ref[0:1, :]
        hi = hi * inv * gain_ref[1:2, :]
        cos, sin = cos_ref[...], sin_ref[...]
        o_ref[:, k * SUB:k * SUB + HEAD_DIM] = (lo * cos - hi * sin).astype(BF16)
        o_ref[:, k * SUB + HEAD_DIM:(k + 1) * SUB] = (hi * cos + lo * sin).astype(BF16)


def _in_norm_kernel(h_ref, w_ref, gq_ref, gk_ref, o_ref):
    lhs = h_ref[...]
    for k in range(NORM_DIM // SUB):
        acc = jnp.dot(lhs, w_ref[:, k * SUB:(k + 1) * SUB], preferred_element_type=F32)
        gain_ref = gq_ref if k * SUB < B_DIM else gk_ref
        _store_heads(o_ref, k, acc, lambda a: _rmsnorm(a, gain_ref[...]))


def _in_plain_kernel(h_ref, wvb_ref, wva_ref, o_ref):
    lhs = h_ref[...]
    for k in range(PLAIN_DIM // SUB):
        cols = slice(k * SUB, (k + 1) * SUB)
        w = wvb_ref[:, cols] if k * SUB < B_DIM else wva_ref[:, k * SUB - B_DIM:(k + 1) * SUB - B_DIM]
        o_ref[:, cols] = jnp.dot(lhs, w, preferred_element_type=F32).astype(BF16)


def _in_heads_kernel(h_ref, w_rope_ref, w_norm_ref, w_vb_ref, w_va_ref, cos_ref, sin_ref,
                     gqa_ref, gka_ref, gqb_ref, gkb_ref, rope_ref, norm_ref, plain_ref):
    _in_rope_kernel(h_ref, w_rope_ref, cos_ref, sin_ref, gqa_ref, gka_ref, rope_ref)
    _in_norm_kernel(h_ref, w_norm_ref, gqb_ref, gkb_ref, norm_ref)
    _in_plain_kernel(h_ref, w_vb_ref, w_va_ref, plain_ref)


def _in_heads(h, weights, cos, sin, gains, casts):
    n = h.shape[0]
    steps = n // IN_TM
    seq_tiles = SEQ // IN_TM
    jobs = [_CastJob(src, steps) for src in casts]
    row_block = lambda width: pl.BlockSpec((IN_TM, width), lambda i: (i, 0))
    pos_spec = pl.BlockSpec((IN_TM, HEAD_DIM), lambda i: (i % seq_tiles, 0))
    widths = (ROPE_DIM, NORM_DIM, PLAIN_DIM)
    return pl.pallas_call(
        _with_casts(_in_heads_kernel, jobs, n_in=11, n_out=3),
        out_shape=[jax.ShapeDtypeStruct((n, width), BF16) for width in widths]
                  + [s for j in jobs for s in j.out_shapes()],
        grid=(steps,),
        in_specs=([row_block(D_MODEL)] + [_resident(w.shape) for w in weights]
                  + [pos_spec, pos_spec] + [_resident(g.shape) for g in gains]
                  + [j.in_spec() for j in jobs]),
        out_specs=[row_block(width) for width in widths] + [s for j in jobs for s in j.out_specs()],
        compiler_params=_params("arbitrary"),
        name="in_heads",
    )(h, *weights, cos, sin, *gains, *[j.src for j in jobs])


A_TQ = 1024
A_BLK = WINDOW
A_NB = A_TQ // A_BLK
A_CHUNKS = SEQ // A_TQ


def _attn_a_kernel(sink_ref, q_ref, kp_ref, km_ref, kn_ref, vp_ref, vm_ref, vn_ref, o_ref):
    c = pl.program_id(1)
    k_all = jnp.concatenate([kp_ref[...], km_ref[...], kn_ref[...]], axis=0)
    v_all = jnp.concatenate([vp_ref[...], vm_ref[...], vn_ref[...]], axis=0)
    qq = lax.broadcasted_iota(jnp.int32, (A_BLK, 3 * A_BLK), 0)
    kk = lax.broadcasted_iota(jnp.int32, (A_BLK, 3 * A_BLK), 1)
    d = kk - qq
    band = (d >= 0) & (d <= 2 * WINDOW)
    ones = jnp.ones((3 * A_BLK, HEAD_DIM), BF16)
    lane = lax.broadcasted_iota(jnp.int32, (A_BLK, SUB), 1) % HEAD_DIM
    own_lanes = (lane < HALF_LANES, lane >= HALF_LANES)
    for n in range(A_NB):
        valid = band
        if n == 0:
            valid = valid & ((kk >= A_BLK) | (c > 0))
        if n == A_NB - 1:
            valid = valid & ((kk < 2 * A_BLK) | (c < A_CHUNKS - 1))
        k = k_all[n * A_BLK:(n + 3) * A_BLK, :]
        q_pairs = [q_ref[n * A_BLK:(n + 1) * A_BLK, p * SUB:(p + 1) * SUB] for p in range(A_GROUP)]
        qs = jnp.concatenate([jnp.where(own_lanes[h], q, jnp.zeros_like(q))
                              for h in range(A_KV_HEADS) for q in q_pairs], axis=0)
        s_all = lax.dot_general(qs, k, (((1,), (1,)), ((), ())),
                                preferred_element_type=F32)
        for h in range(A_KV_HEADS):
            v = v_all[n * A_BLK:(n + 3) * A_BLK, h * HEAD_DIM:(h + 1) * HEAD_DIM]
            heads = [h * A_GROUP + g for g in range(A_GROUP)]
            s = s_all[h * A_GROUP * A_BLK:(h + 1) * A_GROUP * A_BLK]
            ps, sink_terms = [], []
            for g, t in enumerate(heads):
                sg = jnp.where(valid, s[g * A_BLK:(g + 1) * A_BLK], NEG)
                m = jnp.maximum(jnp.max(sg, axis=-1, keepdims=True), sink_ref[t])
                ps.append(jnp.exp2(sg - m).astype(BF16))
                sink_terms.append(jnp.exp2(sink_ref[t] - m))
            o = jnp.dot(jnp.concatenate(ps, axis=0), jnp.concatenate([v, ones], axis=1),
                        preferred_element_type=F32)
            for g, t in enumerate(heads):
                og = o[g * A_BLK:(g + 1) * A_BLK]
                og = og[:, :HEAD_DIM] / (og[:, HEAD_DIM:] + sink_terms[g])
                o_ref[n * A_BLK:(n + 1) * A_BLK, t * HEAD_DIM:(t + 1) * HEAD_DIM] = og.astype(BF16)


def _attn_a(qk, vals, sink, batch, casts):
    n = qk.shape[0]
    blk_per_seq = SEQ // A_BLK
    k_col = A_Q_DIM // A_KV_DIM
    v_col = B_DIM // A_KV_DIM
    jobs = [_CastJob(src, batch * A_CHUNKS, step_of=lambda b, c, *_: b * A_CHUNKS + c)
            for src in casts]

    def prev_map(col):
        return lambda b, c, *_: (b * blk_per_seq + jnp.maximum(c * A_NB - 1, 0), col)

    def main_map(col):
        return lambda b, c, *_: (b * A_CHUNKS + c, col)

    def next_map(col):
        return lambda b, c, *_: (b * blk_per_seq + jnp.minimum(c * A_NB + A_NB, blk_per_seq - 1), col)

    halo = lambda m: pl.BlockSpec((A_BLK, A_KV_DIM), m)
    main = lambda m: pl.BlockSpec((A_TQ, A_KV_DIM), m)
    return pl.pallas_call(
        _with_casts(_attn_a_kernel, jobs, n_in=8, n_out=1),
        out_shape=[jax.ShapeDtypeStruct((n, A_Q_DIM), BF16)] + [s for j in jobs for s in j.out_shapes()],
        grid_spec=pltpu.PrefetchScalarGridSpec(
            num_scalar_prefetch=1,
            grid=(batch, A_CHUNKS),
            in_specs=[
                pl.BlockSpec((A_TQ, A_Q_DIM), main_map(0)),
                halo(prev_map(k_col)), main(main_map(k_col)), halo(next_map(k_col)),
                halo(prev_map(v_col)), main(main_map(v_col)), halo(next_map(v_col)),
            ] + [j.in_spec() for j in jobs],
            out_specs=[pl.BlockSpec((A_TQ, A_Q_DIM), main_map(0))]
                      + [s for j in jobs for s in j.out_specs()],
        ),
        compiler_params=_params("arbitrary", "arbitrary"),
        name="attn_a",
    )(sink, qk, qk, qk, qk, vals, vals, vals, *[j.src for j in jobs])


GRID_ROWS = SEQ // GRID_W
B_QR = 4
B_KR = B_QR + WIN_H
B_NRB = GRID_ROWS // B_QR
B_NQ = B_QR * GRID_W
B_NK = B_KR * GRID_W
B_KBLOCKS = B_KR // B_QR
B_KS_MAX = B_NRB - B_KBLOCKS
B_STEP_BLOCKS = 4
B_STEP_KBLOCKS = B_KBLOCKS + B_STEP_BLOCKS - 1
B_PATTERN_ROWBLOCKS = (0, 1, B_NRB - 1)


def _attn_b_row_windows():
    starts, row_ok = [], []
    for rb in B_PATTERN_ROWBLOCKS:
        ks = int(np.clip(rb - 1, 0, B_KS_MAX))
        qr = rb * B_QR + np.arange(B_QR)
        kr = ks * B_QR + np.arange(B_KR)
        rs = np.clip(qr - WIN_H // 2, 0, GRID_ROWS - WIN_H)
        row_ok.append((kr[None, :] >= rs[:, None]) & (kr[None, :] < rs[:, None] + WIN_H))
        starts.append(kr[0] - qr + WIN_H - 1)
    return np.stack(starts), np.stack(row_ok)


B_ROW_START, B_ROW_OK = _attn_b_row_windows()
B_STRIP_LO = max(0, -int(B_ROW_START.min()))
B_NSTRIP = int(B_ROW_START.max()) + B_KR - 1 + B_STRIP_LO
B_LANES = 2 * GRID_W


def _attn_b_build_bias(strip_ref, bias_ref):
    qc = lax.broadcasted_iota(jnp.int32, (GRID_W, B_LANES), 0)
    lane = lax.broadcasted_iota(jnp.int32, (GRID_W, B_LANES), 1)
    kc = lane % GRID_W
    cs = jnp.clip(qc - WIN_W // 2, 0, GRID_W - WIN_W)
    col_ok = (kc >= cs) & (kc < cs + WIN_W)
    first_row = lane < GRID_W
    for h in range(B_HEADS):
        tiles = {}

        def tile(i):
            if i not in tiles:
                rows = jnp.broadcast_to(strip_ref[h, i:i + 1, :], (GRID_W, B_LANES))
                toeplitz = pltpu.roll(rows, 0, axis=1, stride=1, stride_axis=0)
                tiles[i] = jnp.where(col_ok, toeplitz, NEG)
            return tiles[i]

        for p in range(len(B_PATTERN_ROWBLOCKS)):
            for qr in range(B_QR):
                for c in range(B_KR // 2):
                    ok0, ok1 = B_ROW_OK[p, qr, 2 * c], B_ROW_OK[p, qr, 2 * c + 1]
                    if ok0 or ok1:
                        t = tile(int(B_ROW_START[p, qr]) + 2 * c + B_STRIP_LO)
                        if not ok1:
                            t = jnp.where(first_row, t, NEG)
                        if not ok0:
                            t = jnp.where(first_row, NEG, t)
                    else:
                        t = jnp.full((GRID_W, B_LANES), NEG, F32)
                    bias_ref[p, h, qr * GRID_W:(qr + 1) * GRID_W, c * B_LANES:(c + 1) * B_LANES] = t


def _attn_b_key_base(step):
    return jnp.clip(step * B_STEP_BLOCKS - 1, 0, B_NRB - B_STEP_KBLOCKS)


def _attn_b_kernel(q_ref, *refs):
    k_refs = refs[:B_STEP_KBLOCKS]
    v_refs = refs[B_STEP_KBLOCKS:2 * B_STEP_KBLOCKS]
    strip_ref, o_ref, bias_ref, k_buf, v_buf = refs[2 * B_STEP_KBLOCKS:]
    step = pl.program_id(1)

    @pl.when((pl.program_id(0) == 0) & (step == 0))
    def _():
        _attn_b_build_bias(strip_ref, bias_ref)

    for t in range(B_STEP_KBLOCKS):
        k_buf[t * B_QR:(t + 1) * B_QR] = k_refs[t][...]
        v_buf[t * B_QR:(t + 1) * B_QR] = v_refs[t][...]

    ones = jnp.ones((B_NK, HEAD_DIM), BF16)
    for sb in range(B_STEP_BLOCKS):
        r = step * B_STEP_BLOCKS + sb
        pattern = jnp.where(r == 0, 0, jnp.where(r == B_NRB - 1, 2, 1))
        first_row = (jnp.clip(r - 1, 0, B_KS_MAX) - _attn_b_key_base(step)) * B_QR
        rows = slice(sb * B_QR, (sb + 1) * B_QR)
        for h in range(B_HEADS):
            hd = slice(h * HEAD_DIM, (h + 1) * HEAD_DIM)
            q = q_ref[rows, :, hd].reshape(B_NQ, HEAD_DIM)
            k = k_buf[pl.ds(first_row, B_KR), :, hd].reshape(B_NK, HEAD_DIM)
            v = v_buf[pl.ds(first_row, B_KR), :, hd].reshape(B_NK, HEAD_DIM)
            s = lax.dot_general(q, k, (((1,), (1,)), ((), ())),
                                preferred_element_type=F32) + bias_ref[pattern, h]
            m = jnp.max(s, axis=-1, keepdims=True)
            p = jnp.exp2(s - m).astype(BF16)
            o = jnp.dot(p, jnp.concatenate([v, ones], axis=1), preferred_element_type=F32)
            o = o[:, :HEAD_DIM] / o[:, HEAD_DIM:]
            o_ref[rows, :, hd] = o.reshape(B_QR, GRID_W, HEAD_DIM).astype(BF16)


def _attn_b_strips(rpb):
    rows = B_NSTRIP + 1
    table = jnp.pad(rpb.astype(F32) * LOG2E,
                    ((0, 0), (B_STRIP_LO, rows - B_STRIP_LO - rpb.shape[1]), (0, 0)))
    first, second = table[:, :-1], table[:, 1:]
    gap = jnp.zeros(first.shape[:2] + (GRID_W - 2 * WIN_W + 1,), F32)
    strips = jnp.concatenate(
        [first[..., WIN_W - 1:], gap, second, gap, first[..., :WIN_W - 1]], axis=-1)
    assert strips.shape == (B_HEADS, B_NSTRIP, B_LANES)
    return strips


def _attn_b(qk3, vals3, strips, batch):
    rows = qk3.shape[0]

    steps = B_NRB // B_STEP_BLOCKS

    def kv_spec(col, t):
        return pl.BlockSpec((B_QR, GRID_W, B_DIM),
                            lambda b, s: (b * B_NRB + _attn_b_key_base(s) + t, 0, col))

    q_spec = pl.BlockSpec((B_STEP_BLOCKS * B_QR, GRID_W, B_DIM), lambda b, s: (b * steps + s, 0, 0))
    window = (B_STEP_KBLOCKS * B_QR, GRID_W, B_DIM)
    return pl.pallas_call(
        _attn_b_kernel,
        out_shape=jax.ShapeDtypeStruct((rows, GRID_W, B_DIM), BF16),
        grid=(batch, steps),
        in_specs=([q_spec] + [kv_spec(1, t) for t in range(B_STEP_KBLOCKS)]
                  + [kv_spec(0, t) for t in range(B_STEP_KBLOCKS)]
                  + [_resident((B_HEADS, B_NSTRIP, B_LANES))]),
        out_specs=q_spec,
        scratch_shapes=[pltpu.VMEM((len(B_PATTERN_ROWBLOCKS), B_HEADS, B_NQ, B_NK), F32),
                        pltpu.VMEM(window, BF16), pltpu.VMEM(window, BF16)],
        compiler_params=_params("arbitrary", "arbitrary"),
        name="attn_b",
    )(qk3, *[qk3] * B_STEP_KBLOCKS, *[vals3] * B_STEP_KBLOCKS, strips)


MG_TM = 512


def _merge_kernel(x_ref, oa_ref, ob_ref, ga_ref, gb_ref, wa_ref, wb_ref, wo_ref, g_ref,
                  x1_ref, h2_ref):
    a = jnp.dot(oa_ref[...], wa_ref[...], preferred_element_type=F32)
    b = jnp.dot(ob_ref[...], wb_ref[...], preferred_element_type=F32)
    merged = ga_ref[...].astype(F32) * a + gb_ref[...].astype(F32) * b
    y = jnp.dot(merged.astype(BF16), wo_ref[...], preferred_element_type=F32)
    x1 = x_ref[...] + y
    x1_ref[...] = x1
    h2_ref[...] = _rmsnorm(x1, g_ref[...]).astype(BF16)


def _merge(x2, out_a, out_b, gates, wa, wb, wo, norm_g):
    n = x2.shape[0]
    row_block = lambda width, col=0: pl.BlockSpec((MG_TM, width), lambda i: (i, col))
    return pl.pallas_call(
        _merge_kernel,
        out_shape=(jax.ShapeDtypeStruct((n, D_MODEL), F32),
                   jax.ShapeDtypeStruct((n, D_MODEL), BF16)),
        grid=(n // MG_TM,),
        in_specs=[
            row_block(D_MODEL), row_block(A_Q_DIM), row_block(B_DIM),
            row_block(D_MODEL, 0), row_block(D_MODEL, 1),
            _resident((A_Q_DIM, D_MODEL)), _resident((B_DIM, D_MODEL)),
            _resident((D_MODEL, D_MODEL)), _resident((1, D_MODEL)),
        ],
        out_specs=(row_block(D_MODEL), row_block(D_MODEL)),
        compiler_params=_params("arbitrary"),
        name="merge",
    )(x2, out_a, out_b, gates, gates, wa, wb, wo, norm_g)


UP_TM = 1024
UP_NJ = 4
UP_TN = D_FF // UP_NJ
UP_LANES = SUB // 2
UP_NSUB = UP_TN // UP_LANES
HALO_FETCH = 16
HALO = 8
UP_ROWS = UP_TM + 2 * HALO
UP_SPLIT_TAIL = 2


def _ffn_up_kernel(hp_ref, hm_ref, hn_ref, wg_ref, wv_ref, cwg_ref, cwv_ref, cbg_ref, cbv_ref,
                   o_ref, lhs_ref, u_ref):
    i = pl.program_id(0)
    tiles_per_seq = SEQ // UP_TM

    @pl.when(pl.program_id(1) == 0)
    def _():
        first = (i % tiles_per_seq) == 0
        last = (i % tiles_per_seq) == tiles_per_seq - 1
        above_rows, below_rows = hp_ref[HALO_FETCH - HALO:, :], hn_ref[:HALO, :]
        lhs_ref[0:HALO, :] = jnp.where(first, jnp.zeros_like(above_rows), above_rows)
        lhs_ref[HALO:HALO + UP_TM, :] = hm_ref[...]
        lhs_ref[HALO + UP_TM:, :] = jnp.where(last, jnp.zeros_like(below_rows), below_rows)

    def conv(slot, rows, lanes, cw_ref, cb_ref, cols, scale):
        below = u_ref[slot, HALO - 1:HALO - 1 + rows, lanes]
        mid = u_ref[slot, HALO:HALO + rows, lanes]
        above = u_ref[slot, HALO + 1:HALO + 1 + rows, lanes]
        c0, c1, c2 = (scale * cw_ref[t:t + 1, cols] for t in range(3))
        y = mid * c1 + scale * cb_ref[:, cols]
        y = y + below * c0
        return y + above * c2

    def unit(k, slot, row0, rows):
        cols = slice(k * UP_LANES, (k + 1) * UP_LANES)
        w = jnp.concatenate([wg_ref[:, cols], wv_ref[:, cols]], axis=1)
        u_ref[slot, 0:rows + 2 * HALO, :] = jnp.dot(
            lhs_ref[row0:row0 + rows + 2 * HALO, :], w, preferred_element_type=F32)
        half_gate = conv(slot, rows, slice(0, UP_LANES), cwg_ref, cbg_ref, cols, 0.5)
        val = conv(slot, rows, slice(UP_LANES, SUB), cwv_ref, cbv_ref, cols, 1.0)
        silu = half_gate * jnp.tanh(half_gate) + half_gate
        o_ref[row0:row0 + rows, cols] = (silu * val).astype(BF16)

    units = []
    for k in range(UP_NSUB):
        if k < UP_NSUB - UP_SPLIT_TAIL:
            units.append((k, 0, UP_TM))
        else:
            units += [(k, 0, UP_TM // 2), (k, UP_TM // 2, UP_TM // 2)]
    for slot, (k, row0, rows) in enumerate(units):
        unit(k, slot % 2, row0, rows)


def _ffn_up(h2, w_up, conv_w, conv_b):
    n = h2.shape[0]
    halo_blocks = UP_TM // HALO_FETCH
    last_halo = n // HALO_FETCH - 1
    return pl.pallas_call(
        _ffn_up_kernel,
        out_shape=jax.ShapeDtypeStruct((n, D_FF), BF16),
        grid=(n // UP_TM, UP_NJ),
        in_specs=[
            pl.BlockSpec((HALO_FETCH, D_MODEL), lambda i, j: (jnp.maximum(i * halo_blocks - 1, 0), 0)),
            pl.BlockSpec((UP_TM, D_MODEL), lambda i, j: (i, 0)),
            pl.BlockSpec((HALO_FETCH, D_MODEL),
                         lambda i, j: (jnp.minimum((i + 1) * halo_blocks, last_halo), 0)),
            pl.BlockSpec((D_MODEL, UP_TN), lambda i, j: (0, j)),
            pl.BlockSpec((D_MODEL, UP_TN), lambda i, j: (0, UP_NJ + j)),
            pl.BlockSpec((3, UP_TN), lambda i, j: (0, j)),
            pl.BlockSpec((3, UP_TN), lambda i, j: (0, UP_NJ + j)),
            pl.BlockSpec((1, UP_TN), lambda i, j: (0, j)),
            pl.BlockSpec((1, UP_TN), lambda i, j: (0, UP_NJ + j)),
        ],
        out_specs=pl.BlockSpec((UP_TM, UP_TN), lambda i, j: (i, j)),
        scratch_shapes=[pltpu.VMEM((UP_ROWS, D_MODEL), BF16),
                        pltpu.VMEM((2, UP_ROWS, SUB), F32)],
        compiler_params=_params("arbitrary", "arbitrary"),
        name="ffn_up",
    )(h2, h2, h2, w_up, w_up, conv_w, conv_w, conv_b, conv_b)


DN_TM = 512


def _ffn_down_kernel(a_ref, w_ref, x_ref, o_ref):
    lhs = a_ref[...]
    for k in range(D_MODEL // SUB):
        cols = slice(k * SUB, (k + 1) * SUB)
        o_ref[:, cols] = x_ref[:, cols] + jnp.dot(lhs, w_ref[:, cols], preferred_element_type=F32)


def _ffn_down(act, w_down, x1):
    n = act.shape[0]
    row_block = lambda width: pl.BlockSpec((DN_TM, width), lambda i: (i, 0))
    return pl.pallas_call(
        _ffn_down_kernel,
        out_shape=jax.ShapeDtypeStruct((n, D_MODEL), F32),
        grid=(n // DN_TM,),
        in_specs=[row_block(D_FF), _resident((D_FF, D_MODEL)), row_block(D_MODEL)],
        out_specs=row_block(D_MODEL),
        compiler_params=_params("arbitrary"),
        name="ffn_down",
    )(act, w_down, x1)


def _rope_tables():
    half = HEAD_DIM // 2
    pos = jnp.arange(SEQ, dtype=F32)
    inv_freq = ROPE_THETA ** (-jnp.arange(half, dtype=F32) * (2.0 / HEAD_DIM))
    ang = pos[:, None] * inv_freq[None, :]
    cos, sin = jnp.cos(ang), jnp.sin(ang)
    return jnp.concatenate([cos, cos], axis=-1), jnp.concatenate([sin, sin], axis=-1)


def _paired_gain(gain):
    lo, hi = gain[:HALF_LANES], gain[HALF_LANES:]
    return jnp.stack([jnp.concatenate([lo, lo]), jnp.concatenate([hi, hi])])


IN_COL_VA = ROPE_DIM
IN_COL_QB = IN_COL_VA + A_KV_DIM
IN_COL_VB = IN_COL_QB + NORM_DIM
IN_COL_GATE = IN_COL_VB + B_DIM
IN_HEAD_OUTPUTS = [ROPE_COLUMNS, [(IN_COL_QB, IN_COL_VB)], [(IN_COL_VB, IN_COL_GATE)],
                   [(IN_COL_VA, IN_COL_QB)]]


def kernel(x, norm_mix, w_in, a_q_norm, a_k_norm, a_sink, b_q_norm, b_k_norm, b_rpb,
           w_branch_a, w_branch_b, w_out, norm_ffn, w_up, conv_w, conv_b, w_down):
    batch, seq, d_model = x.shape
    assert (seq, d_model) == (SEQ, D_MODEL)
    n = batch * seq
    cos, sin = _rope_tables()
    scale = LOG2E / math.sqrt(HEAD_DIM)
    x2 = x.reshape(n, d_model)
    for l in range(norm_mix.shape[0]):
        w_gate = w_in[l][:, IN_COL_GATE:].astype(BF16)
        gates, h, w_rope, w_norm, w_vb, w_va, w_up_bf = _in_gate(
            x2, norm_mix[l][None], w_gate, [(w_in[l], IN_HEAD_OUTPUTS), (w_up[l], None)])
        gains = (_paired_gain(a_q_norm[l] * scale), _paired_gain(a_k_norm[l]),
                 b_q_norm[l][None] * scale, b_k_norm[l][None])
        qk_a, qk_b, vals = _in_heads(h, (w_rope, w_norm, w_vb, w_va), cos, sin, gains, [])
        out_a, wa_bf, wb_bf, w_out_bf, w_down_bf = _attn_a(
            qk_a, vals, a_sink[l] * LOG2E, batch,
            [w_branch_a[l], w_branch_b[l], w_out[l], w_down[l]])
        out_b = _attn_b(qk_b.reshape(n // GRID_W, GRID_W, NORM_DIM),
                        vals.reshape(n // GRID_W, GRID_W, PLAIN_DIM), _attn_b_strips(b_rpb[l]), batch)
        x1, h2 = _merge(x2, out_a, out_b.reshape(n, B_DIM), gates, wa_bf, wb_bf, w_out_bf,
                        norm_ffn[l][None])
        act = _ffn_up(h2, w_up_bf, conv_w[l], conv_b[l][None])
        x2 = _ffn_down(act, w_down_bf, x1)
    return x2.reshape(batch, seq, d_model)
```

```python
import math

import numpy as np
import jax
import jax.numpy as jnp
from jax import lax
from jax.experimental import pallas as pl
from jax.experimental.pallas import tpu as pltpu

D_MODEL = 2048
SEQ = 4096
HEAD_DIM = 128
A_Q_HEADS = 8
A_KV_HEADS = 2
A_GROUP = A_Q_HEADS // A_KV_HEADS
WINDOW = 128
B_HEADS = 8
GRID_W = 64
WIN_H = 8
WIN_W = 16
D_FF = 5632
ROPE_THETA = 10000.0
EPS = 1e-6
NEG = -1e30
LOG2E = math.log2(math.e)

A_Q_DIM = A_Q_HEADS * HEAD_DIM
A_KV_DIM = A_KV_HEADS * HEAD_DIM
B_DIM = B_HEADS * HEAD_DIM

VMEM_LIMIT_BYTES = 56 * 1024 * 1024
MXU_COLS = 256
SUB = MXU_COLS
HEADS_PER_SUB = SUB // HEAD_DIM

BF16 = jnp.bfloat16
F32 = jnp.float32


def _params(*semantics):
    return pltpu.CompilerParams(dimension_semantics=semantics, vmem_limit_bytes=VMEM_LIMIT_BYTES)


def _resident(shape):
    return pl.BlockSpec(shape, lambda *_: (0,) * len(shape), pipeline_mode=pl.Buffered(1))


def _rmsnorm(x, gain):
    ms = jnp.mean(x * x, axis=-1, keepdims=True)
    return x * lax.rsqrt(ms + EPS) * gain


class _CastJob:
    def __init__(self, src, steps, outputs=None, step_of=lambda i: i):
        rows, cols = src.shape
        self.src = src
        self.outputs = outputs or [[(0, cols)]]
        self.slab = rows // steps
        assert self.slab * steps == rows
        self.span = max(hi for pieces in self.outputs for _, hi in pieces)
        self.widths = [sum(hi - lo for lo, hi in pieces) for pieces in self.outputs]
        self.index_map = lambda *idx: (step_of(*idx), 0)

    def in_spec(self):
        return pl.BlockSpec((self.slab, self.span), self.index_map)

    def out_specs(self):
        return [pl.BlockSpec((self.slab, width), self.index_map) for width in self.widths]

    def out_shapes(self):
        return [jax.ShapeDtypeStruct((self.src.shape[0], width), BF16) for width in self.widths]

    def run(self, src_ref, out_refs):
        for pieces, out_ref in zip(self.outputs, out_refs):
            parts = [src_ref[:, lo:hi] for lo, hi in pieces]
            value = parts[0] if len(parts) == 1 else jnp.concatenate(parts, axis=1)
            out_ref[...] = value.astype(BF16)


def _with_casts(body, jobs, n_in, n_out):
    n_cast_out = sum(len(j.outputs) for j in jobs)

    def wrapped(*refs):
        ins = refs[:n_in]
        srcs = refs[n_in:n_in + len(jobs)]
        outs = refs[n_in + len(jobs):n_in + len(jobs) + n_out]
        cast_outs = list(refs[n_in + len(jobs) + n_out:n_in + len(jobs) + n_out + n_cast_out])
        scratch = refs[n_in + len(jobs) + n_out + n_cast_out:]
        for job, src_ref in zip(jobs, srcs):
            job.run(src_ref, [cast_outs.pop(0) for _ in job.outputs])
        body(*ins, *outs, *scratch)

    return wrapped


def _sigmoid(x):
    return 0.5 * jnp.tanh(0.5 * x) + 0.5


IN_TM = 1024
GATE_TM = 512
GATE_DIM = 2 * D_MODEL
ROPE_DIM = A_Q_DIM + A_KV_DIM
NORM_DIM = 2 * B_DIM
PLAIN_DIM = B_DIM + A_KV_DIM


def _in_gate_kernel(x_ref, g_ref, w_ref, o_ref, h_ref):
    h_ref[...] = _rmsnorm(x_ref[...], g_ref[...]).astype(BF16)
    lhs = h_ref[...]
    for k in range(GATE_DIM // SUB):
        cols = slice(k * SUB, (k + 1) * SUB)
        acc = jnp.dot(lhs, w_ref[:, cols], preferred_element_type=F32)
        o_ref[:, cols] = _sigmoid(acc).astype(BF16)


def _in_gate(x2, norm_g, w_gate, casts):
    n = x2.shape[0]
    steps = n // GATE_TM
    jobs = [_CastJob(src, steps, ranges) for src, ranges in casts]
    row_block = lambda width: pl.BlockSpec((GATE_TM, width), lambda i: (i, 0))
    return pl.pallas_call(
        _with_casts(_in_gate_kernel, jobs, n_in=3, n_out=2),
        out_shape=[jax.ShapeDtypeStruct((n, GATE_DIM), BF16),
                   jax.ShapeDtypeStruct((n, D_MODEL), BF16)]
                  + [s for j in jobs for s in j.out_shapes()],
        grid=(steps,),
        in_specs=[row_block(D_MODEL), _resident((1, D_MODEL)), _resident((D_MODEL, GATE_DIM))]
                 + [j.in_spec() for j in jobs],
        out_specs=[row_block(GATE_DIM), row_block(D_MODEL)]
                  + [s for j in jobs for s in j.out_specs()],
        compiler_params=_params("arbitrary"),
        name="in_gate",
    )(x2, norm_g, w_gate, *[j.src for j in jobs])


def _store_heads(o_ref, k, acc, fn):
    for t in range(HEADS_PER_SUB):
        col = k * SUB + t * HEAD_DIM
        o_ref[:, col:col + HEAD_DIM] = fn(acc[:, t * HEAD_DIM:(t + 1) * HEAD_DIM]).astype(BF16)


def _pair_columns(head_a, head_b):
    half = HEAD_DIM // 2
    a, b = head_a * HEAD_DIM, head_b * HEAD_DIM
    return [(a, a + half), (b, b + half), (a + half, a + HEAD_DIM), (b + half, b + HEAD_DIM)]


ROPE_PAIRS = [(p, p + A_GROUP) for p in range(A_GROUP)] + [(A_Q_HEADS, A_Q_HEADS + 1)]
assert A_KV_HEADS == 2 and len(ROPE_PAIRS) * SUB == ROPE_DIM
ROPE_COLUMNS = [piece for pair in ROPE_PAIRS for piece in _pair_columns(*pair)]
HALF_LANES = HEAD_DIM // 2


def _in_rope_kernel(h_ref, w_ref, cos_ref, sin_ref, gq_ref, gk_ref, o_ref):
    lhs = h_ref[...]
    first_head = lax.broadcasted_iota(jnp.int32, (IN_TM, HEAD_DIM), 1) < HALF_LANES
    for k in range(ROPE_DIM // SUB):
        acc = jnp.dot(lhs, w_ref[:, k * SUB:(k + 1) * SUB], preferred_element_type=F32)
        gain_ref = gq_ref if k * SUB < A_Q_DIM else gk_ref
        lo, hi = acc[:, :HEAD_DIM], acc[:, HEAD_DIM:]
        sq = lo * lo + hi * hi
        ss_a = jnp.sum(jnp.where(first_head, sq, 0.0), axis=-1, keepdims=True)
        ss_b = jnp.sum(jnp.where(first_head, 0.0, sq), axis=-1, keepdims=True)
        inv = jnp.where(first_head, lax.rsqrt(ss_a * (1.0 / HEAD_DIM) + EPS),
                        lax.rsqrt(ss_b * (1.0 / HEAD_DIM) + EPS))
        lo = lo * inv * gain_ref[0:1, :]
        hi = hi * inv * gain_ref[1:2, :]
        cos, sin = cos_ref[...], sin_ref[...]
        o_ref[:, k * SUB:k * SUB + HEAD_DIM] = (lo * cos - hi * sin).astype(BF16)
        o_ref[:, k * SUB + HEAD_DIM:(k + 1) * SUB] = (hi * cos + lo * sin).astype(BF16)


def _in_norm_kernel(h_ref, w_ref, gq_ref, gk_ref, o_ref):
    lhs = h_ref[...]
    for k in range(NORM_DIM // SUB):
        acc = jnp.dot(lhs, w_ref[:, k * SUB:(k + 1) * SUB], preferred_element_type=F32)
        gain_ref = gq_ref if k * SUB < B_DIM else gk_ref
        _store_heads(o_ref, k, acc, lambda a: _rmsnorm(a, gain_ref[...]))


def _in_plain_kernel(h_ref, wvb_ref, wva_ref, o_ref):
    lhs = h_ref[...]
    for k in range(PLAIN_DIM // SUB):
        cols = slice(k * SUB, (k + 1) * SUB)
        w = wvb_ref[:, cols] if k * SUB < B_DIM else wva_ref[:, k * SUB - B_DIM:(k + 1) * SUB - B_DIM]
        o_ref[:, cols] = jnp.dot(lhs, w, preferred_element_type=F32).astype(BF16)


def _in_heads_kernel(h_ref, w_rope_ref, w_norm_ref, w_vb_ref, w_va_ref, cos_ref, sin_ref,
                     gqa_ref, gka_ref, gqb_ref, gkb_ref, rope_ref, norm_ref, plain_ref):
    _in_rope_kernel(h_ref, w_rope_ref, cos_ref, sin_ref, gqa_ref, gka_ref, rope_ref)
    _in_norm_kernel(h_ref, w_norm_ref, gqb_ref, gkb_ref, norm_ref)
    _in_plain_kernel(h_ref, w_vb_ref, w_va_ref, plain_ref)


def _in_heads(h, weights, cos, sin, gains, casts):
    n = h.shape[0]
    steps = n // IN_TM
    seq_tiles = SEQ // IN_TM
    jobs = [_CastJob(src, steps) for src in casts]
    row_block = lambda width: pl.BlockSpec((IN_TM, width), lambda i: (i, 0))
    pos_spec = pl.BlockSpec((IN_TM, HEAD_DIM), lambda i: (i % seq_tiles, 0))
    widths = (ROPE_DIM, NORM_DIM, PLAIN_DIM)
    return pl.pallas_call(
        _with_casts(_in_heads_kernel, jobs, n_in=11, n_out=3),
        out_shape=[jax.ShapeDtypeStruct((n, width), BF16) for width in widths]
                  + [s for j in jobs for s in j.out_shapes()],
        grid=(steps,),
        in_specs=([row_block(D_MODEL)] + [_resident(w.shape) for w in weights]
                  + [pos_spec, pos_spec] + [_resident(g.shape) for g in gains]
                  + [j.in_spec() for j in jobs]),
        out_specs=[row_block(width) for width in widths] + [s for j in jobs for s in j.out_specs()],
        compiler_params=_params("arbitrary"),
        name="in_heads",
    )(h, *weights, cos, sin, *gains, *[j.src for j in jobs])


A_TQ = 1024
A_BLK = WINDOW
A_NB = A_TQ // A_BLK
A_CHUNKS = SEQ // A_TQ


def _attn_a_kernel(sink_ref, q_ref, kp_ref, km_ref, kn_ref, vp_ref, vm_ref, vn_ref, o_ref):
    c = pl.program_id(1)
    k_all = jnp.concatenate([kp_ref[...], km_ref[...], kn_ref[...]], axis=0)
    v_all = jnp.concatenate([vp_ref[...], vm_ref[...], vn_ref[...]], axis=0)
    qq = lax.broadcasted_iota(jnp.int32, (A_BLK, 3 * A_BLK), 0)
    kk = lax.broadcasted_iota(jnp.int32, (A_BLK, 3 * A_BLK), 1)
    d = kk - qq
    band = (d >= 0) & (d <= 2 * WINDOW)
    ones = jnp.ones((3 * A_BLK, HEAD_DIM), BF16)
    lane = lax.broadcasted_iota(jnp.int32, (A_BLK, SUB), 1) % HEAD_DIM
    own_lanes = (lane < HALF_LANES, lane >= HALF_LANES)
    for n in range(A_NB):
        valid = band
        if n == 0:
            valid = valid & ((kk >= A_BLK) | (c > 0))
        if n == A_NB - 1:
            valid = valid & ((kk < 2 * A_BLK) | (c < A_CHUNKS - 1))
        k = k_all[n * A_BLK:(n + 3) * A_BLK, :]
        q_pairs = [q_ref[n * A_BLK:(n + 1) * A_BLK, p * SUB:(p + 1) * SUB] for p in range(A_GROUP)]
        qs = jnp.concatenate([jnp.where(own_lanes[h], q, jnp.zeros_like(q))
                              for h in range(A_KV_HEADS) for q in q_pairs], axis=0)
        s_all = lax.dot_general(qs, k, (((1,), (1,)), ((), ())),
                                preferred_element_type=F32)
        for h in range(A_KV_HEADS):
            v = v_all[n * A_BLK:(n + 3) * A_BLK, h * HEAD_DIM:(h + 1) * HEAD_DIM]
            heads = [h * A_GROUP + g for g in range(A_GROUP)]
            s = s_all[h * A_GROUP * A_BLK:(h + 1) * A_GROUP * A_BLK]
            ps, sink_terms = [], []
            for g, t in enumerate(heads):
                sg = jnp.where(valid, s[g * A_BLK:(g + 1) * A_BLK], NEG)
                m = jnp.maximum(jnp.max(sg, axis=-1, keepdims=True), sink_ref[t])
                ps.append(jnp.exp2(sg - m).astype(BF16))
                sink_terms.append(jnp.exp2(sink_ref[t] - m))
            o = jnp.dot(jnp.concatenate(ps, axis=0), jnp.concatenate([v, ones], axis=1),
                        preferred_element_type=F32)
            for g, t in enumerate(heads):
                og = o[g * A_BLK:(g + 1) * A_BLK]
                og = og[:, :HEAD_DIM] / (og[:, HEAD_DIM:] + sink_terms[g])
                o_ref[n * A_BLK:(n + 1) * A_BLK, t * HEAD_DIM:(t + 1) * HEAD_DIM] = og.astype(BF16)


def _attn_a(qk, vals, sink, batch, casts):
    n = qk.shape[0]
    blk_per_seq = SEQ // A_BLK
    k_col = A_Q_DIM // A_KV_DIM
    v_col = B_DIM // A_KV_DIM
    jobs = [_CastJob(src, batch * A_CHUNKS, step_of=lambda b, c, *_: b * A_CHUNKS + c)
            for src in casts]

    def prev_map(col):
        return lambda b, c, *_: (b * blk_per_seq + jnp.maximum(c * A_NB - 1, 0), col)

    def main_map(col):
        return lambda b, c, *_: (b * A_CHUNKS + c, col)

    def next_map(col):
        return lambda b, c, *_: (b * blk_per_seq + jnp.minimum(c * A_NB + A_NB, blk_per_seq - 1), col)

    halo = lambda m: pl.BlockSpec((A_BLK, A_KV_DIM), m)
    main = lambda m: pl.BlockSpec((A_TQ, A_KV_DIM), m)
    return pl.pallas_call(
        _with_casts(_attn_a_kernel, jobs, n_in=8, n_out=1),
        out_shape=[jax.ShapeDtypeStruct((n, A_Q_DIM), BF16)] + [s for j in jobs for s in j.out_shapes()],
        grid_spec=pltpu.PrefetchScalarGridSpec(
            num_scalar_prefetch=1,
            grid=(batch, A_CHUNKS),
            in_specs=[
                pl.BlockSpec((A_TQ, A_Q_DIM), main_map(0)),
                halo(prev_map(k_col)), main(main_map(k_col)), halo(next_map(k_col)),
                halo(prev_map(v_col)), main(main_map(v_col)), halo(next_map(v_col)),
            ] + [j.in_spec() for j in jobs],
            out_specs=[pl.BlockSpec((A_TQ, A_Q_DIM), main_map(0))]
                      + [s for j in jobs for s in j.out_specs()],
        ),
        compiler_params=_params("arbitrary", "arbitrary"),
        name="attn_a",
    )(sink, qk, qk, qk, qk, vals, vals, vals, *[j.src for j in jobs])


GRID_ROWS = SEQ // GRID_W
B_QR = 4
B_KR = B_QR + WIN_H
B_NRB = GRID_ROWS // B_QR
B_NQ = B_QR * GRID_W
B_NK = B_KR * GRID_W
B_KBLOCKS = B_KR // B_QR
B_KS_MAX = B_NRB - B_KBLOCKS
B_STEP_BLOCKS = 4
B_STEP_KBLOCKS = B_KBLOCKS + B_STEP_BLOCKS - 1
B_PATTERN_ROWBLOCKS = (0, 1, B_NRB - 1)


def _attn_b_row_windows():
    starts, row_ok = [], []
    for rb in B_PATTERN_ROWBLOCKS:
        ks = int(np.clip(rb - 1, 0, B_KS_MAX))
        qr = rb * B_QR + np.arange(B_QR)
        kr = ks * B_QR + np.arange(B_KR)
        rs = np.clip(qr - WIN_H // 2, 0, GRID_ROWS - WIN_H)
        row_ok.append((kr[None, :] >= rs[:, None]) & (kr[None, :] < rs[:, None] + WIN_H))
        starts.append(kr[0] - qr + WIN_H - 1)
    return np.stack(starts), np.stack(row_ok)


B_ROW_START, B_ROW_OK = _attn_b_row_windows()
B_STRIP_LO = max(0, -int(B_ROW_START.min()))
B_NSTRIP = int(B_ROW_START.max()) + B_KR - 1 + B_STRIP_LO
B_LANES = 2 * GRID_W


def _attn_b_build_bias(strip_ref, bias_ref):
    qc = lax.broadcasted_iota(jnp.int32, (GRID_W, B_LANES), 0)
    lane = lax.broadcasted_iota(jnp.int32, (GRID_W, B_LANES), 1)
    kc = lane % GRID_W
    cs = jnp.clip(qc - WIN_W // 2, 0, GRID_W - WIN_W)
    col_ok = (kc >= cs) & (kc < cs + WIN_W)
    first_row = lane < GRID_W
    for h in range(B_HEADS):
        tiles = {}

        def tile(i):
            if i not in tiles:
                rows = jnp.broadcast_to(strip_ref[h, i:i + 1, :], (GRID_W, B_LANES))
                toeplitz = pltpu.roll(rows, 0, axis=1, stride=1, stride_axis=0)
                tiles[i] = jnp.where(col_ok, toeplitz, NEG)
            return tiles[i]

        for p in range(len(B_PATTERN_ROWBLOCKS)):
            for qr in range(B_QR):
                for c in range(B_KR // 2):
                    ok0, ok1 = B_ROW_OK[p, qr, 2 * c], B_ROW_OK[p, qr, 2 * c + 1]
                    if ok0 or ok1:
                        t = tile(int(B_ROW_START[p, qr]) + 2 * c + B_STRIP_LO)
                        if not ok1:
                            t = jnp.where(first_row, t, NEG)
                        if not ok0:
                            t = jnp.where(first_row, NEG, t)
                    else:
                        t = jnp.full((GRID_W, B_LANES), NEG, F32)
                    bias_ref[p, h, qr * GRID_W:(qr + 1) * GRID_W, c * B_LANES:(c + 1) * B_LANES] = t


def _attn_b_key_base(step):
    return jnp.clip(step * B_STEP_BLOCKS - 1, 0, B_NRB - B_STEP_KBLOCKS)


def _attn_b_step_plans():
    plans = {}
    for step in range(B_NRB // B_STEP_BLOCKS):
        base = int(np.clip(step * B_STEP_BLOCKS - 1, 0, B_NRB - B_STEP_KBLOCKS))
        blocks = []
        for sb in range(B_STEP_BLOCKS):
            r = step * B_STEP_BLOCKS + sb
            offset = int(np.clip(r - 1, 0, B_KS_MAX)) - base
            pattern = 0 if r == 0 else 2 if r == B_NRB - 1 else 1
            blocks.append((offset, pattern))
        plans.setdefault(tuple(blocks), []).append(step)
    return plans


def _attn_b_kernel(q_ref, *refs):
    k_refs = refs[:B_STEP_KBLOCKS]
    v_refs = refs[B_STEP_KBLOCKS:2 * B_STEP_KBLOCKS]
    strip_ref, o_ref, bias_ref = refs[2 * B_STEP_KBLOCKS:]
    step = pl.program_id(1)

    @pl.when((pl.program_id(0) == 0) & (step == 0))
    def _():
        _attn_b_build_bias(strip_ref, bias_ref)

    def body(blocks):
        ones = jnp.ones((B_NK, HEAD_DIM), BF16)
        for sb, (offset, pattern) in enumerate(blocks):
            rows = slice(sb * B_QR, (sb + 1) * B_QR)
            window = range(offset, offset + B_KBLOCKS)
            for h in range(B_HEADS):
                hd = slice(h * HEAD_DIM, (h + 1) * HEAD_DIM)
                q = q_ref[rows, :, hd].reshape(B_NQ, HEAD_DIM)
                k = jnp.concatenate([k_refs[t][:, :, hd] for t in window], axis=0)
                v = jnp.concatenate([v_refs[t][:, :, hd] for t in window], axis=0)
                s = lax.dot_general(q, k.reshape(B_NK, HEAD_DIM), (((1,), (1,)), ((), ())),
                                    preferred_element_type=F32) + bias_ref[pattern, h]
                m = jnp.max(s, axis=-1, keepdims=True)
                p = jnp.exp2(s - m).astype(BF16)
                o = jnp.dot(p, jnp.concatenate([v.reshape(B_NK, HEAD_DIM), ones], axis=1),
                            preferred_element_type=F32)
                o = o[:, :HEAD_DIM] / o[:, HEAD_DIM:]
                o_ref[rows, :, hd] = o.reshape(B_QR, GRID_W, HEAD_DIM).astype(BF16)

    for blocks, steps in _attn_b_step_plans().items():
        is_kind = step == steps[0]
        for other in steps[1:]:
            is_kind = is_kind | (step == other)
        pl.when(is_kind)(lambda blocks=blocks: body(blocks))


def _attn_b_strips(rpb):
    rows = B_NSTRIP + 1
    table = jnp.pad(rpb.astype(F32) * LOG2E,
                    ((0, 0), (B_STRIP_LO, rows - B_STRIP_LO - rpb.shape[1]), (0, 0)))
    first, second = table[:, :-1], table[:, 1:]
    gap = jnp.zeros(first.shape[:2] + (GRID_W - 2 * WIN_W + 1,), F32)
    strips = jnp.concatenate(
        [first[..., WIN_W - 1:], gap, second, gap, first[..., :WIN_W - 1]], axis=-1)
    assert strips.shape == (B_HEADS, B_NSTRIP, B_LANES)
    return strips


def _attn_b(qk3, vals3, strips, batch):
    rows = qk3.shape[0]

    steps = B_NRB // B_STEP_BLOCKS

    def kv_spec(col, t):
        return pl.BlockSpec((B_QR, GRID_W, B_DIM),
                            lambda b, s: (b * B_NRB + _attn_b_key_base(s) + t, 0, col))

    q_spec = pl.BlockSpec((B_STEP_BLOCKS * B_QR, GRID_W, B_DIM), lambda b, s: (b * steps + s, 0, 0))
    return pl.pallas_call(
        _attn_b_kernel,
        out_shape=jax.ShapeDtypeStruct((rows, GRID_W, B_DIM), BF16),
        grid=(batch, steps),
        in_specs=([q_spec] + [kv_spec(1, t) for t in range(B_STEP_KBLOCKS)]
                  + [kv_spec(0, t) for t in range(B_STEP_KBLOCKS)]
                  + [_resident((B_HEADS, B_NSTRIP, B_LANES))]),
        out_specs=q_spec,
        scratch_shapes=[pltpu.VMEM((len(B_PATTERN_ROWBLOCKS), B_HEADS, B_NQ, B_NK), F32)],
        compiler_params=_params("arbitrary", "arbitrary"),
        name="attn_b",
    )(qk3, *[qk3] * B_STEP_KBLOCKS, *[vals3] * B_STEP_KBLOCKS, strips)


MG_TM = 512


def _merge_kernel(x_ref, oa_ref, ob_ref, ga_ref, gb_ref, wa_ref, wb_ref, wo_ref, g_ref,
                  x1_ref, h2_ref):
    a = jnp.dot(oa_ref[...], wa_ref[...], preferred_element_type=F32)
    b = jnp.dot(ob_ref[...], wb_ref[...], preferred_element_type=F32)
    merged = ga_ref[...].astype(F32) * a + gb_ref[...].astype(F32) * b
    y = jnp.dot(merged.astype(BF16), wo_ref[...], preferred_element_type=F32)
    x1 = x_ref[...] + y
    x1_ref[...] = x1
    h2_ref[...] = _rmsnorm(x1, g_ref[...]).astype(BF16)


def _merge(x2, out_a, out_b, gates, wa, wb, wo, norm_g):
    n = x2.shape[0]
    row_block = lambda width, col=0: pl.BlockSpec((MG_TM, width), lambda i: (i, col))
    return pl.pallas_call(
        _merge_kernel,
        out_shape=(jax.ShapeDtypeStruct((n, D_MODEL), F32),
                   jax.ShapeDtypeStruct((n, D_MODEL), BF16)),
        grid=(n // MG_TM,),
        in_specs=[
            row_block(D_MODEL), row_block(A_Q_DIM), row_block(B_DIM),
            row_block(D_MODEL, 0), row_block(D_MODEL, 1),
            _resident((A_Q_DIM, D_MODEL)), _resident((B_DIM, D_MODEL)),
            _resident((D_MODEL, D_MODEL)), _resident((1, D_MODEL)),
        ],
        out_specs=(row_block(D_MODEL), row_block(D_MODEL)),
        compiler_params=_params("arbitrary"),
        name="merge",
    )(x2, out_a, out_b, gates, gates, wa, wb, wo, norm_g)


UP_TM = 1024
UP_NJ = 4
UP_TN = D_FF // UP_NJ
UP_LANES = SUB // 2
UP_NSUB = UP_TN // UP_LANES
HALO_FETCH = 16
HALO = 8
UP_ROWS = UP_TM + 2 * HALO
UP_SPLIT_TAIL = 2


def _ffn_up_kernel(hp_ref, hm_ref, hn_ref, wg_ref, wv_ref, cwg_ref, cwv_ref, cbg_ref, cbv_ref,
                   o_ref, lhs_ref, u_ref):
    i = pl.program_id(0)
    tiles_per_seq = SEQ // UP_TM

    @pl.when(pl.program_id(1) == 0)
    def _():
        first = (i % tiles_per_seq) == 0
        last = (i % tiles_per_seq) == tiles_per_seq - 1
        above_rows, below_rows = hp_ref[HALO_FETCH - HALO:, :], hn_ref[:HALO, :]
        lhs_ref[0:HALO, :] = jnp.where(first, jnp.zeros_like(above_rows), above_rows)
        lhs_ref[HALO:HALO + UP_TM, :] = hm_ref[...]
        lhs_ref[HALO + UP_TM:, :] = jnp.where(last, jnp.zeros_like(below_rows), below_rows)

    def conv(slot, rows, lanes, cw_ref, cb_ref, cols, scale):
        below = u_ref[slot, HALO - 1:HALO - 1 + rows, lanes]
        mid = u_ref[slot, HALO:HALO + rows, lanes]
        above = u_ref[slot, HALO + 1:HALO + 1 + rows, lanes]
        c0, c1, c2 = (scale * cw_ref[t:t + 1, cols] for t in range(3))
        y = mid * c1 + scale * cb_ref[:, cols]
        y = y + below * c0
        return y + above * c2

    def unit(k, slot, row0, rows):
        cols = slice(k * UP_LANES, (k + 1) * UP_LANES)
        w = jnp.concatenate([wg_ref[:, cols], wv_ref[:, cols]], axis=1)
        u_ref[slot, 0:rows + 2 * HALO, :] = jnp.dot(
            lhs_ref[row0:row0 + rows + 2 * HALO, :], w, preferred_element_type=F32)
        half_gate = conv(slot, rows, slice(0, UP_LANES), cwg_ref, cbg_ref, cols, 0.5)
        val = conv(slot, rows, slice(UP_LANES, SUB), cwv_ref, cbv_ref, cols, 1.0)
        silu = half_gate * jnp.tanh(half_gate) + half_gate
        o_ref[row0:row0 + rows, cols] = (silu * val).astype(BF16)

    units = []
    for k in range(UP_NSUB):
        if k < UP_NSUB - UP_SPLIT_TAIL:
            units.append((k, 0, UP_TM))
        else:
            units += [(k, 0, UP_TM // 2), (k, UP_TM // 2, UP_TM // 2)]
    for slot, (k, row0, rows) in enumerate(units):
        unit(k, slot % 2, row0, rows)


def _ffn_up(h2, w_up, conv_w, conv_b):
    n = h2.shape[0]
    halo_blocks = UP_TM // HALO_FETCH
    last_halo = n // HALO_FETCH - 1
    return pl.pallas_call(
        _ffn_up_kernel,
        out_shape=jax.ShapeDtypeStruct((n, D_FF), BF16),
        grid=(n // UP_TM, UP_NJ),
        in_specs=[
            pl.BlockSpec((HALO_FETCH, D_MODEL), lambda i, j: (jnp.maximum(i * halo_blocks - 1, 0), 0)),
            pl.BlockSpec((UP_TM, D_MODEL), lambda i, j: (i, 0)),
            pl.BlockSpec((HALO_FETCH, D_MODEL),
                         lambda i, j: (jnp.minimum((i + 1) * halo_blocks, last_halo), 0)),
            pl.BlockSpec((D_MODEL, UP_TN), lambda i, j: (0, j)),
            pl.BlockSpec((D_MODEL, UP_TN), lambda i, j: (0, UP_NJ + j)),
            pl.BlockSpec((3, UP_TN), lambda i, j: (0, j)),
            pl.BlockSpec((3, UP_TN), lambda i, j: (0, UP_NJ + j)),
            pl.BlockSpec((1, UP_TN), lambda i, j: (0, j)),
            pl.BlockSpec((1, UP_TN), lambda i, j: (0, UP_NJ + j)),
        ],
        out_specs=pl.BlockSpec((UP_TM, UP_TN), lambda i, j: (i, j)),
        scratch_shapes=[pltpu.VMEM((UP_ROWS, D_MODEL), BF16),
                        pltpu.VMEM((2, UP_ROWS, SUB), F32)],
        compiler_params=_params("arbitrary", "arbitrary"),
        name="ffn_up",
    )(h2, h2, h2, w_up, w_up, conv_w, conv_w, conv_b, conv_b)


DN_TM = 512


def _ffn_down_kernel(a_ref, w_ref, x_ref, o_ref):
    lhs = a_ref[...]
    for k in range(D_MODEL // SUB):
        cols = slice(k * SUB, (k + 1) * SUB)
        o_ref[:, cols] = x_ref[:, cols] + jnp.dot(lhs, w_ref[:, cols], preferred_element_type=F32)


def _ffn_down(act, w_down, x1):
    n = act.shape[0]
    row_block = lambda width: pl.BlockSpec((DN_TM, width), lambda i: (i, 0))
    return pl.pallas_call(
        _ffn_down_kernel,
        out_shape=jax.ShapeDtypeStruct((n, D_MODEL), F32),
        grid=(n // DN_TM,),
        in_specs=[row_block(D_FF), _resident((D_FF, D_MODEL)), row_block(D_MODEL)],
        out_specs=row_block(D_MODEL),
        compiler_params=_params("arbitrary"),
        name="ffn_down",
    )(act, w_down, x1)


def _rope_tables():
    half = HEAD_DIM // 2
    pos = jnp.arange(SEQ, dtype=F32)
    inv_freq = ROPE_THETA ** (-jnp.arange(half, dtype=F32) * (2.0 / HEAD_DIM))
    ang = pos[:, None] * inv_freq[None, :]
    cos, sin = jnp.cos(ang), jnp.sin(ang)
    return jnp.concatenate([cos, cos], axis=-1), jnp.concatenate([sin, sin], axis=-1)


def _paired_gain(gain):
    lo, hi = gain[:HALF_LANES], gain[HALF_LANES:]
    return jnp.stack([jnp.concatenate([lo, lo]), jnp.concatenate([hi, hi])])


IN_COL_VA = ROPE_DIM
IN_COL_QB = IN_COL_VA + A_KV_DIM
IN_COL_VB = IN_COL_QB + NORM_DIM
IN_COL_GATE = IN_COL_VB + B_DIM
IN_HEAD_OUTPUTS = [ROPE_COLUMNS, [(IN_COL_QB, IN_COL_VB)], [(IN_COL_VB, IN_COL_GATE)],
                   [(IN_COL_VA, IN_COL_QB)]]


def kernel(x, norm_mix, w_in, a_q_norm, a_k_norm, a_sink, b_q_norm, b_k_norm, b_rpb,
           w_branch_a, w_branch_b, w_out, norm_ffn, w_up, conv_w, conv_b, w_down):
    batch, seq, d_model = x.shape
    assert (seq, d_model) == (SEQ, D_MODEL)
    n = batch * seq
    cos, sin = _rope_tables()
    scale = LOG2E / math.sqrt(HEAD_DIM)
    x2 = x.reshape(n, d_model)
    for l in range(norm_mix.shape[0]):
        w_gate = w_in[l][:, IN_COL_GATE:].astype(BF16)
        gates, h, w_rope, w_norm, w_vb, w_va, w_up_bf = _in_gate(
            x2, norm_mix[l][None], w_gate, [(w_in[l], IN_HEAD_OUTPUTS), (w_up[l], None)])
        gains = (_paired_gain(a_q_norm[l] * scale), _paired_gain(a_k_norm[l]),
                 b_q_norm[l][None] * scale, b_k_norm[l][None])
        qk_a, qk_b, vals = _in_heads(h, (w_rope, w_norm, w_vb, w_va), cos, sin, gains, [])
        out_a, wa_bf, wb_bf, w_out_bf, w_down_bf = _attn_a(
            qk_a, vals, a_sink[l] * LOG2E, batch,
            [w_branch_a[l], w_branch_b[l], w_out[l], w_down[l]])
        out_b = _attn_b(qk_b.reshape(n // GRID_W, GRID_W, NORM_DIM),
                        vals.reshape(n // GRID_W, GRID_W, PLAIN_DIM), _attn_b_strips(b_rpb[l]), batch)
        x1, h2 = _merge(x2, out_a, out_b.reshape(n, B_DIM), gates, wa_bf, wb_bf, w_out_bf,
                        norm_ffn[l][None])
        act = _ffn_up(h2, w_up_bf, conv_w[l], conv_b[l][None])
        x2 = _ffn_down(act, w_down_bf, x1)
    return x2.reshape(batch, seq, d_model)
```

```python
import math

import numpy as np
import jax
import jax.numpy as jnp
from jax import lax
from jax.experimental import pallas as pl
from jax.experimental.pallas import tpu as pltpu

D_MODEL = 2048
SEQ = 4096
HEAD_DIM = 128
A_Q_HEADS = 8
A_KV_HEADS = 2
A_GROUP = A_Q_HEADS // A_KV_HEADS
WINDOW = 128
B_HEADS = 8
GRID_W = 64
WIN_H = 8
WIN_W = 16
D_FF = 5632
ROPE_THETA = 10000.0
EPS = 1e-6
NEG = -1e30
LOG2E = math.log2(math.e)

A_Q_DIM = A_Q_HEADS * HEAD_DIM
A_KV_DIM = A_KV_HEADS * HEAD_DIM
B_DIM = B_HEADS * HEAD_DIM

VMEM_LIMIT_BYTES = 56 * 1024 * 1024
MXU_COLS = 256
SUB = MXU_COLS
HEADS_PER_SUB = SUB // HEAD_DIM

BF16 = jnp.bfloat16
F32 = jnp.float32


def _params(*semantics):
    return pltpu.CompilerParams(dimension_semantics=semantics, vmem_limit_bytes=VMEM_LIMIT_BYTES)


def _resident(shape):
    return pl.BlockSpec(shape, lambda *_: (0,) * len(shape), pipeline_mode=pl.Buffered(1))


def _rmsnorm(x, gain):
    ms = jnp.mean(x * x, axis=-1, keepdims=True)
    return x * lax.rsqrt(ms + EPS) * gain


class _CastJob:
    def __init__(self, src, steps, outputs=None, step_of=lambda i: i):
        rows, cols = src.shape
        self.src = src
        self.outputs = outputs or [[(0, cols)]]
        self.slab = rows // steps
        assert self.slab * steps == rows
        self.span = max(hi for pieces in self.outputs for _, hi in pieces)
        self.widths = [sum(hi - lo for lo, hi in pieces) for pieces in self.outputs]
        self.index_map = lambda *idx: (step_of(*idx), 0)

    def in_spec(self):
        return pl.BlockSpec((self.slab, self.span), self.index_map)

    def out_specs(self):
        return [pl.BlockSpec((self.slab, width), self.index_map) for width in self.widths]

    def out_shapes(self):
        return [jax.ShapeDtypeStruct((self.src.shape[0], width), BF16) for width in self.widths]

    def run(self, src_ref, out_refs):
        for pieces, out_ref in zip(self.outputs, out_refs):
            parts = [src_ref[:, lo:hi] for lo, hi in pieces]
            value = parts[0] if len(parts) == 1 else jnp.concatenate(parts, axis=1)
            out_ref[...] = value.astype(BF16)


def _with_casts(body, jobs, n_in, n_out):
    n_cast_out = sum(len(j.outputs) for j in jobs)

    def wrapped(*refs):
        ins = refs[:n_in]
        srcs = refs[n_in:n_in + len(jobs)]
        outs = refs[n_in + len(jobs):n_in + len(jobs) + n_out]
        cast_outs = list(refs[n_in + len(jobs) + n_out:n_in + len(jobs) + n_out + n_cast_out])
        scratch = refs[n_in + len(jobs) + n_out + n_cast_out:]
        for job, src_ref in zip(jobs, srcs):
            job.run(src_ref, [cast_outs.pop(0) for _ in job.outputs])
        body(*ins, *outs, *scratch)

    return wrapped


def _sigmoid(x):
    return 0.5 * jnp.tanh(0.5 * x) + 0.5


IN_TM = 1024
GATE_TM = 512
GATE_DIM = 2 * D_MODEL
ROPE_DIM = A_Q_DIM + A_KV_DIM
NORM_DIM = 2 * B_DIM
PLAIN_DIM = B_DIM + A_KV_DIM


def _in_gate_kernel(x_ref, g_ref, w_ref, o_ref, h_ref):
    h_ref[...] = _rmsnorm(x_ref[...], g_ref[...]).astype(BF16)
    lhs = h_ref[...]
    for k in range(GATE_DIM // SUB):
        cols = slice(k * SUB, (k + 1) * SUB)
        acc = jnp.dot(lhs, w_ref[:, cols], preferred_element_type=F32)
        o_ref[:, cols] = _sigmoid(acc).astype(BF16)


def _in_gate(x2, norm_g, w_gate, casts):
    n = x2.shape[0]
    steps = n // GATE_TM
    jobs = [_CastJob(src, steps, ranges) for src, ranges in casts]
    row_block = lambda width: pl.BlockSpec((GATE_TM, width), lambda i: (i, 0))
    return pl.pallas_call(
        _with_casts(_in_gate_kernel, jobs, n_in=3, n_out=2),
        out_shape=[jax.ShapeDtypeStruct((n, GATE_DIM), BF16),
                   jax.ShapeDtypeStruct((n, D_MODEL), BF16)]
                  + [s for j in jobs for s in j.out_shapes()],
        grid=(steps,),
        in_specs=[row_block(D_MODEL), _resident((1, D_MODEL)), _resident((D_MODEL, GATE_DIM))]
                 + [j.in_spec() for j in jobs],
        out_specs=[row_block(GATE_DIM), row_block(D_MODEL)]
                  + [s for j in jobs for s in j.out_specs()],
        compiler_params=_params("arbitrary"),
        name="in_gate",
    )(x2, norm_g, w_gate, *[j.src for j in jobs])


def _store_heads(o_ref, k, acc, fn):
    for t in range(HEADS_PER_SUB):
        col = k * SUB + t * HEAD_DIM
        o_ref[:, col:col + HEAD_DIM] = fn(acc[:, t * HEAD_DIM:(t + 1) * HEAD_DIM]).astype(BF16)


def _pair_columns(head_a, head_b):
    half = HEAD_DIM // 2
    a, b = head_a * HEAD_DIM, head_b * HEAD_DIM
    return [(a, a + half), (b, b + half), (a + half, a + HEAD_DIM), (b + half, b + HEAD_DIM)]


ROPE_PAIRS = [(p, p + A_GROUP) for p in range(A_GROUP)] + [(A_Q_HEADS, A_Q_HEADS + 1)]
assert A_KV_HEADS == 2 and len(ROPE_PAIRS) * SUB == ROPE_DIM
ROPE_COLUMNS = [piece for pair in ROPE_PAIRS for piece in _pair_columns(*pair)]
HALF_LANES = HEAD_DIM // 2


def _in_rope_kernel(h_ref, w_ref, cos_ref, sin_ref, gq_ref, gk_ref, o_ref):
    lhs = h_ref[...]
    first_head = lax.broadcasted_iota(jnp.int32, (IN_TM, HEAD_DIM), 1) < HALF_LANES
    for k in range(ROPE_DIM // SUB):
        acc = jnp.dot(lhs, w_ref[:, k * SUB:(k + 1) * SUB], preferred_element_type=F32)
        gain_ref = gq_ref if k * SUB < A_Q_DIM else gk_ref
        lo, hi = acc[:, :HEAD_DIM], acc[:, HEAD_DIM:]
        sq = lo * lo + hi * hi
        ss_a = jnp.sum(jnp.where(first_head, sq, 0.0), axis=-1, keepdims=True)
        ss_b = jnp.sum(jnp.where(first_head, 0.0, sq), axis=-1, keepdims=True)
        inv = jnp.where(first_head, lax.rsqrt(ss_a * (1.0 / HEAD_DIM) + EPS),
                        lax.rsqrt(ss_b * (1.0 / HEAD_DIM) + EPS))
        lo = lo * inv * gain_ref[0:1, :]
        hi = hi * inv * gain_ref[1:2, :]
        cos, sin = cos_ref[...], sin_ref[...]
        o_ref[:, k * SUB:k * SUB + HEAD_DIM] = (lo * cos - hi * sin).astype(BF16)
        o_ref[:, k * SUB + HEAD_DIM:(k + 1) * SUB] = (hi * cos + lo * sin).astype(BF16)


def _in_norm_kernel(h_ref, w_ref, gq_ref, gk_ref, o_ref):
    lhs = h_ref[...]
    for k in range(NORM_DIM // SUB):
        acc = jnp.dot(lhs, w_ref[:, k * SUB:(k + 1) * SUB], preferred_element_type=F32)
        gain_ref = gq_ref if k * SUB < B_DIM else gk_ref
        _store_heads(o_ref, k, acc, lambda a: _rmsnorm(a, gain_ref[...]))


def _in_plain_kernel(h_ref, wvb_ref, wva_ref, o_ref):
    lhs = h_ref[...]
    for k in range(PLAIN_DIM // SUB):
        cols = slice(k * SUB, (k + 1) * SUB)
        w = wvb_ref[:, cols] if k * SUB < B_DIM else wva_ref[:, k * SUB - B_DIM:(k + 1) * SUB - B_DIM]
        o_ref[:, cols] = jnp.dot(lhs, w, preferred_element_type=F32).astype(BF16)


def _in_heads_kernel(h_ref, w_rope_ref, w_norm_ref, w_vb_ref, w_va_ref, cos_ref, sin_ref,
                     gqa_ref, gka_ref, gqb_ref, gkb_ref, rope_ref, norm_ref, plain_ref):
    _in_rope_kernel(h_ref, w_rope_ref, cos_ref, sin_ref, gqa_ref, gka_ref, rope_ref)
    _in_norm_kernel(h_ref, w_norm_ref, gqb_ref, gkb_ref, norm_ref)
    _in_plain_kernel(h_ref, w_vb_ref, w_va_ref, plain_ref)


def _in_heads(h, weights, cos, sin, gains, casts):
    n = h.shape[0]
    steps = n // IN_TM
    seq_tiles = SEQ // IN_TM
    jobs = [_CastJob(src, steps) for src in casts]
    row_block = lambda width: pl.BlockSpec((IN_TM, width), lambda i: (i, 0))
    pos_spec = pl.BlockSpec((IN_TM, HEAD_DIM), lambda i: (i % seq_tiles, 0))
    widths = (ROPE_DIM, NORM_DIM, PLAIN_DIM)
    return pl.pallas_call(
        _with_casts(_in_heads_kernel, jobs, n_in=11, n_out=3),
        out_shape=[jax.ShapeDtypeStruct((n, width), BF16) for width in widths]
                  + [s for j in jobs for s in j.out_shapes()],
        grid=(steps,),
        in_specs=([row_block(D_MODEL)] + [_resident(w.shape) for w in weights]
                  + [pos_spec, pos_spec] + [_resident(g.shape) for g in gains]
                  + [j.in_spec() for j in jobs]),
        out_specs=[row_block(width) for width in widths] + [s for j in jobs for s in j.out_specs()],
        compiler_params=_params("arbitrary"),
        name="in_heads",
    )(h, *weights, cos, sin, *gains, *[j.src for j in jobs])


A_TQ = 1024
A_BLK = WINDOW
A_NB = A_TQ // A_BLK
A_CHUNKS = SEQ // A_TQ


def _attn_a_kernel(sink_ref, q_ref, kp_ref, km_ref, kn_ref, vp_ref, vm_ref, vn_ref, o_ref):
    c = pl.program_id(1)
    k_all = jnp.concatenate([kp_ref[...], km_ref[...], kn_ref[...]], axis=0)
    v_all = jnp.concatenate([vp_ref[...], vm_ref[...], vn_ref[...]], axis=0)
    qq = lax.broadcasted_iota(jnp.int32, (A_BLK, 3 * A_BLK), 0)
    kk = lax.broadcasted_iota(jnp.int32, (A_BLK, 3 * A_BLK), 1)
    d = kk - qq
    band = (d >= 0) & (d <= 2 * WINDOW)
    ones = jnp.ones((3 * A_BLK, HEAD_DIM), BF16)
    lane = lax.broadcasted_iota(jnp.int32, (A_BLK, SUB), 1) % HEAD_DIM
    own_lanes = (lane < HALF_LANES, lane >= HALF_LANES)
    for n in range(A_NB):
        valid = band
        if n == 0:
            valid = valid & ((kk >= A_BLK) | (c > 0))
        if n == A_NB - 1:
            valid = valid & ((kk < 2 * A_BLK) | (c < A_CHUNKS - 1))
        k = k_all[n * A_BLK:(n + 3) * A_BLK, :]
        q_pairs = [q_ref[n * A_BLK:(n + 1) * A_BLK, p * SUB:(p + 1) * SUB] for p in range(A_GROUP)]
        qs = jnp.concatenate([jnp.where(own_lanes[h], q, jnp.zeros_like(q))
                              for h in range(A_KV_HEADS) for q in q_pairs], axis=0)
        s_all = lax.dot_general(qs, k, (((1,), (1,)), ((), ())),
                                preferred_element_type=F32)
        for h in range(A_KV_HEADS):
            v = v_all[n * A_BLK:(n + 3) * A_BLK, h * HEAD_DIM:(h + 1) * HEAD_DIM]
            heads = [h * A_GROUP + g for g in range(A_GROUP)]
            s = s_all[h * A_GROUP * A_BLK:(h + 1) * A_GROUP * A_BLK]
            ps, sink_terms = [], []
            for g, t in enumerate(heads):
                sg = jnp.where(valid, s[g * A_BLK:(g + 1) * A_BLK], NEG)
                m = jnp.maximum(jnp.max(sg, axis=-1, keepdims=True), sink_ref[t])
                ps.append(jnp.exp2(sg - m).astype(BF16))
                sink_terms.append(jnp.exp2(sink_ref[t] - m))
            o = jnp.dot(jnp.concatenate(ps, axis=0), jnp.concatenate([v, ones], axis=1),
                        preferred_element_type=F32)
            for g, t in enumerate(heads):
                og = o[g * A_BLK:(g + 1) * A_BLK]
                og = og[:, :HEAD_DIM] / (og[:, HEAD_DIM:] + sink_terms[g])
                o_ref[n * A_BLK:(n + 1) * A_BLK, t * HEAD_DIM:(t + 1) * HEAD_DIM] = og.astype(BF16)


def _attn_a(qk, vals, sink, batch, casts):
    n = qk.shape[0]
    blk_per_seq = SEQ // A_BLK
    k_col = A_Q_DIM // A_KV_DIM
    v_col = B_DIM // A_KV_DIM
    jobs = [_CastJob(src, batch * A_CHUNKS, step_of=lambda b, c, *_: b * A_CHUNKS + c)
            for src in casts]

    def prev_map(col):
        return lambda b, c, *_: (b * blk_per_seq + jnp.maximum(c * A_NB - 1, 0), col)

    def main_map(col):
        return lambda b, c, *_: (b * A_CHUNKS + c, col)

    def next_map(col):
        return lambda b, c, *_: (b * blk_per_seq + jnp.minimum(c * A_NB + A_NB, blk_per_seq - 1), col)

    halo = lambda m: pl.BlockSpec((A_BLK, A_KV_DIM), m)
    main = lambda m: pl.BlockSpec((A_TQ, A_KV_DIM), m)
    return pl.pallas_call(
        _with_casts(_attn_a_kernel, jobs, n_in=8, n_out=1),
        out_shape=[jax.ShapeDtypeStruct((n, A_Q_DIM), BF16)] + [s for j in jobs for s in j.out_shapes()],
        grid_spec=pltpu.PrefetchScalarGridSpec(
            num_scalar_prefetch=1,
            grid=(batch, A_CHUNKS),
            in_specs=[
                pl.BlockSpec((A_TQ, A_Q_DIM), main_map(0)),
                halo(prev_map(k_col)), main(main_map(k_col)), halo(next_map(k_col)),
                halo(prev_map(v_col)), main(main_map(v_col)), halo(next_map(v_col)),
            ] + [j.in_spec() for j in jobs],
            out_specs=[pl.BlockSpec((A_TQ, A_Q_DIM), main_map(0))]
                      + [s for j in jobs for s in j.out_specs()],
        ),
        compiler_params=_params("arbitrary", "arbitrary"),
        name="attn_a",
    )(sink, qk, qk, qk, qk, vals, vals, vals, *[j.src for j in jobs])


GRID_ROWS = SEQ // GRID_W
B_QR = 4
B_KR = B_QR + WIN_H
B_NRB = GRID_ROWS // B_QR
B_NQ = B_QR * GRID_W
B_NK = B_KR * GRID_W
B_KBLOCKS = B_KR // B_QR
B_KS_MAX = B_NRB - B_KBLOCKS
B_STEP_BLOCKS = 4
B_STEP_KBLOCKS = B_KBLOCKS + B_STEP_BLOCKS - 1
B_PATTERN_ROWBLOCKS = (0, 1, B_NRB - 1)


def _attn_b_row_windows():
    starts, row_ok = [], []
    for rb in B_PATTERN_ROWBLOCKS:
        ks = int(np.clip(rb - 1, 0, B_KS_MAX))
        qr = rb * B_QR + np.arange(B_QR)
        kr = ks * B_QR + np.arange(B_KR)
        rs = np.clip(qr - WIN_H // 2, 0, GRID_ROWS - WIN_H)
        row_ok.append((kr[None, :] >= rs[:, None]) & (kr[None, :] < rs[:, None] + WIN_H))
        starts.append(kr[0] - qr + WIN_H - 1)
    return np.stack(starts), np.stack(row_ok)


B_ROW_START, B_ROW_OK = _attn_b_row_windows()
B_STRIP_LO = max(0, -int(B_ROW_START.min()))
B_NSTRIP = int(B_ROW_START.max()) + B_KR - 1 + B_STRIP_LO
B_LANES = 2 * GRID_W


def _attn_b_build_bias(strip_ref, bias_ref):
    qc = lax.broadcasted_iota(jnp.int32, (GRID_W, B_LANES), 0)
    lane = lax.broadcasted_iota(jnp.int32, (GRID_W, B_LANES), 1)
    kc = lane % GRID_W
    cs = jnp.clip(qc - WIN_W // 2, 0, GRID_W - WIN_W)
    col_ok = (kc >= cs) & (kc < cs + WIN_W)
    first_row = lane < GRID_W
    for h in range(B_HEADS):
        tiles = {}

        def tile(i):
            if i not in tiles:
                rows = jnp.broadcast_to(strip_ref[h, i:i + 1, :], (GRID_W, B_LANES))
                toeplitz = pltpu.roll(rows, 0, axis=1, stride=1, stride_axis=0)
                tiles[i] = jnp.where(col_ok, toeplitz, NEG)
            return tiles[i]

        for p in range(len(B_PATTERN_ROWBLOCKS)):
            for qr in range(B_QR):
                for c in range(B_KR // 2):
                    ok0, ok1 = B_ROW_OK[p, qr, 2 * c], B_ROW_OK[p, qr, 2 * c + 1]
                    if ok0 or ok1:
                        t = tile(int(B_ROW_START[p, qr]) + 2 * c + B_STRIP_LO)
                        if not ok1:
                            t = jnp.where(first_row, t, NEG)
                        if not ok0:
                            t = jnp.where(first_row, NEG, t)
                    else:
                        t = jnp.full((GRID_W, B_LANES), NEG, F32)
                    bias_ref[p, h, qr * GRID_W:(qr + 1) * GRID_W, c * B_LANES:(c + 1) * B_LANES] = t


def _attn_b_key_base(step):
    return jnp.clip(step * B_STEP_BLOCKS - 1, 0, B_NRB - B_STEP_KBLOCKS)


def _attn_b_kernel(q_ref, *refs):
    k_refs = refs[:B_STEP_KBLOCKS]
    v_refs = refs[B_STEP_KBLOCKS:2 * B_STEP_KBLOCKS]
    strip_ref, o_ref, bias_ref, k_buf, v_buf = refs[2 * B_STEP_KBLOCKS:]
    step = pl.program_id(1)

    @pl.when((pl.program_id(0) == 0) & (step == 0))
    def _():
        _attn_b_build_bias(strip_ref, bias_ref)

    for t in range(B_STEP_KBLOCKS):
        k_buf[t * B_QR:(t + 1) * B_QR] = k_refs[t][...]
        v_buf[t * B_QR:(t + 1) * B_QR] = v_refs[t][...]

    ones = jnp.ones((B_NK, HEAD_DIM), BF16)
    for sb in range(B_STEP_BLOCKS):
        r = step * B_STEP_BLOCKS + sb
        pattern = jnp.where(r == 0, 0, jnp.where(r == B_NRB - 1, 2, 1))
        first_row = (jnp.clip(r - 1, 0, B_KS_MAX) - _attn_b_key_base(step)) * B_QR
        rows = slice(sb * B_QR, (sb + 1) * B_QR)
        for h in range(B_HEADS):
            hd = slice(h * HEAD_DIM, (h + 1) * HEAD_DIM)
            q = q_ref[rows, :, hd].reshape(B_NQ, HEAD_DIM)
            k = k_buf[pl.ds(first_row, B_KR), :, hd].reshape(B_NK, HEAD_DIM)
            v = v_buf[pl.ds(first_row, B_KR), :, hd].reshape(B_NK, HEAD_DIM)
            s = lax.dot_general(q, k, (((1,), (1,)), ((), ())),
                                preferred_element_type=F32) + bias_ref[pattern, h]
            m = jnp.max(s, axis=-1, keepdims=True)
            p = jnp.exp2(s - m).astype(BF16)
            o = jnp.dot(p, jnp.concatenate([v, ones], axis=1), preferred_element_type=F32)
            o = o[:, :HEAD_DIM] / o[:, HEAD_DIM:]
            o_ref[rows, :, hd] = o.reshape(B_QR, GRID_W, HEAD_DIM).astype(BF16)


def _attn_b_strips(rpb):
    rows = B_NSTRIP + 1
    table = jnp.pad(rpb.astype(F32) * LOG2E,
                    ((0, 0), (B_STRIP_LO, rows - B_STRIP_LO - rpb.shape[1]), (0, 0)))
    first, second = table[:, :-1], table[:, 1:]
    gap = jnp.zeros(first.shape[:2] + (GRID_W - 2 * WIN_W + 1,), F32)
    strips = jnp.concatenate(
        [first[..., WIN_W - 1:], gap, second, gap, first[..., :WIN_W - 1]], axis=-1)
    assert strips.shape == (B_HEADS, B_NSTRIP, B_LANES)
    return strips


def _attn_b(qk3, vals3, strips, batch):
    rows = qk3.shape[0]

    steps = B_NRB // B_STEP_BLOCKS

    def kv_spec(col, t):
        return pl.BlockSpec((B_QR, GRID_W, B_DIM),
                            lambda b, s: (b * B_NRB + _attn_b_key_base(s) + t, 0, col))

    q_spec = pl.BlockSpec((B_STEP_BLOCKS * B_QR, GRID_W, B_DIM), lambda b, s: (b * steps + s, 0, 0))
    window = (B_STEP_KBLOCKS * B_QR, GRID_W, B_DIM)
    return pl.pallas_call(
        _attn_b_kernel,
        out_shape=jax.ShapeDtypeStruct((rows, GRID_W, B_DIM), BF16),
        grid=(batch, steps),
        in_specs=([q_spec] + [kv_spec(1, t) for t in range(B_STEP_KBLOCKS)]
                  + [kv_spec(0, t) for t in range(B_STEP_KBLOCKS)]
                  + [_resident((B_HEADS, B_NSTRIP, B_LANES))]),
        out_specs=q_spec,
        scratch_shapes=[pltpu.VMEM((len(B_PATTERN_ROWBLOCKS), B_HEADS, B_NQ, B_NK), F32),
                        pltpu.VMEM(window, BF16), pltpu.VMEM(window, BF16)],
        compiler_params=_params("arbitrary", "arbitrary"),
        name="attn_b",
    )(qk3, *[qk3] * B_STEP_KBLOCKS, *[vals3] * B_STEP_KBLOCKS, strips)


MG_TM = 512


def _merge_kernel(x_ref, oa_ref, ob_ref, ga_ref, gb_ref, wa_ref, wb_ref, wo_ref, g_ref,
                  x1_ref, h2_ref):
    a = jnp.dot(oa_ref[...], wa_ref[...], preferred_element_type=F32)
    b = jnp.dot(ob_ref[...], wb_ref[...], preferred_element_type=F32)
    merged = ga_ref[...].astype(F32) * a + gb_ref[...].astype(F32) * b
    y = jnp.dot(merged.astype(BF16), wo_ref[...], preferred_element_type=F32)
    x1 = x_ref[...] + y
    x1_ref[...] = x1
    h2_ref[...] = _rmsnorm(x1, g_ref[...]).astype(BF16)


def _merge(x2, out_a, out_b, gates, wa, wb, wo, norm_g):
    n = x2.shape[0]
    row_block = lambda width, col=0: pl.BlockSpec((MG_TM, width), lambda i: (i, col))
    return pl.pallas_call(
        _merge_kernel,
        out_shape=(jax.ShapeDtypeStruct((n, D_MODEL), F32),
                   jax.ShapeDtypeStruct((n, D_MODEL), BF16)),
        grid=(n // MG_TM,),
        in_specs=[
            row_block(D_MODEL), row_block(A_Q_DIM), row_block(B_DIM),
            row_block(D_MODEL, 0), row_block(D_MODEL, 1),
            _resident((A_Q_DIM, D_MODEL)), _resident((B_DIM, D_MODEL)),
            _resident((D_MODEL, D_MODEL)), _resident((1, D_MODEL)),
        ],
        out_specs=(row_block(D_MODEL), row_block(D_MODEL)),
        compiler_params=_params("arbitrary"),
        name="merge",
    )(x2, out_a, out_b, gates, gates, wa, wb, wo, norm_g)


UP_TM = 1024
UP_NJ = 4
UP_TN = D_FF // UP_NJ
UP_LANES = SUB // 2
UP_NSUB = UP_TN // UP_LANES
HALO_FETCH = 16
HALO = 8
UP_ROWS = UP_TM + 2 * HALO
UP_SPLIT_TAIL = 2


def _ffn_up_kernel(hp_ref, hm_ref, hn_ref, wg_ref, wv_ref, cwg_ref, cwv_ref, cbg_ref, cbv_ref,
                   o_ref, lhs_ref, u_ref):
    i = pl.program_id(0)
    tiles_per_seq = SEQ // UP_TM

    @pl.when(pl.program_id(1) == 0)
    def _():
        first = (i % tiles_per_seq) == 0
        last = (i % tiles_per_seq) == tiles_per_seq - 1
        above_rows, below_rows = hp_ref[HALO_FETCH - HALO:, :], hn_ref[:HALO, :]
        lhs_ref[0:HALO, :] = jnp.where(first, jnp.zeros_like(above_rows), above_rows)
        lhs_ref[HALO:HALO + UP_TM, :] = hm_ref[...]
        lhs_ref[HALO + UP_TM:, :] = jnp.where(last, jnp.zeros_like(below_rows), below_rows)

    def conv(slot, rows, lanes, cw_ref, cb_ref, cols, scale):
        below = u_ref[slot, HALO - 1:HALO - 1 + rows, lanes]
        mid = u_ref[slot, HALO:HALO + rows, lanes]
        above = u_ref[slot, HALO + 1:HALO + 1 + rows, lanes]
        c0, c1, c2 = (scale * cw_ref[t:t + 1, cols] for t in range(3))
        y = mid * c1 + scale * cb_ref[:, cols]
        y = y + below * c0
        return y + above * c2

    def unit(k, slot, row0, rows):
        cols = slice(k * UP_LANES, (k + 1) * UP_LANES)
        w = jnp.concatenate([wg_ref[:, cols], wv_ref[:, cols]], axis=1)
        u_ref[slot, 0:rows + 2 * HALO, :] = jnp.dot(
            lhs_ref[row0:row0 + rows + 2 * HALO, :], w, preferred_element_type=F32)
        half_gate = conv(slot, rows, slice(0, UP_LANES), cwg_ref, cbg_ref, cols, 0.5)
        val = conv(slot, rows, slice(UP_LANES, SUB), cwv_ref, cbv_ref, cols, 1.0)
        silu = half_gate * jnp.tanh(half_gate) + half_gate
        o_ref[row0:row0 + rows, cols] = (silu * val).astype(BF16)

    units = []
    for k in range(UP_NSUB):
        if k < UP_NSUB - UP_SPLIT_TAIL:
            units.append((k, 0, UP_TM))
        else:
            units += [(k, 0, UP_TM // 2), (k, UP_TM // 2, UP_TM // 2)]
    for slot, (k, row0, rows) in enumerate(units):
        unit(k, slot % 2, row0, rows)


def _ffn_up(h2, w_up, conv_w, conv_b):
    n = h2.shape[0]
    halo_blocks = UP_TM // HALO_FETCH
    last_halo = n // HALO_FETCH - 1
    return pl.pallas_call(
        _ffn_up_kernel,
        out_shape=jax.ShapeDtypeStruct((n, D_FF), BF16),
        grid=(n // UP_TM, UP_NJ),
        in_specs=[
            pl.BlockSpec((HALO_FETCH, D_MODEL), lambda i, j: (jnp.maximum(i * halo_blocks - 1, 0), 0)),
            pl.BlockSpec((UP_TM, D_MODEL), lambda i, j: (i, 0)),
            pl.BlockSpec((HALO_FETCH, D_MODEL),
                         lambda i, j: (jnp.minimum((i + 1) * halo_blocks, last_halo), 0)),
            pl.BlockSpec((D_MODEL, UP_TN), lambda i, j: (0, j)),
            pl.BlockSpec((D_MODEL, UP_TN), lambda i, j: (0, UP_NJ + j)),
            pl.BlockSpec((3, UP_TN), lambda i, j: (0, j)),
            pl.BlockSpec((3, UP_TN), lambda i, j: (0, UP_NJ + j)),
            pl.BlockSpec((1, UP_TN), lambda i, j: (0, j)),
            pl.BlockSpec((1, UP_TN), lambda i, j: (0, UP_NJ + j)),
        ],
        out_specs=pl.BlockSpec((UP_TM, UP_TN), lambda i, j: (i, j)),
        scratch_shapes=[pltpu.VMEM((UP_ROWS, D_MODEL), BF16),
                        pltpu.VMEM((2, UP_ROWS, SUB), F32)],
        compiler_params=_params("arbitrary", "arbitrary"),
        name="ffn_up",
    )(h2, h2, h2, w_up, w_up, conv_w, conv_w, conv_b, conv_b)


DN_TM = 512


def _ffn_down_kernel(a_ref, w_ref, x_ref, o_ref):
    lhs = a_ref[...]
    for k in range(D_MODEL // SUB):
        cols = slice(k * SUB, (k + 1) * SUB)
        o_ref[:, cols] = x_ref[:, cols] + jnp.dot(lhs, w_ref[:, cols], preferred_element_type=F32)


def _ffn_down(act, w_down, x1):
    n = act.shape[0]
    row_block = lambda width: pl.BlockSpec((DN_TM, width), lambda i: (i, 0))
    return pl.pallas_call(
        _ffn_down_kernel,
        out_shape=jax.ShapeDtypeStruct((n, D_MODEL), F32),
        grid=(n // DN_TM,),
        in_specs=[row_block(D_FF), _resident((D_FF, D_MODEL)), row_block(D_MODEL)],
        out_specs=row_block(D_MODEL),
        compiler_params=_params("arbitrary"),
        name="ffn_down",
    )(act, w_down, x1)


def _rope_tables():
    half = HEAD_DIM // 2
    inv_freq = ROPE_THETA ** (-np.arange(half, dtype=np.float64) * (2.0 / HEAD_DIM))
    ang = np.arange(SEQ, dtype=np.float64)[:, None] * inv_freq[None, :]
    cos, sin = np.cos(ang).astype(np.float32), np.sin(ang).astype(np.float32)
    return (jnp.asarray(np.concatenate([cos, cos], axis=-1)),
            jnp.asarray(np.concatenate([sin, sin], axis=-1)))


def _paired_gain(gain):
    lo, hi = gain[:HALF_LANES], gain[HALF_LANES:]
    return jnp.stack([jnp.concatenate([lo, lo]), jnp.concatenate([hi, hi])])


IN_COL_VA = ROPE_DIM
IN_COL_QB = IN_COL_VA + A_KV_DIM
IN_COL_VB = IN_COL_QB + NORM_DIM
IN_COL_GATE = IN_COL_VB + B_DIM
IN_HEAD_OUTPUTS = [ROPE_COLUMNS, [(IN_COL_QB, IN_COL_VB)], [(IN_COL_VB, IN_COL_GATE)],
                   [(IN_COL_VA, IN_COL_QB)]]


def kernel(x, norm_mix, w_in, a_q_norm, a_k_norm, a_sink, b_q_norm, b_k_norm, b_rpb,
           w_branch_a, w_branch_b, w_out, norm_ffn, w_up, conv_w, conv_b, w_down):
    batch, seq, d_model = x.shape
    assert (seq, d_model) == (SEQ, D_MODEL)
    n = batch * seq
    cos, sin = _rope_tables()
    scale = LOG2E / math.sqrt(HEAD_DIM)
    x2 = x.reshape(n, d_model)
    for l in range(norm_mix.shape[0]):
        w_gate = w_in[l][:, IN_COL_GATE:].astype(BF16)
        gates, h, w_rope, w_norm, w_vb, w_va, w_up_bf = _in_gate(
            x2, norm_mix[l][None], w_gate, [(w_in[l], IN_HEAD_OUTPUTS), (w_up[l], None)])
        gains = (_paired_gain(a_q_norm[l] * scale), _paired_gain(a_k_norm[l]),
                 b_q_norm[l][None] * scale, b_k_norm[l][None])
        qk_a, qk_b, vals = _in_heads(h, (w_rope, w_norm, w_vb, w_va), cos, sin, gains, [])
        out_a, wa_bf, wb_bf, w_out_bf, w_down_bf = _attn_a(
            qk_a, vals, a_sink[l] * LOG2E, batch,
            [w_branch_a[l], w_branch_b[l], w_out[l], w_down[l]])
        out_b = _attn_b(qk_b.reshape(n // GRID_W, GRID_W, NORM_DIM),
                        vals.reshape(n // GRID_W, GRID_W, PLAIN_DIM), _attn_b_strips(b_rpb[l]), batch)
        x1, h2 = _merge(x2, out_a, out_b.reshape(n, B_DIM), gates, wa_bf, wb_bf, w_out_bf,
                        norm_ffn[l][None])
        act = _ffn_up(h2, w_up_bf, conv_w[l], conv_b[l][None])
        x2 = _ffn_down(act, w_down_bf, x1)
    return x2.reshape(batch, seq, d_model)
```

```python
import math

import numpy as np
import jax
import jax.numpy as jnp
from jax import lax
from jax.experimental import pallas as pl
from jax.experimental.pallas import tpu as pltpu

D_MODEL = 2048
SEQ = 4096
HEAD_DIM = 128
A_Q_HEADS = 8
A_KV_HEADS = 2
A_GROUP = A_Q_HEADS // A_KV_HEADS
WINDOW = 128
B_HEADS = 8
GRID_W = 64
WIN_H = 8
WIN_W = 16
D_FF = 5632
ROPE_THETA = 10000.0
EPS = 1e-6
NEG = -1e30
LOG2E = math.log2(math.e)

A_Q_DIM = A_Q_HEADS * HEAD_DIM
A_KV_DIM = A_KV_HEADS * HEAD_DIM
B_DIM = B_HEADS * HEAD_DIM

VMEM_LIMIT_BYTES = 56 * 1024 * 1024
MXU_COLS = 256
SUB = MXU_COLS
HEADS_PER_SUB = SUB // HEAD_DIM

BF16 = jnp.bfloat16
F32 = jnp.float32


def _params(*semantics):
    return pltpu.CompilerParams(dimension_semantics=semantics, vmem_limit_bytes=VMEM_LIMIT_BYTES)


def _resident(shape):
    return pl.BlockSpec(shape, lambda *_: (0,) * len(shape), pipeline_mode=pl.Buffered(1))


def _rmsnorm(x, gain):
    ms = jnp.mean(x * x, axis=-1, keepdims=True)
    return x * lax.rsqrt(ms + EPS) * gain


class _CastJob:
    def __init__(self, src, steps, outputs=None, step_of=lambda i: i):
        rows, cols = src.shape
        self.src = src
        self.outputs = outputs or [[(0, cols)]]
        self.slab = rows // steps
        assert self.slab * steps == rows
        self.span = max(hi for pieces in self.outputs for _, hi in pieces)
        self.widths = [sum(hi - lo for lo, hi in pieces) for pieces in self.outputs]
        self.index_map = lambda *idx: (step_of(*idx), 0)

    def in_spec(self):
        return pl.BlockSpec((self.slab, self.span), self.index_map)

    def out_specs(self):
        return [pl.BlockSpec((self.slab, width), self.index_map) for width in self.widths]

    def out_shapes(self):
        return [jax.ShapeDtypeStruct((self.src.shape[0], width), BF16) for width in self.widths]

    def run(self, src_ref, out_refs):
        for pieces, out_ref in zip(self.outputs, out_refs):
            parts = [src_ref[:, lo:hi] for lo, hi in pieces]
            value = parts[0] if len(parts) == 1 else jnp.concatenate(parts, axis=1)
            out_ref[...] = value.astype(BF16)


def _with_casts(body, jobs, n_in, n_out):
    n_cast_out = sum(len(j.outputs) for j in jobs)

    def wrapped(*refs):
        ins = refs[:n_in]
        srcs = refs[n_in:n_in + len(jobs)]
        outs = refs[n_in + len(jobs):n_in + len(jobs) + n_out]
        cast_outs = list(refs[n_in + len(jobs) + n_out:n_in + len(jobs) + n_out + n_cast_out])
        scratch = refs[n_in + len(jobs) + n_out + n_cast_out:]
        for job, src_ref in zip(jobs, srcs):
            job.run(src_ref, [cast_outs.pop(0) for _ in job.outputs])
        body(*ins, *outs, *scratch)

    return wrapped


def _sigmoid(x):
    return 0.5 * jnp.tanh(0.5 * x) + 0.5


IN_TM = 1024
GATE_TM = 512
GATE_DIM = 2 * D_MODEL
ROPE_DIM = A_Q_DIM + A_KV_DIM
NORM_DIM = 2 * B_DIM
PLAIN_DIM = B_DIM + A_KV_DIM


def _in_gate_kernel(x_ref, g_ref, w_ref, o_ref, h_ref):
    h_ref[...] = _rmsnorm(x_ref[...], g_ref[...]).astype(BF16)
    lhs = h_ref[...]
    for k in range(GATE_DIM // SUB):
        cols = slice(k * SUB, (k + 1) * SUB)
        acc = jnp.dot(lhs, w_ref[:, cols], preferred_element_type=F32)
        o_ref[:, cols] = _sigmoid(acc).astype(BF16)


GATE_PROLOGUE = GATE_DIM // SUB


def _in_gate_body(jobs):
    n_cast_out = sum(len(j.outputs) for j in jobs)

    def body(x_ref, g_ref, wf_ref, *refs):
        srcs = refs[:len(jobs)]
        o_ref, h_ref = refs[len(jobs):len(jobs) + 2]
        cast_outs = list(refs[len(jobs) + 2:len(jobs) + 2 + n_cast_out])
        w_ref = refs[-1]
        step = pl.program_id(0)
        for p in range(GATE_PROLOGUE):
            @pl.when(step == p)
            def _(p=p):
                w_ref[:, p * SUB:(p + 1) * SUB] = wf_ref[...].astype(BF16)

        @pl.when(step >= GATE_PROLOGUE)
        def _():
            for job, src_ref in zip(jobs, srcs):
                job.run(src_ref, [cast_outs.pop(0) for _ in job.outputs])
            _in_gate_kernel(x_ref, g_ref, w_ref, o_ref, h_ref)

    return body


def _in_gate(x2, norm_g, w_in, gate_col, casts):
    n = x2.shape[0]
    steps = n // GATE_TM
    tile = lambda i: jnp.maximum(i - GATE_PROLOGUE, 0)
    jobs = [_CastJob(src, steps, ranges, step_of=tile) for src, ranges in casts]
    row_block = lambda width: pl.BlockSpec((GATE_TM, width), lambda i: (tile(i), 0))
    first_group = gate_col // SUB
    weight_group = pl.BlockSpec(
        (D_MODEL, SUB), lambda i: (0, first_group + jnp.minimum(i, GATE_PROLOGUE - 1)))
    return pl.pallas_call(
        _in_gate_body(jobs),
        out_shape=[jax.ShapeDtypeStruct((n, GATE_DIM), BF16),
                   jax.ShapeDtypeStruct((n, D_MODEL), BF16)]
                  + [s for j in jobs for s in j.out_shapes()],
        grid=(GATE_PROLOGUE + steps,),
        in_specs=[row_block(D_MODEL), _resident((1, D_MODEL)), weight_group]
                 + [j.in_spec() for j in jobs],
        out_specs=[row_block(GATE_DIM), row_block(D_MODEL)]
                  + [s for j in jobs for s in j.out_specs()],
        scratch_shapes=[pltpu.VMEM((D_MODEL, GATE_DIM), BF16)],
        compiler_params=_params("arbitrary"),
        name="in_gate",
    )(x2, norm_g, w_in, *[j.src for j in jobs])


def _store_heads(o_ref, k, acc, fn):
    for t in range(HEADS_PER_SUB):
        col = k * SUB + t * HEAD_DIM
        o_ref[:, col:col + HEAD_DIM] = fn(acc[:, t * HEAD_DIM:(t + 1) * HEAD_DIM]).astype(BF16)


def _pair_columns(head_a, head_b):
    half = HEAD_DIM // 2
    a, b = head_a * HEAD_DIM, head_b * HEAD_DIM
    return [(a, a + half), (b, b + half), (a + half, a + HEAD_DIM), (b + half, b + HEAD_DIM)]


ROPE_PAIRS = [(p, p + A_GROUP) for p in range(A_GROUP)] + [(A_Q_HEADS, A_Q_HEADS + 1)]
assert A_KV_HEADS == 2 and len(ROPE_PAIRS) * SUB == ROPE_DIM
ROPE_COLUMNS = [piece for pair in ROPE_PAIRS for piece in _pair_columns(*pair)]
HALF_LANES = HEAD_DIM // 2


def _in_rope_kernel(h_ref, w_ref, cos_ref, sin_ref, gq_ref, gk_ref, o_ref):
    lhs = h_ref[...]
    first_head = lax.broadcasted_iota(jnp.int32, (IN_TM, HEAD_DIM), 1) < HALF_LANES
    for k in range(ROPE_DIM // SUB):
        acc = jnp.dot(lhs, w_ref[:, k * SUB:(k + 1) * SUB], preferred_element_type=F32)
        gain_ref = gq_ref if k * SUB < A_Q_DIM else gk_ref
        lo, hi = acc[:, :HEAD_DIM], acc[:, HEAD_DIM:]
        sq = lo * lo + hi * hi
        ss_a = jnp.sum(jnp.where(first_head, sq, 0.0), axis=-1, keepdims=True)
        ss_b = jnp.sum(jnp.where(first_head, 0.0, sq), axis=-1, keepdims=True)
        inv = jnp.where(first_head, lax.rsqrt(ss_a * (1.0 / HEAD_DIM) + EPS),
                        lax.rsqrt(ss_b * (1.0 / HEAD_DIM) + EPS))
        lo = lo * inv * gain_ref[0:1, :]
        hi = hi * inv * gain_ref[1:2, :]
        cos, sin = cos_ref[...], sin_ref[...]
        o_ref[:, k * SUB:k * SUB + HEAD_DIM] = (lo * cos - hi * sin).astype(BF16)
        o_ref[:, k * SUB + HEAD_DIM:(k + 1) * SUB] = (hi * cos + lo * sin).astype(BF16)


def _in_norm_kernel(h_ref, w_ref, gq_ref, gk_ref, o_ref):
    lhs = h_ref[...]
    for k in range(NORM_DIM // SUB):
        acc = jnp.dot(lhs, w_ref[:, k * SUB:(k + 1) * SUB], preferred_element_type=F32)
        gain_ref = gq_ref if k * SUB < B_DIM else gk_ref
        _store_heads(o_ref, k, acc, lambda a: _rmsnorm(a, gain_ref[...]))


def _in_plain_kernel(h_ref, wvb_ref, wva_ref, o_ref):
    lhs = h_ref[...]
    for k in range(PLAIN_DIM // SUB):
        cols = slice(k * SUB, (k + 1) * SUB)
        w = wvb_ref[:, cols] if k * SUB < B_DIM else wva_ref[:, k * SUB - B_DIM:(k + 1) * SUB - B_DIM]
        o_ref[:, cols] = jnp.dot(lhs, w, preferred_element_type=F32).astype(BF16)


def _in_heads_kernel(h_ref, w_rope_ref, w_norm_ref, w_vb_ref, w_va_ref, cos_ref, sin_ref,
                     gqa_ref, gka_ref, gqb_ref, gkb_ref, rope_ref, norm_ref, plain_ref):
    _in_rope_kernel(h_ref, w_rope_ref, cos_ref, sin_ref, gqa_ref, gka_ref, rope_ref)
    _in_norm_kernel(h_ref, w_norm_ref, gqb_ref, gkb_ref, norm_ref)
    _in_plain_kernel(h_ref, w_vb_ref, w_va_ref, plain_ref)


def _in_heads(h, weights, cos, sin, gains, casts):
    n = h.shape[0]
    steps = n // IN_TM
    seq_tiles = SEQ // IN_TM
    jobs = [_CastJob(src, steps) for src in casts]
    row_block = lambda width: pl.BlockSpec((IN_TM, width), lambda i: (i, 0))
    pos_spec = pl.BlockSpec((IN_TM, HEAD_DIM), lambda i: (i % seq_tiles, 0))
    widths = (ROPE_DIM, NORM_DIM, PLAIN_DIM)
    return pl.pallas_call(
        _with_casts(_in_heads_kernel, jobs, n_in=11, n_out=3),
        out_shape=[jax.ShapeDtypeStruct((n, width), BF16) for width in widths]
                  + [s for j in jobs for s in j.out_shapes()],
        grid=(steps,),
        in_specs=([row_block(D_MODEL)] + [_resident(w.shape) for w in weights]
                  + [pos_spec, pos_spec] + [_resident(g.shape) for g in gains]
                  + [j.in_spec() for j in jobs]),
        out_specs=[row_block(width) for width in widths] + [s for j in jobs for s in j.out_specs()],
        compiler_params=_params("arbitrary"),
        name="in_heads",
    )(h, *weights, cos, sin, *gains, *[j.src for j in jobs])


A_TQ = 1024
A_BLK = WINDOW
A_NB = A_TQ // A_BLK
A_CHUNKS = SEQ // A_TQ
A_SPLIT = 2


def _attn_a_kernel(sink_ref, q_ref, kp_ref, km_ref, kn_ref, vp_ref, vm_ref, vn_ref, o_ref):
    c = pl.program_id(1)
    k_all = jnp.concatenate([kp_ref[...], km_ref[...], kn_ref[...]], axis=0)
    v_all = jnp.concatenate([vp_ref[...], vm_ref[...], vn_ref[...]], axis=0)
    qq = lax.broadcasted_iota(jnp.int32, (A_BLK, 3 * A_BLK), 0)
    kk = lax.broadcasted_iota(jnp.int32, (A_BLK, 3 * A_BLK), 1)
    d = kk - qq
    band = (d >= 0) & (d <= 2 * WINDOW)
    ones = jnp.ones((3 * A_BLK, HEAD_DIM), BF16)
    lane = lax.broadcasted_iota(jnp.int32, (A_BLK, SUB), 1) % HEAD_DIM
    own_lanes = (lane < HALF_LANES, lane >= HALF_LANES)
    for n in range(A_NB):
        valid = band
        if n == 0:
            valid = valid & ((kk >= A_BLK) | (c > 0))
        if n == A_NB - 1:
            valid = valid & ((kk < 2 * A_BLK) | (c < A_CHUNKS - 1))
        k = k_all[n * A_BLK:(n + 3) * A_BLK, :]
        q_pairs = [q_ref[n * A_BLK:(n + 1) * A_BLK, p * SUB:(p + 1) * SUB] for p in range(A_GROUP)]
        for h in range(A_KV_HEADS):
            v = v_all[n * A_BLK:(n + 3) * A_BLK, h * HEAD_DIM:(h + 1) * HEAD_DIM]
            heads = [h * A_GROUP + g for g in range(A_GROUP)]
            v_ones = jnp.concatenate([v, ones], axis=1)
            for g0 in range(0, A_GROUP, A_SPLIT):
                group = list(range(g0, g0 + A_SPLIT))
                qs = jnp.concatenate([jnp.where(own_lanes[h], q_pairs[g], jnp.zeros_like(q_pairs[g]))
                                      for g in group], axis=0)
                s = lax.dot_general(qs, k, (((1,), (1,)), ((), ())),
                                    preferred_element_type=F32)
                ps, sink_terms = [], []
                for i, g in enumerate(group):
                    sg = jnp.where(valid, s[i * A_BLK:(i + 1) * A_BLK], NEG)
                    m = jnp.maximum(jnp.max(sg, axis=-1, keepdims=True), sink_ref[heads[g]])
                    ps.append(jnp.exp2(sg - m).astype(BF16))
                    sink_terms.append(jnp.exp2(sink_ref[heads[g]] - m))
                o = jnp.dot(jnp.concatenate(ps, axis=0), v_ones, preferred_element_type=F32)
                for i, g in enumerate(group):
                    t = heads[g]
                    og = o[i * A_BLK:(i + 1) * A_BLK]
                    og = og[:, :HEAD_DIM] / (og[:, HEAD_DIM:] + sink_terms[i])
                    o_ref[n * A_BLK:(n + 1) * A_BLK, t * HEAD_DIM:(t + 1) * HEAD_DIM] = og.astype(BF16)


def _attn_a(qk, vals, sink, batch, casts):
    n = qk.shape[0]
    blk_per_seq = SEQ // A_BLK
    k_col = A_Q_DIM // A_KV_DIM
    v_col = B_DIM // A_KV_DIM
    jobs = [_CastJob(src, batch * A_CHUNKS, step_of=lambda b, c, *_: b * A_CHUNKS + c)
            for src in casts]

    def prev_map(col):
        return lambda b, c, *_: (b * blk_per_seq + jnp.maximum(c * A_NB - 1, 0), col)

    def main_map(col):
        return lambda b, c, *_: (b * A_CHUNKS + c, col)

    def next_map(col):
        return lambda b, c, *_: (b * blk_per_seq + jnp.minimum(c * A_NB + A_NB, blk_per_seq - 1), col)

    halo = lambda m: pl.BlockSpec((A_BLK, A_KV_DIM), m)
    main = lambda m: pl.BlockSpec((A_TQ, A_KV_DIM), m)
    return pl.pallas_call(
        _with_casts(_attn_a_kernel, jobs, n_in=8, n_out=1),
        out_shape=[jax.ShapeDtypeStruct((n, A_Q_DIM), BF16)] + [s for j in jobs for s in j.out_shapes()],
        grid_spec=pltpu.PrefetchScalarGridSpec(
            num_scalar_prefetch=1,
            grid=(batch, A_CHUNKS),
            in_specs=[
                pl.BlockSpec((A_TQ, A_Q_DIM), main_map(0)),
                halo(prev_map(k_col)), main(main_map(k_col)), halo(next_map(k_col)),
                halo(prev_map(v_col)), main(main_map(v_col)), halo(next_map(v_col)),
            ] + [j.in_spec() for j in jobs],
            out_specs=[pl.BlockSpec((A_TQ, A_Q_DIM), main_map(0))]
                      + [s for j in jobs for s in j.out_specs()],
        ),
        compiler_params=_params("arbitrary", "arbitrary"),
        name="attn_a",
    )(sink, qk, qk, qk, qk, vals, vals, vals, *[j.src for j in jobs])


GRID_ROWS = SEQ // GRID_W
B_QR = 4
B_KR = B_QR + WIN_H
B_NRB = GRID_ROWS // B_QR
B_NQ = B_QR * GRID_W
B_NK = B_KR * GRID_W
B_KBLOCKS = B_KR // B_QR
B_KS_MAX = B_NRB - B_KBLOCKS
B_STEP_BLOCKS = 4
B_STEP_KBLOCKS = B_KBLOCKS + B_STEP_BLOCKS - 1
B_PATTERN_ROWBLOCKS = (0, 1, B_NRB - 1)


def _attn_b_row_windows():
    starts, row_ok = [], []
    for rb in B_PATTERN_ROWBLOCKS:
        ks = int(np.clip(rb - 1, 0, B_KS_MAX))
        qr = rb * B_QR + np.arange(B_QR)
        kr = ks * B_QR + np.arange(B_KR)
        rs = np.clip(qr - WIN_H // 2, 0, GRID_ROWS - WIN_H)
        row_ok.append((kr[None, :] >= rs[:, None]) & (kr[None, :] < rs[:, None] + WIN_H))
        starts.append(kr[0] - qr + WIN_H - 1)
    return np.stack(starts), np.stack(row_ok)


B_ROW_START, B_ROW_OK = _attn_b_row_windows()
B_STRIP_LO = max(0, -int(B_ROW_START.min()))
B_NSTRIP = int(B_ROW_START.max()) + B_KR - 1 + B_STRIP_LO
B_LANES = 2 * GRID_W


def _attn_b_build_bias(strip_ref, bias_ref):
    qc = lax.broadcasted_iota(jnp.int32, (GRID_W, B_LANES), 0)
    lane = lax.broadcasted_iota(jnp.int32, (GRID_W, B_LANES), 1)
    kc = lane % GRID_W
    cs = jnp.clip(qc - WIN_W // 2, 0, GRID_W - WIN_W)
    col_ok = (kc >= cs) & (kc < cs + WIN_W)
    first_row = lane < GRID_W
    for h in range(B_HEADS):
        tiles = {}

        def tile(i):
            if i not in tiles:
                rows = jnp.broadcast_to(strip_ref[h, i:i + 1, :], (GRID_W, B_LANES))
                toeplitz = pltpu.roll(rows, 0, axis=1, stride=1, stride_axis=0)
                tiles[i] = jnp.where(col_ok, toeplitz, NEG)
            return tiles[i]

        for p in range(len(B_PATTERN_ROWBLOCKS)):
            for qr in range(B_QR):
                for c in range(B_KR // 2):
                    ok0, ok1 = B_ROW_OK[p, qr, 2 * c], B_ROW_OK[p, qr, 2 * c + 1]
                    if ok0 or ok1:
                        t = tile(int(B_ROW_START[p, qr]) + 2 * c + B_STRIP_LO)
                        if not ok1:
                            t = jnp.where(first_row, t, NEG)
                        if not ok0:
                            t = jnp.where(first_row, NEG, t)
                    else:
                        t = jnp.full((GRID_W, B_LANES), NEG, F32)
                    bias_ref[p, h, qr * GRID_W:(qr + 1) * GRID_W, c * B_LANES:(c + 1) * B_LANES] = t


def _attn_b_key_base(step):
    return jnp.clip(step * B_STEP_BLOCKS - 1, 0, B_NRB - B_STEP_KBLOCKS)


def _attn_b_kernel(q_ref, *refs):
    k_refs = refs[:B_STEP_KBLOCKS]
    v_refs = refs[B_STEP_KBLOCKS:2 * B_STEP_KBLOCKS]
    strip_ref, o_ref, bias_ref, k_buf, v_buf = refs[2 * B_STEP_KBLOCKS:]
    step = pl.program_id(1)

    @pl.when((pl.program_id(0) == 0) & (step == 0))
    def _():
        _attn_b_build_bias(strip_ref, bias_ref)

    for t in range(B_STEP_KBLOCKS):
        k_buf[t * B_QR:(t + 1) * B_QR] = k_refs[t][...]
        v_buf[t * B_QR:(t + 1) * B_QR] = v_refs[t][...]

    ones = jnp.ones((B_NK, HEAD_DIM), BF16)
    for sb in range(B_STEP_BLOCKS):
        r = step * B_STEP_BLOCKS + sb
        pattern = jnp.where(r == 0, 0, jnp.where(r == B_NRB - 1, 2, 1))
        first_row = (jnp.clip(r - 1, 0, B_KS_MAX) - _attn_b_key_base(step)) * B_QR
        rows = slice(sb * B_QR, (sb + 1) * B_QR)
        for h in range(B_HEADS):
            hd = slice(h * HEAD_DIM, (h + 1) * HEAD_DIM)
            q = q_ref[rows, :, hd].reshape(B_NQ, HEAD_DIM)
            k = k_buf[pl.ds(first_row, B_KR), :, hd].reshape(B_NK, HEAD_DIM)
            v = v_buf[pl.ds(first_row, B_KR), :, hd].reshape(B_NK, HEAD_DIM)
            s = lax.dot_general(q, k, (((1,), (1,)), ((), ())),
                                preferred_element_type=F32) + bias_ref[pattern, h]
            m = jnp.max(s, axis=-1, keepdims=True)
            p = jnp.exp2(s - m).astype(BF16)
            o = jnp.dot(p, jnp.concatenate([v, ones], axis=1), preferred_element_type=F32)
            o = o[:, :HEAD_DIM] / o[:, HEAD_DIM:]
            o_ref[rows, :, hd] = o.reshape(B_QR, GRID_W, HEAD_DIM).astype(BF16)


def _attn_b_strips(rpb):
    rows = B_NSTRIP + 1
    table = jnp.pad(rpb.astype(F32) * LOG2E,
                    ((0, 0), (B_STRIP_LO, rows - B_STRIP_LO - rpb.shape[1]), (0, 0)))
    first, second = table[:, :-1], table[:, 1:]
    gap = jnp.zeros(first.shape[:2] + (GRID_W - 2 * WIN_W + 1,), F32)
    strips = jnp.concatenate(
        [first[..., WIN_W - 1:], gap, second, gap, first[..., :WIN_W - 1]], axis=-1)
    assert strips.shape == (B_HEADS, B_NSTRIP, B_LANES)
    return strips


def _attn_b(qk3, vals3, strips, batch):
    rows = qk3.shape[0]

    steps = B_NRB // B_STEP_BLOCKS

    def kv_spec(col, t):
        return pl.BlockSpec((B_QR, GRID_W, B_DIM),
                            lambda b, s: (b * B_NRB + _attn_b_key_base(s) + t, 0, col))

    q_spec = pl.BlockSpec((B_STEP_BLOCKS * B_QR, GRID_W, B_DIM), lambda b, s: (b * steps + s, 0, 0))
    window = (B_STEP_KBLOCKS * B_QR, GRID_W, B_DIM)
    return pl.pallas_call(
        _attn_b_kernel,
        out_shape=jax.ShapeDtypeStruct((rows, GRID_W, B_DIM), BF16),
        grid=(batch, steps),
        in_specs=([q_spec] + [kv_spec(1, t) for t in range(B_STEP_KBLOCKS)]
                  + [kv_spec(0, t) for t in range(B_STEP_KBLOCKS)]
                  + [_resident((B_HEADS, B_NSTRIP, B_LANES))]),
        out_specs=q_spec,
        scratch_shapes=[pltpu.VMEM((len(B_PATTERN_ROWBLOCKS), B_HEADS, B_NQ, B_NK), F32),
                        pltpu.VMEM(window, BF16), pltpu.VMEM(window, BF16)],
        compiler_params=_params("arbitrary", "arbitrary"),
        name="attn_b",
    )(qk3, *[qk3] * B_STEP_KBLOCKS, *[vals3] * B_STEP_KBLOCKS, strips)


MG_TM = 512


def _merge_kernel(x_ref, oa_ref, ob_ref, ga_ref, gb_ref, wa_ref, wb_ref, wo_ref, g_ref,
                  x1_ref, h2_ref):
    a = jnp.dot(oa_ref[...], wa_ref[...], preferred_element_type=F32)
    b = jnp.dot(ob_ref[...], wb_ref[...], preferred_element_type=F32)
    merged = ga_ref[...].astype(F32) * a + gb_ref[...].astype(F32) * b
    y = jnp.dot(merged.astype(BF16), wo_ref[...], preferred_element_type=F32)
    x1 = x_ref[...] + y
    x1_ref[...] = x1
    h2_ref[...] = _rmsnorm(x1, g_ref[...]).astype(BF16)


def _merge(x2, out_a, out_b, gates, wa, wb, wo, norm_g):
    n = x2.shape[0]
    row_block = lambda width, col=0: pl.BlockSpec((MG_TM, width), lambda i: (i, col))
    return pl.pallas_call(
        _merge_kernel,
        out_shape=(jax.ShapeDtypeStruct((n, D_MODEL), F32),
                   jax.ShapeDtypeStruct((n, D_MODEL), BF16)),
        grid=(n // MG_TM,),
        in_specs=[
            row_block(D_MODEL), row_block(A_Q_DIM), row_block(B_DIM),
            row_block(D_MODEL, 0), row_block(D_MODEL, 1),
            _resident((A_Q_DIM, D_MODEL)), _resident((B_DIM, D_MODEL)),
            _resident((D_MODEL, D_MODEL)), _resident((1, D_MODEL)),
        ],
        out_specs=(row_block(D_MODEL), row_block(D_MODEL)),
        compiler_params=_params("arbitrary"),
        name="merge",
    )(x2, out_a, out_b, gates, gates, wa, wb, wo, norm_g)


UP_TM = 1024
UP_NJ = 4
UP_TN = D_FF // UP_NJ
UP_LANES = SUB // 2
UP_NSUB = UP_TN // UP_LANES
HALO_FETCH = 16
HALO = 8
UP_ROWS = UP_TM + 2 * HALO
UP_SPLIT_TAIL = 2


def _ffn_up_kernel(hp_ref, hm_ref, hn_ref, wg_ref, wv_ref, cwg_ref, cwv_ref, cbg_ref, cbv_ref,
                   o_ref, lhs_ref, u_ref):
    i = pl.program_id(0)
    tiles_per_seq = SEQ // UP_TM

    @pl.when(pl.program_id(1) == 0)
    def _():
        first = (i % tiles_per_seq) == 0
        last = (i % tiles_per_seq) == tiles_per_seq - 1
        above_rows, below_rows = hp_ref[HALO_FETCH - HALO:, :], hn_ref[:HALO, :]
        lhs_ref[0:HALO, :] = jnp.where(first, jnp.zeros_like(above_rows), above_rows)
        lhs_ref[HALO:HALO + UP_TM, :] = hm_ref[...]
        lhs_ref[HALO + UP_TM:, :] = jnp.where(last, jnp.zeros_like(below_rows), below_rows)

    def conv(slot, rows, lanes, cw_ref, cb_ref, cols, scale):
        below = u_ref[slot, HALO - 1:HALO - 1 + rows, lanes]
        mid = u_ref[slot, HALO:HALO + rows, lanes]
        above = u_ref[slot, HALO + 1:HALO + 1 + rows, lanes]
        c0, c1, c2 = (scale * cw_ref[t:t + 1, cols] for t in range(3))
        y = mid * c1 + scale * cb_ref[:, cols]
        y = y + below * c0
        return y + above * c2

    def unit(k, slot, row0, rows):
        cols = slice(k * UP_LANES, (k + 1) * UP_LANES)
        w = jnp.concatenate([wg_ref[:, cols], wv_ref[:, cols]], axis=1)
        u_ref[slot, 0:rows + 2 * HALO, :] = jnp.dot(
            lhs_ref[row0:row0 + rows + 2 * HALO, :], w, preferred_element_type=F32)
        half_gate = conv(slot, rows, slice(0, UP_LANES), cwg_ref, cbg_ref, cols, 0.5)
        val = conv(slot, rows, slice(UP_LANES, SUB), cwv_ref, cbv_ref, cols, 1.0)
        silu = half_gate * jnp.tanh(half_gate) + half_gate
        o_ref[row0:row0 + rows, cols] = (silu * val).astype(BF16)

    units = []
    for k in range(UP_NSUB):
        if k < UP_NSUB - UP_SPLIT_TAIL:
            units.append((k, 0, UP_TM))
        else:
            units += [(k, 0, UP_TM // 2), (k, UP_TM // 2, UP_TM // 2)]
    for slot, (k, row0, rows) in enumerate(units):
        unit(k, slot % 2, row0, rows)


def _ffn_up(h2, w_up, conv_w, conv_b):
    n = h2.shape[0]
    halo_blocks = UP_TM // HALO_FETCH
    last_halo = n // HALO_FETCH - 1
    return pl.pallas_call(
        _ffn_up_kernel,
        out_shape=jax.ShapeDtypeStruct((n, D_FF), BF16),
        grid=(n // UP_TM, UP_NJ),
        in_specs=[
            pl.BlockSpec((HALO_FETCH, D_MODEL), lambda i, j: (jnp.maximum(i * halo_blocks - 1, 0), 0)),
            pl.BlockSpec((UP_TM, D_MODEL), lambda i, j: (i, 0)),
            pl.BlockSpec((HALO_FETCH, D_MODEL),
                         lambda i, j: (jnp.minimum((i + 1) * halo_blocks, last_halo), 0)),
            pl.BlockSpec((D_MODEL, UP_TN), lambda i, j: (0, j)),
            pl.BlockSpec((D_MODEL, UP_TN), lambda i, j: (0, UP_NJ + j)),
            pl.BlockSpec((3, UP_TN), lambda i, j: (0, j)),
            pl.BlockSpec((3, UP_TN), lambda i, j: (0, UP_NJ + j)),
            pl.BlockSpec((1, UP_TN), lambda i, j: (0, j)),
            pl.BlockSpec((1, UP_TN), lambda i, j: (0, UP_NJ + j)),
        ],
        out_specs=pl.BlockSpec((UP_TM, UP_TN), lambda i, j: (i, j)),
        scratch_shapes=[pltpu.VMEM((UP_ROWS, D_MODEL), BF16),
                        pltpu.VMEM((2, UP_ROWS, SUB), F32)],
        compiler_params=_params("arbitrary", "arbitrary"),
        name="ffn_up",
    )(h2, h2, h2, w_up, w_up, conv_w, conv_w, conv_b, conv_b)


DN_TM = 512


def _ffn_down_kernel(a_ref, w_ref, x_ref, o_ref):
    lhs = a_ref[...]
    for k in range(D_MODEL // SUB):
        cols = slice(k * SUB, (k + 1) * SUB)
        o_ref[:, cols] = x_ref[:, cols] + jnp.dot(lhs, w_ref[:, cols], preferred_element_type=F32)


def _ffn_down(act, w_down, x1):
    n = act.shape[0]
    row_block = lambda width: pl.BlockSpec((DN_TM, width), lambda i: (i, 0))
    return pl.pallas_call(
        _ffn_down_kernel,
        out_shape=jax.ShapeDtypeStruct((n, D_MODEL), F32),
        grid=(n // DN_TM,),
        in_specs=[row_block(D_FF), _resident((D_FF, D_MODEL)), row_block(D_MODEL)],
        out_specs=row_block(D_MODEL),
        compiler_params=_params("arbitrary"),
        name="ffn_down",
    )(act, w_down, x1)


def _rope_tables():
    half = HEAD_DIM // 2
    inv_freq = ROPE_THETA ** (-np.arange(half, dtype=np.float64) * (2.0 / HEAD_DIM))
    ang = np.arange(SEQ, dtype=np.float64)[:, None] * inv_freq[None, :]
    cos, sin = np.cos(ang).astype(np.float32), np.sin(ang).astype(np.float32)
    return (jnp.asarray(np.concatenate([cos, cos], axis=-1)),
            jnp.asarray(np.concatenate([sin, sin], axis=-1)))


def _paired_gain(gain):
    lo, hi = gain[:HALF_LANES], gain[HALF_LANES:]
    return jnp.stack([jnp.concatenate([lo, lo]), jnp.concatenate([hi, hi])])


IN_COL_VA = ROPE_DIM
IN_COL_QB = IN_COL_VA + A_KV_DIM
IN_COL_VB = IN_COL_QB + NORM_DIM
IN_COL_GATE = IN_COL_VB + B_DIM
IN_HEAD_OUTPUTS = [ROPE_COLUMNS, [(IN_COL_QB, IN_COL_VB)], [(IN_COL_VB, IN_COL_GATE)],
                   [(IN_COL_VA, IN_COL_QB)]]


def kernel(x, norm_mix, w_in, a_q_norm, a_k_norm, a_sink, b_q_norm, b_k_norm, b_rpb,
           w_branch_a, w_branch_b, w_out, norm_ffn, w_up, conv_w, conv_b, w_down):
    batch, seq, d_model = x.shape
    assert (seq, d_model) == (SEQ, D_MODEL)
    n = batch * seq
    cos, sin = _rope_tables()
    scale = LOG2E / math.sqrt(HEAD_DIM)
    x2 = x.reshape(n, d_model)
    for l in range(norm_mix.shape[0]):
        gates, h, w_rope, w_norm, w_vb, w_va, w_up_bf = _in_gate(
            x2, norm_mix[l][None], w_in[l], IN_COL_GATE,
            [(w_in[l], IN_HEAD_OUTPUTS), (w_up[l], None)])
        gains = (_paired_gain(a_q_norm[l] * scale), _paired_gain(a_k_norm[l]),
                 b_q_norm[l][None] * scale, b_k_norm[l][None])
        qk_a, qk_b, vals = _in_heads(h, (w_rope, w_norm, w_vb, w_va), cos, sin, gains, [])
        out_a, wa_bf, wb_bf, w_out_bf, w_down_bf = _attn_a(
            qk_a, vals, a_sink[l] * LOG2E, batch,
            [w_branch_a[l], w_branch_b[l], w_out[l], w_down[l]])
        out_b = _attn_b(qk_b.reshape(n // GRID_W, GRID_W, NORM_DIM),
                        vals.reshape(n // GRID_W, GRID_W, PLAIN_DIM), _attn_b_strips(b_rpb[l]), batch)
        x1, h2 = _merge(x2, out_a, out_b.reshape(n, B_DIM), gates, wa_bf, wb_bf, w_out_bf,
                        norm_ffn[l][None])
        act = _ffn_up(h2, w_up_bf, conv_w[l], conv_b[l][None])
        x2 = _ffn_down(act, w_down_bf, x1)
    return x2.reshape(batch, seq, d_model)
```

```python
import math

import numpy as np
import jax
import jax.numpy as jnp
from jax import lax
from jax.experimental import pallas as pl
from jax.experimental.pallas import tpu as pltpu

D_MODEL = 2048
SEQ = 4096
HEAD_DIM = 128
A_Q_HEADS = 8
A_KV_HEADS = 2
A_GROUP = A_Q_HEADS // A_KV_HEADS
WINDOW = 128
B_HEADS = 8
GRID_W = 64
WIN_H = 8
WIN_W = 16
D_FF = 5632
ROPE_THETA = 10000.0
EPS = 1e-6
NEG = -1e30
LOG2E = math.log2(math.e)

A_Q_DIM = A_Q_HEADS * HEAD_DIM
A_KV_DIM = A_KV_HEADS * HEAD_DIM
B_DIM = B_HEADS * HEAD_DIM

VMEM_LIMIT_BYTES = 56 * 1024 * 1024
MXU_COLS = 256
SUB = MXU_COLS
HEADS_PER_SUB = SUB // HEAD_DIM

BF16 = jnp.bfloat16
F32 = jnp.float32


def _params(*semantics):
    return pltpu.CompilerParams(dimension_semantics=semantics, vmem_limit_bytes=VMEM_LIMIT_BYTES)


def _resident(shape):
    return pl.BlockSpec(shape, lambda *_: (0,) * len(shape), pipeline_mode=pl.Buffered(1))


def _rmsnorm(x, gain):
    ms = jnp.mean(x * x, axis=-1, keepdims=True)
    return x * lax.rsqrt(ms + EPS) * gain


class _CastJob:
    def __init__(self, src, steps, outputs=None, step_of=lambda i: i):
        rows, cols = src.shape
        self.src = src
        self.outputs = outputs or [[(0, cols)]]
        self.slab = rows // steps
        assert self.slab * steps == rows
        self.span = max(hi for pieces in self.outputs for _, hi in pieces)
        self.widths = [sum(hi - lo for lo, hi in pieces) for pieces in self.outputs]
        self.index_map = lambda *idx: (step_of(*idx), 0)

    def in_spec(self):
        return pl.BlockSpec((self.slab, self.span), self.index_map)

    def out_specs(self):
        return [pl.BlockSpec((self.slab, width), self.index_map) for width in self.widths]

    def out_shapes(self):
        return [jax.ShapeDtypeStruct((self.src.shape[0], width), BF16) for width in self.widths]

    def run(self, src_ref, out_refs):
        for pieces, out_ref in zip(self.outputs, out_refs):
            parts = [src_ref[:, lo:hi] for lo, hi in pieces]
            value = parts[0] if len(parts) == 1 else jnp.concatenate(parts, axis=1)
            out_ref[...] = value.astype(BF16)


def _with_casts(body, jobs, n_in, n_out):
    n_cast_out = sum(len(j.outputs) for j in jobs)

    def wrapped(*refs):
        ins = refs[:n_in]
        srcs = refs[n_in:n_in + len(jobs)]
        outs = refs[n_in + len(jobs):n_in + len(jobs) + n_out]
        cast_outs = list(refs[n_in + len(jobs) + n_out:n_in + len(jobs) + n_out + n_cast_out])
        scratch = refs[n_in + len(jobs) + n_out + n_cast_out:]
        for job, src_ref in zip(jobs, srcs):
            job.run(src_ref, [cast_outs.pop(0) for _ in job.outputs])
        body(*ins, *outs, *scratch)

    return wrapped


def _sigmoid(x):
    return 0.5 * jnp.tanh(0.5 * x) + 0.5


IN_TM = 1024
GATE_TM = 512
GATE_DIM = 2 * D_MODEL
ROPE_DIM = A_Q_DIM + A_KV_DIM
NORM_DIM = 2 * B_DIM
PLAIN_DIM = B_DIM + A_KV_DIM


def _in_gate_kernel(x_ref, g_ref, w_ref, o_ref, h_ref):
    h_ref[...] = _rmsnorm(x_ref[...], g_ref[...]).astype(BF16)
    lhs = h_ref[...]
    for k in range(GATE_DIM // SUB):
        cols = slice(k * SUB, (k + 1) * SUB)
        acc = jnp.dot(lhs, w_ref[:, cols], preferred_element_type=F32)
        o_ref[:, cols] = _sigmoid(acc).astype(BF16)


GATE_GROUP = 2 * SUB
GATE_PROLOGUE = GATE_DIM // GATE_GROUP


def _in_gate_body(jobs):
    n_cast_out = sum(len(j.outputs) for j in jobs)

    def body(x_ref, g_ref, wf_ref, *refs):
        srcs = refs[:len(jobs)]
        o_ref, h_ref = refs[len(jobs):len(jobs) + 2]
        cast_outs = list(refs[len(jobs) + 2:len(jobs) + 2 + n_cast_out])
        w_ref = refs[-1]
        step = pl.program_id(0)
        for p in range(GATE_PROLOGUE):
            @pl.when(step == p)
            def _(p=p):
                w_ref[:, p * GATE_GROUP:(p + 1) * GATE_GROUP] = wf_ref[...].astype(BF16)

        @pl.when(step >= GATE_PROLOGUE)
        def _():
            for job, src_ref in zip(jobs, srcs):
                job.run(src_ref, [cast_outs.pop(0) for _ in job.outputs])
            _in_gate_kernel(x_ref, g_ref, w_ref, o_ref, h_ref)

    return body


def _in_gate(x2, norm_g, w_in, gate_col, casts):
    n = x2.shape[0]
    steps = n // GATE_TM
    tile = lambda i: jnp.maximum(i - GATE_PROLOGUE, 0)
    jobs = [_CastJob(src, steps, ranges, step_of=tile) for src, ranges in casts]
    row_block = lambda width: pl.BlockSpec((GATE_TM, width), lambda i: (tile(i), 0))
    first_group = gate_col // GATE_GROUP
    weight_group = pl.BlockSpec(
        (D_MODEL, GATE_GROUP), lambda i: (0, first_group + jnp.minimum(i, GATE_PROLOGUE - 1)))
    return pl.pallas_call(
        _in_gate_body(jobs),
        out_shape=[jax.ShapeDtypeStruct((n, GATE_DIM), BF16),
                   jax.ShapeDtypeStruct((n, D_MODEL), BF16)]
                  + [s for j in jobs for s in j.out_shapes()],
        grid=(GATE_PROLOGUE + steps,),
        in_specs=[row_block(D_MODEL), _resident((1, D_MODEL)), weight_group]
                 + [j.in_spec() for j in jobs],
        out_specs=[row_block(GATE_DIM), row_block(D_MODEL)]
                  + [s for j in jobs for s in j.out_specs()],
        scratch_shapes=[pltpu.VMEM((D_MODEL, GATE_DIM), BF16)],
        compiler_params=_params("arbitrary"),
        name="in_gate",
    )(x2, norm_g, w_in, *[j.src for j in jobs])


def _store_heads(o_ref, k, acc, fn):
    for t in range(HEADS_PER_SUB):
        col = k * SUB + t * HEAD_DIM
        o_ref[:, col:col + HEAD_DIM] = fn(acc[:, t * HEAD_DIM:(t + 1) * HEAD_DIM]).astype(BF16)


def _pair_columns(head_a, head_b):
    half = HEAD_DIM // 2
    a, b = head_a * HEAD_DIM, head_b * HEAD_DIM
    return [(a, a + half), (b, b + half), (a + half, a + HEAD_DIM), (b + half, b + HEAD_DIM)]


ROPE_PAIRS = [(p, p + A_GROUP) for p in range(A_GROUP)] + [(A_Q_HEADS, A_Q_HEADS + 1)]
assert A_KV_HEADS == 2 and len(ROPE_PAIRS) * SUB == ROPE_DIM
ROPE_COLUMNS = [piece for pair in ROPE_PAIRS for piece in _pair_columns(*pair)]
HALF_LANES = HEAD_DIM // 2


def _in_rope_kernel(h_ref, w_ref, cos_ref, sin_ref, gq_ref, gk_ref, o_ref):
    lhs = h_ref[...]
    first_head = lax.broadcasted_iota(jnp.int32, (IN_TM, HEAD_DIM), 1) < HALF_LANES
    for k in range(ROPE_DIM // SUB):
        acc = jnp.dot(lhs, w_ref[:, k * SUB:(k + 1) * SUB], preferred_element_type=F32)
        gain_ref = gq_ref if k * SUB < A_Q_DIM else gk_ref
        lo, hi = acc[:, :HEAD_DIM], acc[:, HEAD_DIM:]
        sq = lo * lo + hi * hi
        ss_a = jnp.sum(jnp.where(first_head, sq, 0.0), axis=-1, keepdims=True)
        ss_b = jnp.sum(jnp.where(first_head, 0.0, sq), axis=-1, keepdims=True)
        inv = jnp.where(first_head, lax.rsqrt(ss_a * (1.0 / HEAD_DIM) + EPS),
                        lax.rsqrt(ss_b * (1.0 / HEAD_DIM) + EPS))
        lo = lo * inv * gain_ref[0:1, :]
        hi = hi * inv * gain_ref[1:2, :]
        cos, sin = cos_ref[...], sin_ref[...]
        o_ref[:, k * SUB:k * SUB + HEAD_DIM] = (lo * cos - hi * sin).astype(BF16)
        o_ref[:, k * SUB + HEAD_DIM:(k + 1) * SUB] = (hi * cos + lo * sin).astype(BF16)


def _in_norm_kernel(h_ref, w_ref, gq_ref, gk_ref, o_ref):
    lhs = h_ref[...]
    for k in range(NORM_DIM // SUB):
        acc = jnp.dot(lhs, w_ref[:, k * SUB:(k + 1) * SUB], preferred_element_type=F32)
        gain_ref = gq_ref if k * SUB < B_DIM else gk_ref
        _store_heads(o_ref, k, acc, lambda a: _rmsnorm(a, gain_ref[...]))


def _in_plain_kernel(h_ref, wvb_ref, wva_ref, o_ref):
    lhs = h_ref[...]
    for k in range(PLAIN_DIM // SUB):
        cols = slice(k * SUB, (k + 1) * SUB)
        w = wvb_ref[:, cols] if k * SUB < B_DIM else wva_ref[:, k * SUB - B_DIM:(k + 1) * SUB - B_DIM]
        o_ref[:, cols] = jnp.dot(lhs, w, preferred_element_type=F32).astype(BF16)


def _in_heads_kernel(h_ref, w_rope_ref, w_norm_ref, w_vb_ref, w_va_ref, cos_ref, sin_ref,
                     gqa_ref, gka_ref, gqb_ref, gkb_ref, rope_ref, norm_ref, plain_ref):
    _in_rope_kernel(h_ref, w_rope_ref, cos_ref, sin_ref, gqa_ref, gka_ref, rope_ref)
    _in_norm_kernel(h_ref, w_norm_ref, gqb_ref, gkb_ref, norm_ref)
    _in_plain_kernel(h_ref, w_vb_ref, w_va_ref, plain_ref)


def _in_heads(h, weights, cos, sin, gains, casts):
    n = h.shape[0]
    steps = n // IN_TM
    seq_tiles = SEQ // IN_TM
    jobs = [_CastJob(src, steps) for src in casts]
    row_block = lambda width: pl.BlockSpec((IN_TM, width), lambda i: (i, 0))
    pos_spec = pl.BlockSpec((IN_TM, HEAD_DIM), lambda i: (i % seq_tiles, 0))
    widths = (ROPE_DIM, NORM_DIM, PLAIN_DIM)
    return pl.pallas_call(
        _with_casts(_in_heads_kernel, jobs, n_in=11, n_out=3),
        out_shape=[jax.ShapeDtypeStruct((n, width), BF16) for width in widths]
                  + [s for j in jobs for s in j.out_shapes()],
        grid=(steps,),
        in_specs=([row_block(D_MODEL)] + [_resident(w.shape) for w in weights]
                  + [pos_spec, pos_spec] + [_resident(g.shape) for g in gains]
                  + [j.in_spec() for j in jobs]),
        out_specs=[row_block(width) for width in widths] + [s for j in jobs for s in j.out_specs()],
        compiler_params=_params("arbitrary"),
        name="in_heads",
    )(h, *weights, cos, sin, *gains, *[j.src for j in jobs])


A_TQ = 1024
A_BLK = WINDOW
A_NB = A_TQ // A_BLK
A_CHUNKS = SEQ // A_TQ
A_SPLIT = 2


def _attn_a_kernel(sink_ref, q_ref, kp_ref, km_ref, kn_ref, vp_ref, vm_ref, vn_ref, o_ref):
    c = pl.program_id(1)
    k_all = jnp.concatenate([kp_ref[...], km_ref[...], kn_ref[...]], axis=0)
    v_all = jnp.concatenate([vp_ref[...], vm_ref[...], vn_ref[...]], axis=0)
    qq = lax.broadcasted_iota(jnp.int32, (A_BLK, 3 * A_BLK), 0)
    kk = lax.broadcasted_iota(jnp.int32, (A_BLK, 3 * A_BLK), 1)
    d = kk - qq
    band = (d >= 0) & (d <= 2 * WINDOW)
    ones = jnp.ones((3 * A_BLK, HEAD_DIM), BF16)
    lane = lax.broadcasted_iota(jnp.int32, (A_BLK, SUB), 1) % HEAD_DIM
    own_lanes = (lane < HALF_LANES, lane >= HALF_LANES)
    for n in range(A_NB):
        valid = band
        if n == 0:
            valid = valid & ((kk >= A_BLK) | (c > 0))
        if n == A_NB - 1:
            valid = valid & ((kk < 2 * A_BLK) | (c < A_CHUNKS - 1))
        k = k_all[n * A_BLK:(n + 3) * A_BLK, :]
        q_pairs = [q_ref[n * A_BLK:(n + 1) * A_BLK, p * SUB:(p + 1) * SUB] for p in range(A_GROUP)]
        for h in range(A_KV_HEADS):
            v = v_all[n * A_BLK:(n + 3) * A_BLK, h * HEAD_DIM:(h + 1) * HEAD_DIM]
            heads = [h * A_GROUP + g for g in range(A_GROUP)]
            v_ones = jnp.concatenate([v, ones], axis=1)
            for g0 in range(0, A_GROUP, A_SPLIT):
                group = list(range(g0, g0 + A_SPLIT))
                qs = jnp.concatenate([jnp.where(own_lanes[h], q_pairs[g], jnp.zeros_like(q_pairs[g]))
                                      for g in group], axis=0)
                s = lax.dot_general(qs, k, (((1,), (1,)), ((), ())),
                                    preferred_element_type=F32)
                ps, sink_terms = [], []
                for i, g in enumerate(group):
                    sg = jnp.where(valid, s[i * A_BLK:(i + 1) * A_BLK], NEG)
                    m = jnp.maximum(jnp.max(sg, axis=-1, keepdims=True), sink_ref[heads[g]])
                    ps.append(jnp.exp2(sg - m).astype(BF16))
                    sink_terms.append(jnp.exp2(sink_ref[heads[g]] - m))
                o = jnp.dot(jnp.concatenate(ps, axis=0), v_ones, preferred_element_type=F32)
                for i, g in enumerate(group):
                    t = heads[g]
                    og = o[i * A_BLK:(i + 1) * A_BLK]
                    og = og[:, :HEAD_DIM] / (og[:, HEAD_DIM:] + sink_terms[i])
                    o_ref[n * A_BLK:(n + 1) * A_BLK, t * HEAD_DIM:(t + 1) * HEAD_DIM] = og.astype(BF16)


def _attn_a(qk, vals, sink, batch, casts):
    n = qk.shape[0]
    blk_per_seq = SEQ // A_BLK
    k_col = A_Q_DIM // A_KV_DIM
    v_col = B_DIM // A_KV_DIM
    jobs = [_CastJob(src, batch * A_CHUNKS, step_of=lambda b, c, *_: b * A_CHUNKS + c)
            for src in casts]

    def prev_map(col):
        return lambda b, c, *_: (b * blk_per_seq + jnp.maximum(c * A_NB - 1, 0), col)

    def main_map(col):
        return lambda b, c, *_: (b * A_CHUNKS + c, col)

    def next_map(col):
        return lambda b, c, *_: (b * blk_per_seq + jnp.minimum(c * A_NB + A_NB, blk_per_seq - 1), col)

    halo = lambda m: pl.BlockSpec((A_BLK, A_KV_DIM), m)
    main = lambda m: pl.BlockSpec((A_TQ, A_KV_DIM), m)
    return pl.pallas_call(
        _with_casts(_attn_a_kernel, jobs, n_in=8, n_out=1),
        out_shape=[jax.ShapeDtypeStruct((n, A_Q_DIM), BF16)] + [s for j in jobs for s in j.out_shapes()],
        grid_spec=pltpu.PrefetchScalarGridSpec(
            num_scalar_prefetch=1,
            grid=(batch, A_CHUNKS),
            in_specs=[
                pl.BlockSpec((A_TQ, A_Q_DIM), main_map(0)),
                halo(prev_map(k_col)), main(main_map(k_col)), halo(next_map(k_col)),
                halo(prev_map(v_col)), main(main_map(v_col)), halo(next_map(v_col)),
            ] + [j.in_spec() for j in jobs],
            out_specs=[pl.BlockSpec((A_TQ, A_Q_DIM), main_map(0))]
                      + [s for j in jobs for s in j.out_specs()],
        ),
        compiler_params=_params("arbitrary", "arbitrary"),
        name="attn_a",
    )(sink, qk, qk, qk, qk, vals, vals, vals, *[j.src for j in jobs])


GRID_ROWS = SEQ // GRID_W
B_QR = 4
B_KR = B_QR + WIN_H
B_NRB = GRID_ROWS // B_QR
B_NQ = B_QR * GRID_W
B_NK = B_KR * GRID_W
B_KBLOCKS = B_KR // B_QR
B_KS_MAX = B_NRB - B_KBLOCKS
B_STEP_BLOCKS = 4
B_STEP_KBLOCKS = B_KBLOCKS + B_STEP_BLOCKS - 1
B_PATTERN_ROWBLOCKS = (0, 1, B_NRB - 1)


def _attn_b_row_windows():
    starts, row_ok = [], []
    for rb in B_PATTERN_ROWBLOCKS:
        ks = int(np.clip(rb - 1, 0, B_KS_MAX))
        qr = rb * B_QR + np.arange(B_QR)
        kr = ks * B_QR + np.arange(B_KR)
        rs = np.clip(qr - WIN_H // 2, 0, GRID_ROWS - WIN_H)
        row_ok.append((kr[None, :] >= rs[:, None]) & (kr[None, :] < rs[:, None] + WIN_H))
        starts.append(kr[0] - qr + WIN_H - 1)
    return np.stack(starts), np.stack(row_ok)


B_ROW_START, B_ROW_OK = _attn_b_row_windows()
B_STRIP_LO = max(0, -int(B_ROW_START.min()))
B_NSTRIP = int(B_ROW_START.max()) + B_KR - 1 + B_STRIP_LO
B_LANES = 2 * GRID_W


def _attn_b_build_bias(strip_ref, bias_ref):
    qc = lax.broadcasted_iota(jnp.int32, (GRID_W, B_LANES), 0)
    lane = lax.broadcasted_iota(jnp.int32, (GRID_W, B_LANES), 1)
    kc = lane % GRID_W
    cs = jnp.clip(qc - WIN_W // 2, 0, GRID_W - WIN_W)
    col_ok = (kc >= cs) & (kc < cs + WIN_W)
    first_row = lane < GRID_W
    for h in range(B_HEADS):
        tiles = {}

        def tile(i):
            if i not in tiles:
                rows = jnp.broadcast_to(strip_ref[h, i:i + 1, :], (GRID_W, B_LANES))
                toeplitz = pltpu.roll(rows, 0, axis=1, stride=1, stride_axis=0)
                tiles[i] = jnp.where(col_ok, toeplitz, NEG)
            return tiles[i]

        for p in range(len(B_PATTERN_ROWBLOCKS)):
            for qr in range(B_QR):
                for c in range(B_KR // 2):
                    ok0, ok1 = B_ROW_OK[p, qr, 2 * c], B_ROW_OK[p, qr, 2 * c + 1]
                    if ok0 or ok1:
                        t = tile(int(B_ROW_START[p, qr]) + 2 * c + B_STRIP_LO)
                        if not ok1:
                            t = jnp.where(first_row, t, NEG)
                        if not ok0:
                            t = jnp.where(first_row, NEG, t)
                    else:
                        t = jnp.full((GRID_W, B_LANES), NEG, F32)
                    bias_ref[p, h, qr * GRID_W:(qr + 1) * GRID_W, c * B_LANES:(c + 1) * B_LANES] = t


def _attn_b_key_base(step):
    return jnp.clip(step * B_STEP_BLOCKS - 1, 0, B_NRB - B_STEP_KBLOCKS)


def _attn_b_kernel(q_ref, *refs):
    k_refs = refs[:B_STEP_KBLOCKS]
    v_refs = refs[B_STEP_KBLOCKS:2 * B_STEP_KBLOCKS]
    strip_ref, o_ref, bias_ref, k_buf, v_buf = refs[2 * B_STEP_KBLOCKS:]
    step = pl.program_id(1)

    @pl.when((pl.program_id(0) == 0) & (step == 0))
    def _():
        _attn_b_build_bias(strip_ref, bias_ref)

    for t in range(B_STEP_KBLOCKS):
        k_buf[t * B_QR:(t + 1) * B_QR] = k_refs[t][...]
        v_buf[t * B_QR:(t + 1) * B_QR] = v_refs[t][...]

    ones = jnp.ones((B_NK, HEAD_DIM), BF16)
    for sb in range(B_STEP_BLOCKS):
        r = step * B_STEP_BLOCKS + sb
        pattern = jnp.where(r == 0, 0, jnp.where(r == B_NRB - 1, 2, 1))
        first_row = (jnp.clip(r - 1, 0, B_KS_MAX) - _attn_b_key_base(step)) * B_QR
        rows = slice(sb * B_QR, (sb + 1) * B_QR)
        for h in range(B_HEADS):
            hd = slice(h * HEAD_DIM, (h + 1) * HEAD_DIM)
            q = q_ref[rows, :, hd].reshape(B_NQ, HEAD_DIM)
            k = k_buf[pl.ds(first_row, B_KR), :, hd].reshape(B_NK, HEAD_DIM)
            v = v_buf[pl.ds(first_row, B_KR), :, hd].reshape(B_NK, HEAD_DIM)
            s = lax.dot_general(q, k, (((1,), (1,)), ((), ())),
                                preferred_element_type=F32) + bias_ref[pattern, h]
            m = jnp.max(s, axis=-1, keepdims=True)
            p = jnp.exp2(s - m).astype(BF16)
            o = jnp.dot(p, jnp.concatenate([v, ones], axis=1), preferred_element_type=F32)
            o = o[:, :HEAD_DIM] / o[:, HEAD_DIM:]
            o_ref[rows, :, hd] = o.reshape(B_QR, GRID_W, HEAD_DIM).astype(BF16)


def _attn_b_strips(rpb):
    rows = B_NSTRIP + 1
    table = jnp.pad(rpb.astype(F32) * LOG2E,
                    ((0, 0), (B_STRIP_LO, rows - B_STRIP_LO - rpb.shape[1]), (0, 0)))
    first, second = table[:, :-1], table[:, 1:]
    gap = jnp.zeros(first.shape[:2] + (GRID_W - 2 * WIN_W + 1,), F32)
    strips = jnp.concatenate(
        [first[..., WIN_W - 1:], gap, second, gap, first[..., :WIN_W - 1]], axis=-1)
    assert strips.shape == (B_HEADS, B_NSTRIP, B_LANES)
    return strips


def _attn_b(qk3, vals3, strips, batch):
    rows = qk3.shape[0]

    steps = B_NRB // B_STEP_BLOCKS

    def kv_spec(col, t):
        return pl.BlockSpec((B_QR, GRID_W, B_DIM),
                            lambda b, s: (b * B_NRB + _attn_b_key_base(s) + t, 0, col))

    q_spec = pl.BlockSpec((B_STEP_BLOCKS * B_QR, GRID_W, B_DIM), lambda b, s: (b * steps + s, 0, 0))
    window = (B_STEP_KBLOCKS * B_QR, GRID_W, B_DIM)
    return pl.pallas_call(
        _attn_b_kernel,
        out_shape=jax.ShapeDtypeStruct((rows, GRID_W, B_DIM), BF16),
        grid=(batch, steps),
        in_specs=([q_spec] + [kv_spec(1, t) for t in range(B_STEP_KBLOCKS)]
                  + [kv_spec(0, t) for t in range(B_STEP_KBLOCKS)]
                  + [_resident((B_HEADS, B_NSTRIP, B_LANES))]),
        out_specs=q_spec,
        scratch_shapes=[pltpu.VMEM((len(B_PATTERN_ROWBLOCKS), B_HEADS, B_NQ, B_NK), F32),
                        pltpu.VMEM(window, BF16), pltpu.VMEM(window, BF16)],
        compiler_params=_params("arbitrary", "arbitrary"),
        name="attn_b",
    )(qk3, *[qk3] * B_STEP_KBLOCKS, *[vals3] * B_STEP_KBLOCKS, strips)


MG_TM = 512


def _merge_kernel(x_ref, oa_ref, ob_ref, ga_ref, gb_ref, wa_ref, wb_ref, wo_ref, g_ref,
                  x1_ref, h2_ref):
    a = jnp.dot(oa_ref[...], wa_ref[...], preferred_element_type=F32)
    b = jnp.dot(ob_ref[...], wb_ref[...], preferred_element_type=F32)
    merged = ga_ref[...].astype(F32) * a + gb_ref[...].astype(F32) * b
    y = jnp.dot(merged.astype(BF16), wo_ref[...], preferred_element_type=F32)
    x1 = x_ref[...] + y
    x1_ref[...] = x1
    h2_ref[...] = _rmsnorm(x1, g_ref[...]).astype(BF16)


def _merge(x2, out_a, out_b, gates, wa, wb, wo, norm_g):
    n = x2.shape[0]
    row_block = lambda width, col=0: pl.BlockSpec((MG_TM, width), lambda i: (i, col))
    return pl.pallas_call(
        _merge_kernel,
        out_shape=(jax.ShapeDtypeStruct((n, D_MODEL), F32),
                   jax.ShapeDtypeStruct((n, D_MODEL), BF16)),
        grid=(n // MG_TM,),
        in_specs=[
            row_block(D_MODEL), row_block(A_Q_DIM), row_block(B_DIM),
            row_block(D_MODEL, 0), row_block(D_MODEL, 1),
            _resident((A_Q_DIM, D_MODEL)), _resident((B_DIM, D_MODEL)),
            _resident((D_MODEL, D_MODEL)), _resident((1, D_MODEL)),
        ],
        out_specs=(row_block(D_MODEL), row_block(D_MODEL)),
        compiler_params=_params("arbitrary"),
        name="merge",
    )(x2, out_a, out_b, gates, gates, wa, wb, wo, norm_g)


UP_TM = 1024
UP_NJ = 4
UP_TN = D_FF // UP_NJ
UP_LANES = SUB // 2
UP_NSUB = UP_TN // UP_LANES
HALO_FETCH = 16
HALO = 8
UP_ROWS = UP_TM + 2 * HALO
UP_SPLIT_TAIL = 2


def _ffn_up_kernel(hp_ref, hm_ref, hn_ref, wg_ref, wv_ref, cwg_ref, cwv_ref, cbg_ref, cbv_ref,
                   o_ref, lhs_ref, u_ref):
    i = pl.program_id(0)
    tiles_per_seq = SEQ // UP_TM

    @pl.when(pl.program_id(1) == 0)
    def _():
        first = (i % tiles_per_seq) == 0
        last = (i % tiles_per_seq) == tiles_per_seq - 1
        above_rows, below_rows = hp_ref[HALO_FETCH - HALO:, :], hn_ref[:HALO, :]
        lhs_ref[0:HALO, :] = jnp.where(first, jnp.zeros_like(above_rows), above_rows)
        lhs_ref[HALO:HALO + UP_TM, :] = hm_ref[...]
        lhs_ref[HALO + UP_TM:, :] = jnp.where(last, jnp.zeros_like(below_rows), below_rows)

    def conv(slot, rows, lanes, cw_ref, cb_ref, cols, scale):
        below = u_ref[slot, HALO - 1:HALO - 1 + rows, lanes]
        mid = u_ref[slot, HALO:HALO + rows, lanes]
        above = u_ref[slot, HALO + 1:HALO + 1 + rows, lanes]
        c0, c1, c2 = (scale * cw_ref[t:t + 1, cols] for t in range(3))
        y = mid * c1 + scale * cb_ref[:, cols]
        y = y + below * c0
        return y + above * c2

    def unit(k, slot, row0, rows):
        cols = slice(k * UP_LANES, (k + 1) * UP_LANES)
        w = jnp.concatenate([wg_ref[:, cols], wv_ref[:, cols]], axis=1)
        u_ref[slot, 0:rows + 2 * HALO, :] = jnp.dot(
            lhs_ref[row0:row0 + rows + 2 * HALO, :], w, preferred_element_type=F32)
        half_gate = conv(slot, rows, slice(0, UP_LANES), cwg_ref, cbg_ref, cols, 0.5)
        val = conv(slot, rows, slice(UP_LANES, SUB), cwv_ref, cbv_ref, cols, 1.0)
        silu = half_gate * jnp.tanh(half_gate) + half_gate
        o_ref[row0:row0 + rows, cols] = (silu * val).astype(BF16)

    units = []
    for k in range(UP_NSUB):
        if k < UP_NSUB - UP_SPLIT_TAIL:
            units.append((k, 0, UP_TM))
        else:
            units += [(k, 0, UP_TM // 2), (k, UP_TM // 2, UP_TM // 2)]
    for slot, (k, row0, rows) in enumerate(units):
        unit(k, slot % 2, row0, rows)


def _ffn_up(h2, w_up, conv_w, conv_b):
    n = h2.shape[0]
    halo_blocks = UP_TM // HALO_FETCH
    last_halo = n // HALO_FETCH - 1
    return pl.pallas_call(
        _ffn_up_kernel,
        out_shape=jax.ShapeDtypeStruct((n, D_FF), BF16),
        grid=(n // UP_TM, UP_NJ),
        in_specs=[
            pl.BlockSpec((HALO_FETCH, D_MODEL), lambda i, j: (jnp.maximum(i * halo_blocks - 1, 0), 0)),
            pl.BlockSpec((UP_TM, D_MODEL), lambda i, j: (i, 0)),
            pl.BlockSpec((HALO_FETCH, D_MODEL),
                         lambda i, j: (jnp.minimum((i + 1) * halo_blocks, last_halo), 0)),
            pl.BlockSpec((D_MODEL, UP_TN), lambda i, j: (0, j)),
            pl.BlockSpec((D_MODEL, UP_TN), lambda i, j: (0, UP_NJ + j)),
            pl.BlockSpec((3, UP_TN), lambda i, j: (0, j)),
            pl.BlockSpec((3, UP_TN), lambda i, j: (0, UP_NJ + j)),
            pl.BlockSpec((1, UP_TN), lambda i, j: (0, j)),
            pl.BlockSpec((1, UP_TN), lambda i, j: (0, UP_NJ + j)),
        ],
        out_specs=pl.BlockSpec((UP_TM, UP_TN), lambda i, j: (i, j)),
        scratch_shapes=[pltpu.VMEM((UP_ROWS, D_MODEL), BF16),
                        pltpu.VMEM((2, UP_ROWS, SUB), F32)],
        compiler_params=_params("arbitrary", "arbitrary"),
        name="ffn_up",
    )(h2, h2, h2, w_up, w_up, conv_w, conv_w, conv_b, conv_b)


DN_TM = 512


def _ffn_down_kernel(a_ref, w_ref, x_ref, o_ref):
    lhs = a_ref[...]
    for k in range(D_MODEL // SUB):
        cols = slice(k * SUB, (k + 1) * SUB)
        o_ref[:, cols] = x_ref[:, cols] + jnp.dot(lhs, w_ref[:, cols], preferred_element_type=F32)


def _ffn_down(act, w_down, x1):
    n = act.shape[0]
    row_block = lambda width: pl.BlockSpec((DN_TM, width), lambda i: (i, 0))
    return pl.pallas_call(
        _ffn_down_kernel,
        out_shape=jax.ShapeDtypeStruct((n, D_MODEL), F32),
        grid=(n // DN_TM,),
        in_specs=[row_block(D_FF), _resident((D_FF, D_MODEL)), row_block(D_MODEL)],
        out_specs=row_block(D_MODEL),
        compiler_params=_params("arbitrary"),
        name="ffn_down",
    )(act, w_down, x1)


def _rope_tables():
    half = HEAD_DIM // 2
    inv_freq = ROPE_THETA ** (-np.arange(half, dtype=np.float64) * (2.0 / HEAD_DIM))
    ang = np.arange(SEQ, dtype=np.float64)[:, None] * inv_freq[None, :]
    cos, sin = np.cos(ang).astype(np.float32), np.sin(ang).astype(np.float32)
    return (jnp.asarray(np.concatenate([cos, cos], axis=-1)),
            jnp.asarray(np.concatenate([sin, sin], axis=-1)))


def _paired_gain(gain):
    lo, hi = gain[:HALF_LANES], gain[HALF_LANES:]
    return jnp.stack([jnp.concatenate([lo, lo]), jnp.concatenate([hi, hi])])


IN_COL_VA = ROPE_DIM
IN_COL_QB = IN_COL_VA + A_KV_DIM
IN_COL_VB = IN_COL_QB + NORM_DIM
IN_COL_GATE = IN_COL_VB + B_DIM
IN_HEAD_OUTPUTS = [ROPE_COLUMNS, [(IN_COL_QB, IN_COL_VB)], [(IN_COL_VB, IN_COL_GATE)],
                   [(IN_COL_VA, IN_COL_QB)]]


def kernel(x, norm_mix, w_in, a_q_norm, a_k_norm, a_sink, b_q_norm, b_k_norm, b_rpb,
           w_branch_a, w_branch_b, w_out, norm_ffn, w_up, conv_w, conv_b, w_down):
    batch, seq, d_model = x.shape
    assert (seq, d_model) == (SEQ, D_MODEL)
    n = batch * seq
    cos, sin = _rope_tables()
    scale = LOG2E / math.sqrt(HEAD_DIM)
    x2 = x.reshape(n, d_model)
    for l in range(norm_mix.shape[0]):
        gates, h, w_rope, w_norm, w_vb, w_va = _in_gate(
            x2, norm_mix[l][None], w_in[l], IN_COL_GATE, [(w_in[l], IN_HEAD_OUTPUTS)])
        gains = (_paired_gain(a_q_norm[l] * scale), _paired_gain(a_k_norm[l]),
                 b_q_norm[l][None] * scale, b_k_norm[l][None])
        qk_a, qk_b, vals = _in_heads(h, (w_rope, w_norm, w_vb, w_va), cos, sin, gains, [])
        out_a, wa_bf, wb_bf, w_out_bf, w_down_bf, w_up_bf = _attn_a(
            qk_a, vals, a_sink[l] * LOG2E, batch,
            [w_branch_a[l], w_branch_b[l], w_out[l], w_down[l], w_up[l]])
        out_b = _attn_b(qk_b.reshape(n // GRID_W, GRID_W, NORM_DIM),
                        vals.reshape(n // GRID_W, GRID_W, PLAIN_DIM), _attn_b_strips(b_rpb[l]), batch)
        x1, h2 = _merge(x2, out_a, out_b.reshape(n, B_DIM), gates, wa_bf, wb_bf, w_out_bf,
                        norm_ffn[l][None])
        act = _ffn_up(h2, w_up_bf, conv_w[l], conv_b[l][None])
        x2 = _ffn_down(act, w_down_bf, x1)
    return x2.reshape(batch, seq, d_model)
```
